```python
import math
import jax, jax.numpy as jnp
from jax import lax
import numpy as np

D_MODEL = 1024
BATCH = 2
SEQ = 16384
DEPTH = 2

GRID_W = 64
CTX_LEN = 256
HEAD_DIM = 64
BLOCK = 128
WINDOW = 128
ROPE_BASE = 10000.0
EPS = 1e-6
NEG = -1e30
GROUP_W = D_MODEL // 4
A_HEADS = GROUP_W // HEAD_DIM
A_QK_DIM = HEAD_DIM // 2
A_V_DIM = HEAD_DIM
A_W = A_HEADS * A_V_DIM
B_CH = GROUP_W
CONV_W = 3
W_Q_HEADS = GROUP_W // HEAD_DIM
W_KV_HEADS = W_Q_HEADS // 2
W_GROUP = W_Q_HEADS // W_KV_HEADS
W_W = W_Q_HEADS * HEAD_DIM
G_Q_HEADS = GROUP_W // HEAD_DIM
G_KV_HEADS = G_Q_HEADS // 2
G_GROUP = G_Q_HEADS // G_KV_HEADS
G_W = G_Q_HEADS * HEAD_DIM
MIX_W = A_W + B_CH + W_W + G_W
SPLIT_SIZES = (A_HEADS * 2 * A_QK_DIM, A_HEADS * 2 * A_QK_DIM, A_HEADS * A_V_DIM,
               B_CH, B_CH, B_CH,
               W_Q_HEADS * HEAD_DIM, W_KV_HEADS * HEAD_DIM, W_KV_HEADS * HEAD_DIM,
               G_Q_HEADS * HEAD_DIM, G_KV_HEADS * HEAD_DIM, G_KV_HEADS * HEAD_DIM)
IN_W = sum(SPLIT_SIZES)
D_FF = 256 * ((8 * D_MODEL // 3 + 255) // 256)
N_MOD = 9

kernel_name = "hybrid_parallel_group_dit_block"


def _layer_norm(x, g, b):
    xf = x.astype(jnp.float32)
    mu = jnp.mean(xf, axis=-1, keepdims=True)
    var = jnp.mean(jnp.square(xf - mu), axis=-1, keepdims=True)
    return ((xf - mu) * lax.rsqrt(var + EPS) * g + b).astype(x.dtype)


def _rms_norm(x, g):
    xf = x.astype(jnp.float32)
    return (xf * lax.rsqrt(jnp.mean(xf * xf, axis=-1, keepdims=True) + EPS) * g).astype(x.dtype)


def _modulate(s, shift, scale):
    return s * (1.0 + scale) + shift


def _swiglu(h, w_gu, w_dn):
    g, u = jnp.split(h @ w_gu, 2, axis=-1)
    return (jax.nn.silu(g) * u) @ w_dn


def _ffn_step(s, shift, scale, gate, w_gu, w_dn, g, b, alpha):
    h = _modulate(s, shift, scale)
    return _layer_norm(alpha * s + 0.5 * gate * _swiglu(h, w_gu, w_dn), g, b)


def _rope_tables(rows, dim):
    n_freq = dim // 4
    inv_freq = ROPE_BASE ** (-jnp.arange(n_freq, dtype=jnp.float32) / n_freq)
    ang_r = jnp.arange(rows, dtype=jnp.float32)[:, None] * inv_freq
    ang_c = jnp.arange(GRID_W, dtype=jnp.float32)[:, None] * inv_freq
    ang_r = jnp.broadcast_to(ang_r[:, None, :], (rows, GRID_W, n_freq))
    ang_c = jnp.broadcast_to(ang_c[None, :, :], (rows, GRID_W, n_freq))
    ang = jnp.concatenate([ang_r, ang_r, ang_c, ang_c], axis=-1).reshape(rows * GRID_W, dim)
    return jnp.cos(ang), jnp.sin(ang)


def _apply_rope(x, cos, sin):
    shape = (cos.shape[0],) + (1,) * (x.ndim - 3) + (cos.shape[1],)
    a, b, c, d = jnp.split(x, 4, axis=-1)
    rot = jnp.concatenate([-b, a, -d, c], axis=-1)
    return (x * cos.reshape(shape) + rot * sin.reshape(shape)).astype(x.dtype)


def _split_cols(z):
    idx = np.cumsum(SPLIT_SIZES)[:-1].tolist()
    return jnp.split(z, idx, axis=-1)


def _gqa_softmax(q, k, v, sink=None):
    logits = jnp.einsum('bqhgd,bkhd->bhgqk', q, k).astype(jnp.float32) * (HEAD_DIM ** -0.5)
    if sink is None:
        p = jax.nn.softmax(logits, axis=-1)
    else:
        s = jnp.broadcast_to(sink.astype(jnp.float32).reshape((1,) + q.shape[2:4] + (1, 1)),
                             logits.shape[:-1] + (1,))
        p = jax.nn.softmax(jnp.concatenate([logits, s], axis=-1), axis=-1)[..., :-1]
    return jnp.einsum('bhgqk,bkhd->bqhgd', p.astype(v.dtype), v)


def _diff_softmax(q, k, v, lam):
    logits = jnp.einsum('bqhmd,bkhmd->bhmqk', q, k).astype(jnp.float32) * (A_QK_DIM ** -0.5)
    p = jax.nn.softmax(logits, axis=-1)
    w = p[:, :, 0] - lam * p[:, :, 1]
    return jnp.einsum('bhqk,bkhd->bqhd', w.astype(v.dtype), v)


def _diff_attention(zx, zc, lam, lambda_init, subln_w, rope, need_ctx):
    aq, ak, av = zx
    cq, ck, cv = zc
    B, S = aq.shape[:2]
    L = cq.shape[1]
    nb = S // BLOCK
    q = _apply_rope(aq.reshape(B, S, A_HEADS, 2, A_QK_DIM), *rope)
    k = _apply_rope(ak.reshape(B, S, A_HEADS, 2, A_QK_DIM), *rope)
    v = av.reshape(B, S, A_HEADS, A_V_DIM)
    qc = cq.reshape(B, L, A_HEADS, 2, A_QK_DIM)
    kc = ck.reshape(B, L, A_HEADS, 2, A_QK_DIM)
    vc = cv.reshape(B, L, A_HEADS, A_V_DIM)
    k_all = jnp.concatenate([kc, k], axis=1)
    v_all = jnp.concatenate([vc, v], axis=1)
    qb = jnp.moveaxis(q.reshape(B, nb, BLOCK, A_HEADS, 2, A_QK_DIM), 1, 0)
    o = lax.map(lambda qi: _diff_softmax(qi, k_all, v_all, lam), qb)
    o = jnp.moveaxis(o, 0, 1).reshape(B, S, A_HEADS, A_V_DIM)
    y_lat = (_rms_norm(o, subln_w) * (1.0 - lambda_init)).reshape(B, S, A_W)
    y_ctx = None
    if need_ctx:
        oc = _diff_softmax(qc, kc, vc, lam)
        y_ctx = (_rms_norm(oc, subln_w) * (1.0 - lambda_init)).reshape(B, L, A_W)
    return y_lat, y_ctx


def _short_conv(gb, gc, u, conv_w):
    h = gc * u
    y = lax.conv_general_dilated(h, conv_w[:, None, :].astype(h.dtype), window_strides=(1,),
                                 padding=[(CONV_W // 2, CONV_W // 2)],
                                 dimension_numbers=('NWC', 'WIO', 'NWC'),
                                 feature_group_count=B_CH)
    return gb * y


def _banded_sink_attention(q, k, v, kc, vc, sink):
    B, S = q.shape[:2]
    L = kc.shape[1]
    nb = S // BLOCK
    qb = q.reshape(B, nb, BLOCK, W_KV_HEADS, W_GROUP, HEAD_DIM)
    pad = ((0, 0), (BLOCK, BLOCK), (0, 0), (0, 0))
    kp = jnp.pad(k, pad).reshape(B, nb + 2, BLOCK, W_KV_HEADS, HEAD_DIM)
    vp = jnp.pad(v, pad).reshape(B, nb + 2, BLOCK, W_KV_HEADS, HEAD_DIM)
    kw = jnp.concatenate([kp[:, :-2], kp[:, 1:-1], kp[:, 2:]], axis=2)
    vw = jnp.concatenate([vp[:, :-2], vp[:, 1:-1], vp[:, 2:]], axis=2)
    kpos = jnp.arange(3 * BLOCK) - BLOCK
    qpos = jnp.arange(BLOCK)
    rel = kpos[None, :] - qpos[:, None]
    absk = jnp.arange(nb)[:, None, None] * BLOCK + kpos[None, None, :]
    valid = (jnp.abs(rel) <= WINDOW)[None] & (absk >= 0) & (absk < S)
    scale = HEAD_DIM ** -0.5
    lw = jnp.einsum('bnqhgd,bnkhd->bhgnqk', qb, kw).astype(jnp.float32) * scale
    lw = jnp.where(valid, lw, NEG)
    lc = jnp.einsum('bnqhgd,bchd->bhgnqc', qb, kc).astype(jnp.float32) * scale
    ls = jnp.broadcast_to(sink.astype(jnp.float32).reshape(1, W_KV_HEADS, W_GROUP, 1, 1, 1),
                          lw.shape[:-1] + (1,))
    p = jax.nn.softmax(jnp.concatenate([lw, lc, ls], axis=-1), axis=-1)
    pw = p[..., :3 * BLOCK].astype(v.dtype)
    pc = p[..., 3 * BLOCK:3 * BLOCK + L].astype(v.dtype)
    o = (jnp.einsum('bhgnqk,bnkhd->bnqhgd', pw, vw)
         + jnp.einsum('bhgnqc,bchd->bnqhgd', pc, vc))
    return o.reshape(B, S, W_W)


def _window_attention(zx, zc, sink, rope, need_ctx):
    q, k, v = zx
    qc, kc, vc = zc
    B, S = q.shape[:2]
    L = qc.shape[1]
    q = _apply_rope(q.reshape(B, S, W_KV_HEADS, W_GROUP, HEAD_DIM), *rope)
    k = _apply_rope(k.reshape(B, S, W_KV_HEADS, HEAD_DIM), *rope)
    v = v.reshape(B, S, W_KV_HEADS, HEAD_DIM)
    qc = qc.reshape(B, L, W_KV_HEADS, W_GROUP, HEAD_DIM)
    kc = kc.reshape(B, L, W_KV_HEADS, HEAD_DIM)
    vc = vc.reshape(B, L, W_KV_HEADS, HEAD_DIM)
    y_lat = _banded_sink_attention(q, k, v, kc, vc, sink)
    y_ctx = _gqa_softmax(qc, kc, vc, sink).reshape(B, L, W_W) if need_ctx else None
    return y_lat, y_ctx


def _global_attention(zx, zc, qn_w, kn_w, rope, need_ctx):
    q, k, v = zx
    qc, kc, vc = zc
    B, S = q.shape[:2]
    L = qc.shape[1]
    nb = S // BLOCK
    q = _apply_rope(_rms_norm(q.reshape(B, S, G_KV_HEADS, G_GROUP, HEAD_DIM), qn_w), *rope)
    k = _apply_rope(_rms_norm(k.reshape(B, S, G_KV_HEADS, HEAD_DIM), kn_w), *rope)
    v = v.reshape(B, S, G_KV_HEADS, HEAD_DIM)
    qc = _rms_norm(qc.reshape(B, L, G_KV_HEADS, G_GROUP, HEAD_DIM), qn_w)
    kc = _rms_norm(kc.reshape(B, L, G_KV_HEADS, HEAD_DIM), kn_w)
    vc = vc.reshape(B, L, G_KV_HEADS, HEAD_DIM)
    k_all = jnp.concatenate([kc, k], axis=1)
    v_all = jnp.concatenate([vc, v], axis=1)
    qb = jnp.moveaxis(q.reshape(B, nb, BLOCK, G_KV_HEADS, G_GROUP, HEAD_DIM), 1, 0)
    o = lax.map(lambda qi: _gqa_softmax(qi, k_all, v_all), qb)
    y_lat = jnp.moveaxis(o, 0, 1).reshape(B, S, G_W)
    y_ctx = _gqa_softmax(qc, kc, vc).reshape(B, L, G_W) if need_ctx else None
    return y_lat, y_ctx


def setup_inputs(seed: int = 0) -> dict:
    key = jax.random.key(seed)
    ks = jax.random.split(key, 24)
    beta = (8.0 * DEPTH) ** -0.25

    def nrm(k, shape, s):
        return jax.random.normal(k, shape, jnp.float32) * s

    return {
        "x": nrm(ks[0], (BATCH, SEQ, D_MODEL), 1.0),
        "c": nrm(ks[1], (BATCH, D_MODEL), 1.0),
        "ctx": nrm(ks[2], (BATCH, CTX_LEN, D_MODEL), 1.0),
        "c_ctx": nrm(ks[3], (D_MODEL,), 1.0),
        "w_mod": nrm(ks[4], (DEPTH, D_MODEL, N_MOD * D_MODEL), 0.5 * D_MODEL ** -0.5),
        "b_mod": nrm(ks[5], (DEPTH, N_MOD * D_MODEL), 0.02),
        "ln_g": 1.0 + nrm(ks[6], (DEPTH, 3, D_MODEL), 0.02),
        "ln_b": nrm(ks[7], (DEPTH, 3, D_MODEL), 0.02),
        "w_gu1": nrm(ks[8], (DEPTH, D_MODEL, 2 * D_FF), D_MODEL ** -0.5),
        "w_dn1": nrm(ks[9], (DEPTH, D_FF, D_MODEL), beta * D_FF ** -0.5),
        "w_in": nrm(ks[10], (DEPTH, D_MODEL, IN_W), D_MODEL ** -0.5),
        "w_out": nrm(ks[11], (DEPTH, MIX_W, D_MODEL), beta * MIX_W ** -0.5),
        "conv_w": nrm(ks[12], (DEPTH, CONV_W, B_CH), CONV_W ** -0.5),
        "lam_q1": nrm(ks[13], (DEPTH, A_QK_DIM), 0.1),
        "lam_k1": nrm(ks[14], (DEPTH, A_QK_DIM), 0.1),
        "lam_q2": nrm(ks[15], (DEPTH, A_QK_DIM), 0.1),
        "lam_k2": nrm(ks[16], (DEPTH, A_QK_DIM), 0.1),
        "subln_w": 1.0 + nrm(ks[17], (DEPTH, A_V_DIM), 0.02),
        "sink": nrm(ks[18], (DEPTH, W_Q_HEADS), 0.5),
        "qn_w": 1.0 + nrm(ks[19], (DEPTH, HEAD_DIM), 0.02),
        "kn_w": 1.0 + nrm(ks[20], (DEPTH, HEAD_DIM), 0.02),
        "w_gu2": nrm(ks[21], (DEPTH, D_MODEL, 2 * D_FF), D_MODEL ** -0.5),
        "w_dn2": nrm(ks[22], (DEPTH, D_FF, D_MODEL), beta * D_FF ** -0.5),
    }


def reference(x, c, ctx, c_ctx, w_mod, b_mod, ln_g, ln_b, w_gu1, w_dn1, w_in, w_out, conv_w,
              lam_q1, lam_k1, lam_q2, lam_k2, subln_w, sink, qn_w, kn_w, w_gu2, w_dn2):
    B, S, _ = x.shape
    rows = S // GRID_W
    rope_a = _rope_tables(rows, A_QK_DIM)
    rope_h = _rope_tables(rows, HEAD_DIM)
    alpha = (2.0 * DEPTH) ** 0.25
    xc = ctx
    for l in range(DEPTH):
        need_ctx = l < DEPTH - 1
        lambda_init = 0.8 - 0.6 * math.exp(-0.3 * l)
        mx = jnp.split((jax.nn.silu(c) @ w_mod[l] + b_mod[l])[:, None, :], N_MOD, axis=-1)
        mc = jnp.split((jax.nn.silu(c_ctx) @ w_mod[l] + b_mod[l])[None, None, :], N_MOD, axis=-1)

        x = _ffn_step(x, mx[0], mx[1], mx[2], w_gu1[l], w_dn1[l], ln_g[l, 0], ln_b[l, 0], alpha)
        xc = _ffn_step(xc, mc[0], mc[1], mc[2], w_gu1[l], w_dn1[l], ln_g[l, 0], ln_b[l, 0], alpha)

        zx = _split_cols(_modulate(x, mx[3], mx[4]) @ w_in[l])
        zc = _split_cols(_modulate(xc, mc[3], mc[4]) @ w_in[l])
        lam = (jnp.exp(jnp.sum(lam_q1[l] * lam_k1[l])) - jnp.exp(jnp.sum(lam_q2[l] * lam_k2[l]))
               + lambda_init).astype(jnp.float32)
        ya_x, ya_c = _diff_attention(zx[0:3], zc[0:3], lam, lambda_init, subln_w[l], rope_a, need_ctx)
        yb_x = _short_conv(zx[3], zx[4], zx[5], conv_w[l])
        yw_x, yw_c = _window_attention(zx[6:9], zc[6:9], sink[l], rope_h, need_ctx)
        yg_x, yg_c = _global_attention(zx[9:12], zc[9:12], qn_w[l], kn_w[l], rope_h, need_ctx)
        y_lat = jnp.concatenate([ya_x, yb_x, yw_x, yg_x], axis=-1) @ w_out[l]
        x = _layer_norm(alpha * x + mx[5] * y_lat, ln_g[l, 1], ln_b[l, 1])

        x = _ffn_step(x, mx[6], mx[7], mx[8], w_gu2[l], w_dn2[l], ln_g[l, 2], ln_b[l, 2], alpha)

        if need_ctx:
            yb_c = _short_conv(zc[3], zc[4], zc[5], conv_w[l])
            y_ctx = jnp.concatenate([ya_c, yb_c, yw_c, yg_c], axis=-1) @ w_out[l]
            xc = _layer_norm(alpha * xc + mc[5] * y_ctx, ln_g[l, 1], ln_b[l, 1])
            xc = _ffn_step(xc, mc[6], mc[7], mc[8], w_gu2[l], w_dn2[l], ln_g[l, 2], ln_b[l, 2], alpha)
    return x
```

```python
import functools
import math

import jax
import jax.numpy as jnp
from jax import lax
from jax.experimental import pallas as pl
from jax.experimental.pallas import tpu as pltpu

F32 = jnp.float32
BF16 = jnp.bfloat16

GRID_W = 64
HEAD_DIM = 64
WINDOW = 128
ROPE_BASE = 10000.0
EPS = 1e-6
NEG = -1e30
A_QK_DIM = 32
N_MOD = 9
MOD_ROWS = 8
VMEM_LIMIT = 56 * 1024 * 1024

OFF_AQ, OFF_AK, OFF_AV = 0, 256, 512
OFF_B = 768
OFF_CQ, OFF_CK, OFF_CV = 1536, 1792, 1920
OFF_DQ, OFF_DK, OFF_DV = 2048, 2304, 2432
IN_W = 2560

TOK_TILE = 512
Q_TILE = 256
FF_CHUNK = 256


def _params(*sem):
    return pltpu.CompilerParams(dimension_semantics=sem, vmem_limit_bytes=VMEM_LIMIT)


def _layer_norm(r, g, b):
    mu = jnp.mean(r, axis=-1, keepdims=True)
    d = r - mu
    var = jnp.mean(d * d, axis=-1, keepdims=True)
    return d * lax.rsqrt(var + EPS) * g + b


def _mod_kernel(c_ref, w_ref, b_ref, o_ref):
    c = c_ref[...]
    a = (c * jax.nn.sigmoid(c)).astype(BF16)
    o_ref[0] = jnp.dot(a, w_ref[0].astype(BF16), preferred_element_type=F32) + b_ref[0]


def _mod_vectors(cvec, w_mod, b_mod):
    depth, d, n = w_mod.shape
    tn = n // 8
    return pl.pallas_call(
        _mod_kernel,
        grid=(depth, n // tn),
        in_specs=[
            pl.BlockSpec((MOD_ROWS, d), lambda l, j: (0, 0)),
            pl.BlockSpec((1, d, tn), lambda l, j: (l, 0, j)),
            pl.BlockSpec((1, 1, tn), lambda l, j: (l, 0, j)),
        ],
        out_specs=pl.BlockSpec((1, MOD_ROWS, tn), lambda l, j: (l, 0, j)),
        out_shape=jax.ShapeDtypeStruct((depth, MOD_ROWS, n), F32),
        compiler_params=_params("parallel", "parallel"),
        name="mod_vectors",
    )(cvec, w_mod, b_mod.reshape(depth, 1, n))


def _ffn_kernel(x_ref, mod_ref, wgu_ref, wdn_ref, g_ref, b_ref, o_ref, acc_ref, *, mod_base, alpha):
    x = x_ref[0]
    shift = mod_ref[0, mod_base:mod_base + 1, :]
    scale = mod_ref[0, mod_base + 1:mod_base + 2, :]
    gate = mod_ref[0, mod_base + 2:mod_base + 3, :]
    h = (x * (1.0 + scale) + shift).astype(BF16)
    d_ff = wdn_ref.shape[0]
    for c in range(d_ff // FF_CHUNK):
        lo = c * FF_CHUNK
        g = jnp.dot(h, wgu_ref[:, lo:lo + FF_CHUNK], preferred_element_type=F32)
        u = jnp.dot(h, wgu_ref[:, d_ff + lo:d_ff + lo + FF_CHUNK], preferred_element_type=F32)
        a = (g * jax.nn.sigmoid(g) * u).astype(BF16)
        y = jnp.dot(a, wdn_ref[lo:lo + FF_CHUNK, :], preferred_element_type=F32)
        if c == 0:
            acc_ref[...] = y
        else:
            acc_ref[...] += y
    r = alpha * x + (0.5 * gate) * acc_ref[...]
    o_ref[0] = _layer_norm(r, g_ref[...], b_ref[...])


def _ffn_step(x, mods, mod_row, mod_base, w_gu, w_dn, ln_g, ln_b, alpha):
    bsz, t, d = x.shape
    d_ff = w_dn.shape[0]
    tm = min(TOK_TILE, t)
    const = lambda b, i: (0, 0)
    return pl.pallas_call(
        functools.partial(_ffn_kernel, mod_base=mod_base, alpha=alpha),
        grid=(bsz, t // tm),
        in_specs=[
            pl.BlockSpec((1, tm, d), lambda b, i: (b, i, 0)),
            pl.BlockSpec((1, N_MOD, d), lambda b, i: (mod_row(b), 0, 0)),
            pl.BlockSpec((d, 2 * d_ff), const, pipeline_mode=pl.Buffered(1)),
            pl.BlockSpec((d_ff, d), const, pipeline_mode=pl.Buffered(1)),
            pl.BlockSpec((1, d), const),
            pl.BlockSpec((1, d), const),
        ],
        out_specs=pl.BlockSpec((1, tm, d), lambda b, i: (b, i, 0)),
        out_shape=jax.ShapeDtypeStruct(x.shape, F32),
        scratch_shapes=[pltpu.VMEM((tm, d), F32)],
        compiler_params=_params("parallel", "parallel"),
        name="ffn_step",
    )(x, mods, w_gu, w_dn, ln_g.reshape(1, d), ln_b.reshape(1, d))


def _rope(x, cos, sin_next, sin_prev, quarter):
    width = x.shape[1]
    reps = width // cos.shape[1]
    if reps > 1:
        cos = jnp.concatenate([cos] * reps, axis=1)
        sin_next = jnp.concatenate([sin_next] * reps, axis=1)
        sin_prev = jnp.concatenate([sin_prev] * reps, axis=1)
    x_next = pltpu.roll(x, width - quarter, axis=1)
    x_prev = pltpu.roll(x, quarter, axis=1)
    return x * cos + x_next * sin_next + x_prev * sin_prev


def _head_rms(x, gsum_ref, w):
    x2 = x * x
    hi = x2.astype(BF16)
    lo = (x2 - hi.astype(F32)).astype(BF16)
    gs = gsum_ref[:x.shape[1], :x.shape[1]]
    ss = jnp.dot(hi, gs, preferred_element_type=F32) + jnp.dot(lo, gs, preferred_element_type=F32)
    return x * lax.rsqrt(ss * (1.0 / HEAD_DIM) + EPS) * w


def _inproj_kernel(x_ref, mod_ref, w_ref, ca_ref, san_ref, sap_ref, ch_ref, shn_ref, shp_ref,
                   qnw_ref, knw_ref, gsum_ref,
                   qa_ref, ka_ref, vat_ref, g3_ref, qc_ref, kc_ref, vct_ref, qd_ref, kd_ref, vdt_ref):
    x = x_ref[0]
    shift = mod_ref[0, 3:4, :]
    scale = mod_ref[0, 4:5, :]
    h = (x * (1.0 + scale) + shift).astype(BF16)

    def seg(lo, hi):
        return jnp.dot(h, w_ref[:, lo:hi], preferred_element_type=F32)

    rope_a = functools.partial(_rope, cos=ca_ref[...], sin_next=san_ref[...], sin_prev=sap_ref[...],
                               quarter=A_QK_DIM // 4)
    rope_h = functools.partial(_rope, cos=ch_ref[...], sin_next=shn_ref[...], sin_prev=shp_ref[...],
                               quarter=HEAD_DIM // 4)
    qa_ref[0] = (rope_a(seg(OFF_AQ, OFF_AK)) * (A_QK_DIM ** -0.5)).astype(BF16)
    ka_ref[0, 0] = rope_a(seg(OFF_AK, OFF_AV)).astype(BF16)
    vat_ref[0, 0] = seg(OFF_AV, OFF_B).T.astype(BF16)
    g3_ref[0] = seg(OFF_B, OFF_CQ)
    qc_ref[0] = (rope_h(seg(OFF_CQ, OFF_CK)) * (HEAD_DIM ** -0.5)).astype(BF16)
    kc_ref[0, 0] = rope_h(seg(OFF_CK, OFF_CV)).astype(BF16)
    vct_ref[0, 0] = seg(OFF_CV, OFF_DQ).T.astype(BF16)
    qd = _head_rms(seg(OFF_DQ, OFF_DK), gsum_ref, qnw_ref[...])
    qd_ref[0] = (rope_h(qd) * (HEAD_DIM ** -0.5)).astype(BF16)
    kd = _head_rms(seg(OFF_DK, OFF_DV), gsum_ref, knw_ref[...])
    kd_ref[0, 0] = rope_h(kd).astype(BF16)
    vdt_ref[0, 0] = seg(OFF_DV, IN_W).T.astype(BF16)


def _in_projection(x, mods, mod_row, w_in, tabs_a, tabs_h, qnw, knw, gsum):
    bsz, t, d = x.shape
    tm = min(TOK_TILE, t)
    nc = t // tm
    const = lambda b, i: (0, 0)
    tab = pl.BlockSpec((tm, 128), lambda b, i: (i, 0))
    q_spec = pl.BlockSpec((1, tm, 256), lambda b, i: (b, i, 0))

    def k_spec(w):
        return pl.BlockSpec((1, 1, tm, w), lambda b, i: (b, i, 0, 0))

    def vt_spec(w):
        return pl.BlockSpec((1, 1, w, tm), lambda b, i: (b, i, 0, 0))

    def q_shape():
        return jax.ShapeDtypeStruct((bsz, t, 256), BF16)

    def k_shape(w):
        return jax.ShapeDtypeStruct((bsz, nc, tm, w), BF16)

    def vt_shape(w):
        return jax.ShapeDtypeStruct((bsz, nc, w, tm), BF16)

    return pl.pallas_call(
        _inproj_kernel,
        grid=(bsz, nc),
        in_specs=[
            pl.BlockSpec((1, tm, d), lambda b, i: (b, i, 0)),
            pl.BlockSpec((1, N_MOD, d), lambda b, i: (mod_row(b), 0, 0)),
            pl.BlockSpec((d, IN_W), const, pipeline_mode=pl.Buffered(1)),
            tab, tab, tab, tab, tab, tab,
            pl.BlockSpec((1, 256), const),
            pl.BlockSpec((1, 128), const),
            pl.BlockSpec((256, 256), const),
        ],
        out_specs=[q_spec, k_spec(256), vt_spec(256),
                   pl.BlockSpec((1, tm, 768), lambda b, i: (b, i, 0)),
                   q_spec, k_spec(128), vt_spec(128),
                   q_spec, k_spec(128), vt_spec(128)],
        out_shape=[q_shape(), k_shape(256), vt_shape(256),
                   jax.ShapeDtypeStruct((bsz, t, 768), F32),
                   q_shape(), k_shape(128), vt_shape(128),
                   q_shape(), k_shape(128), vt_shape(128)],
        compiler_params=_params("parallel", "parallel"),
        name="in_projection",
    )(x, mods, w_in, *tabs_a, *tabs_h, qnw, knw, gsum)


def _fill_qpad(q_ref, qpad_ref, n_maps, dq, kw):
    tq = q_ref.shape[1]
    qt = q_ref[0].astype(F32).T
    for j in range(n_maps):
        row = j * dq if kw == qt.shape[0] else (j // 2) * dq
        parts = []
        if row:
            parts.append(jnp.zeros((row, tq), F32))
        parts.append(qt[j * dq:(j + 1) * dq, :])
        if kw - row - dq:
            parts.append(jnp.zeros((kw - row - dq, tq), F32))
        qpad_ref[j] = jnp.concatenate(parts, axis=0).astype(BF16)


def _softmax_step(k_c, vt_c, qpad_ref, m_ref, l_ref, acc_ref, n_maps, mask=None):
    for j in range(n_maps):
        s = jnp.dot(k_c, qpad_ref[j], preferred_element_type=F32)
        if mask is not None:
            s = jnp.where(mask, s, NEG)
        m_old = m_ref[j]
        m_new = jnp.maximum(m_old, jnp.max(s, axis=0, keepdims=True))
        p = jnp.exp(s - m_new)
        alpha = jnp.exp(m_old - m_new)
        l_ref[j] = alpha * l_ref[j] + jnp.sum(p, axis=0, keepdims=True)
        vb = (j // 2) * HEAD_DIM
        pv = jnp.dot(vt_c[vb:vb + HEAD_DIM, :], p.astype(BF16), preferred_element_type=F32)
        acc_ref[j] = alpha * acc_ref[j] + pv
        m_ref[j] = m_new


def _init_state(m_ref, l_ref, acc_ref, n_maps, sink_ref):
    for j in range(n_maps):
        if sink_ref is None:
            m_ref[j] = jnp.full(m_ref.shape[1:], NEG, F32)
            l_ref[j] = jnp.zeros(l_ref.shape[1:], F32)
        else:
            m_ref[j] = jnp.full(m_ref.shape[1:], sink_ref[j], F32)
            l_ref[j] = jnp.ones(l_ref.shape[1:], F32)
        acc_ref[j] = jnp.zeros(acc_ref.shape[1:], F32)


def _finish_plain(o_ref, l_ref, acc_ref, n_maps):
    ot = jnp.concatenate([acc_ref[j] / l_ref[j] for j in range(n_maps)], axis=0)
    o_ref[0] = ot.T.astype(BF16)


def _finish_diff(o_ref, l_ref, acc_ref, lamp_ref, subw_ref, n_maps, lambda_init):
    lp = lamp_ref[...]
    lam = (jnp.exp(jnp.sum(lp[0:1] * lp[1:2], axis=1, keepdims=True))
           - jnp.exp(jnp.sum(lp[2:3] * lp[3:4], axis=1, keepdims=True)) + lambda_init)
    outs = []
    for hd in range(n_maps // 2):
        o = acc_ref[2 * hd] / l_ref[2 * hd] - lam * (acc_ref[2 * hd + 1] / l_ref[2 * hd + 1])
        ms = jnp.mean(o * o, axis=0, keepdims=True)
        outs.append(o * lax.rsqrt(ms + EPS) * subw_ref[...] * (1.0 - lambda_init))
    o_ref[0] = jnp.concatenate(outs, axis=0).T.astype(BF16)


def _dense_attn_kernel(*refs, n_maps, dq, kw, has_latent, has_sink, diff_lambda_init):
    refs = list(refs)
    q_ref, kc_ref, vct_ref = refs[:3]
    pos = 3
    k_ref = vt_ref = sink_ref = lamp_ref = subw_ref = None
    if has_latent:
        k_ref, vt_ref = refs[pos:pos + 2]
        pos += 2
    if has_sink:
        sink_ref = refs[pos]
        pos += 1
    if diff_lambda_init is not None:
        lamp_ref, subw_ref = refs[pos:pos + 2]
        pos += 2
    o_ref, qpad_ref, m_ref, l_ref, acc_ref = refs[pos:]

    _fill_qpad(q_ref, qpad_ref, n_maps, dq, kw)
    _init_state(m_ref, l_ref, acc_ref, n_maps, sink_ref)
    for c in range(kc_ref.shape[1]):
        _softmax_step(kc_ref[0, c], vct_ref.at[0, c], qpad_ref, m_ref, l_ref, acc_ref, n_maps)
    if has_latent:
        def body(c, carry):
            _softmax_step(k_ref[0, c], vt_ref.at[0, c], qpad_ref, m_ref, l_ref, acc_ref, n_maps)
            return carry
        lax.fori_loop(0, k_ref.shape[1], body, 0)
    if diff_lambda_init is None:
        _finish_plain(o_ref, l_ref, acc_ref, n_maps)
    else:
        _finish_diff(o_ref, l_ref, acc_ref, lamp_ref, subw_ref, n_maps, diff_lambda_init)


def _dense_attention(q, k_ctx, vt_ctx, k_lat, vt_lat, *, n_maps, dq, sink=None, lamp=None, subw=None,
                     diff_lambda_init=None):
    bsz, t, qw = q.shape
    kw = k_ctx.shape[-1]
    vw = vt_ctx.shape[-2]
    tq = min(Q_TILE, t)
    has_latent = k_lat is not None

    def whole(arr):
        return pl.BlockSpec((1,) + arr.shape[1:], lambda b, i: (b, 0, 0, 0), pipeline_mode=pl.Buffered(1))

    in_specs = [pl.BlockSpec((1, tq, qw), lambda b, i: (b, i, 0)), whole(k_ctx), whole(vt_ctx)]
    args = [q, k_ctx, vt_ctx]
    if has_latent:
        in_specs += [whole(k_lat), whole(vt_lat)]
        args += [k_lat, vt_lat]
    if sink is not None:
        in_specs.append(pl.BlockSpec(memory_space=pltpu.SMEM))
        args.append(sink)
    if diff_lambda_init is not None:
        in_specs += [pl.BlockSpec(lamp.shape, lambda b, i: (0, 0)), pl.BlockSpec(subw.shape, lambda b, i: (0, 0))]
        args += [lamp, subw]
    return pl.pallas_call(
        functools.partial(_dense_attn_kernel, n_maps=n_maps, dq=dq, kw=kw, has_latent=has_latent,
                          has_sink=sink is not None, diff_lambda_init=diff_lambda_init),
        grid=(bsz, t // tq),
        in_specs=in_specs,
        out_specs=pl.BlockSpec((1, tq, 256), lambda b, i: (b, i, 0)),
        out_shape=jax.ShapeDtypeStruct((bsz, t, 256), BF16),
        scratch_shapes=[pltpu.VMEM((n_maps, kw, tq), BF16),
                        pltpu.VMEM((n_maps, 1, tq), F32),
                        pltpu.VMEM((n_maps, 1, tq), F32),
                        pltpu.VMEM((n_maps, HEAD_DIM, tq), F32)],
        compiler_params=_params("parallel", "parallel"),
        name="dense_attention",
    )(*args)


def _window_attn_kernel(q_ref, kp_ref, kc_ref, kn_ref, vtp_ref, vtc_ref, vtn_ref, kctx_ref, vtctx_ref, sink_ref,
                        o_ref, qpad_ref, m_ref, l_ref, acc_ref, *, n_maps, seq):
    tq = q_ref.shape[1]
    q0 = pl.program_id(1) * tq
    _fill_qpad(q_ref, qpad_ref, n_maps, HEAD_DIM, kctx_ref.shape[-1])
    _init_state(m_ref, l_ref, acc_ref, n_maps, sink_ref)
    for c in range(kctx_ref.shape[1]):
        _softmax_step(kctx_ref[0, c], vtctx_ref.at[0, c], qpad_ref, m_ref, l_ref, acc_ref, n_maps)
    k_win = jnp.concatenate([kp_ref[0], kc_ref[0], kn_ref[0]], axis=0)
    vt_win = jnp.concatenate([vtp_ref[0, 0], vtc_ref[0, 0], vtn_ref[0, 0]], axis=1)
    nk = k_win.shape[0]
    kpos = q0 - WINDOW + lax.broadcasted_iota(jnp.int32, (nk, tq), 0)
    qpos = q0 + lax.broadcasted_iota(jnp.int32, (nk, tq), 1)
    valid = (jnp.abs(kpos - qpos) <= WINDOW) & (kpos >= 0) & (kpos < seq)
    _softmax_step(k_win, vt_win, qpad_ref, m_ref, l_ref, acc_ref, n_maps, mask=valid)
    _finish_plain(o_ref, l_ref, acc_ref, n_maps)


def _window_attention(q, k_ctx, vt_ctx, k_lat, vt_lat, sink):
    bsz, t, qw = q.shape
    nc, tk, kw = k_lat.shape[1:]
    tq = Q_TILE
    n_maps = 4
    k_flat = k_lat.reshape(bsz, t, kw)
    wb = tq // WINDOW
    nwb = t // WINDOW
    per_chunk = tk // WINDOW

    def whole(arr):
        return pl.BlockSpec((1,) + arr.shape[1:], lambda b, i: (b, 0, 0, 0), pipeline_mode=pl.Buffered(1))

    def prev_blk(i):
        return jnp.maximum(i * wb - 1, 0)

    def next_blk(i):
        return jnp.minimum(i * wb + wb, nwb - 1)

    in_specs = [
        pl.BlockSpec((1, tq, qw), lambda b, i: (b, i, 0)),
        pl.BlockSpec((1, WINDOW, kw), lambda b, i: (b, prev_blk(i), 0)),
        pl.BlockSpec((1, tq, kw), lambda b, i: (b, i, 0)),
        pl.BlockSpec((1, WINDOW, kw), lambda b, i: (b, next_blk(i), 0)),
        pl.BlockSpec((1, 1, kw, WINDOW), lambda b, i: (b, prev_blk(i) // per_chunk, 0, prev_blk(i) % per_chunk)),
        pl.BlockSpec((1, 1, kw, tq), lambda b, i: (b, (i * tq) // tk, 0, ((i * tq) % tk) // tq)),
        pl.BlockSpec((1, 1, kw, WINDOW), lambda b, i: (b, next_blk(i) // per_chunk, 0, next_blk(i) % per_chunk)),
        whole(k_ctx), whole(vt_ctx),
        pl.BlockSpec(memory_space=pltpu.SMEM),
    ]
    return pl.pallas_call(
        functools.partial(_window_attn_kernel, n_maps=n_maps, seq=t),
        grid=(bsz, t // tq),
        in_specs=in_specs,
        out_specs=pl.BlockSpec((1, tq, 256), lambda b, i: (b, i, 0)),
        out_shape=jax.ShapeDtypeStruct((bsz, t, 256), BF16),
        scratch_shapes=[pltpu.VMEM((n_maps, kw, tq), BF16),
                        pltpu.VMEM((n_maps, 1, tq), F32),
                        pltpu.VMEM((n_maps, 1, tq), F32),
                        pltpu.VMEM((n_maps, HEAD_DIM, tq), F32)],
        compiler_params=_params("parallel", "parallel"),
        name="window_attention",
    )(q, k_flat, k_flat, k_flat, vt_lat, vt_lat, vt_lat, k_ctx, vt_ctx, sink)


def _outproj_kernel(x_ref, mod_ref, ya_ref, g3_ref, g3p_ref, g3n_ref, yw_ref, yg_ref, cw_ref, w_ref, g_ref, b_ref,
                    o_ref, *, alpha):
    i = pl.program_id(1)
    last = pl.num_programs(1) - 1
    tm = x_ref.shape[1]
    gb = g3_ref[0, :, 0:256]
    hid = g3_ref[0, :, 256:512] * g3_ref[0, :, 512:768]
    halo = g3p_ref.shape[1]
    h_prev = g3p_ref[0, halo - 1:halo, 256:512] * g3p_ref[0, halo - 1:halo, 512:768]
    h_next = g3n_ref[0, 0:1, 256:512] * g3n_ref[0, 0:1, 512:768]
    h_prev = jnp.where(i == 0, 0.0, h_prev)
    h_next = jnp.where(i == last, 0.0, h_next)
    row = lax.broadcasted_iota(jnp.int32, hid.shape, 0)
    below = jnp.where(row == 0, h_prev, pltpu.roll(hid, 1, axis=0))
    above = jnp.where(row == tm - 1, h_next, pltpu.roll(hid, tm - 1, axis=0))
    yb = gb * (cw_ref[0:1, :] * below + cw_ref[1:2, :] * hid + cw_ref[2:3, :] * above)
    y = jnp.dot(ya_ref[0], w_ref[0:256, :], preferred_element_type=F32)
    y += jnp.dot(yb.astype(BF16), w_ref[256:512, :], preferred_element_type=F32)
    y += jnp.dot(yw_ref[0], w_ref[512:768, :], preferred_element_type=F32)
    y += jnp.dot(yg_ref[0], w_ref[768:1024, :], preferred_element_type=F32)
    gate = mod_ref[0, 5:6, :]
    o_ref[0] = _layer_norm(alpha * x_ref[0] + gate * y, g_ref[...], b_ref[...])


def _out_projection(x, mods, mod_row, ya, g3, yw, yg, conv_w, w_out, ln_g, ln_b, alpha):
    bsz, t, d = x.shape
    tm = min(TOK_TILE, t)
    halo = 8
    hb = tm // halo
    n_halo = t // halo
    const = lambda b, i: (0, 0)
    y_spec = pl.BlockSpec((1, tm, 256), lambda b, i: (b, i, 0))
    return pl.pallas_call(
        functools.partial(_outproj_kernel, alpha=alpha),
        grid=(bsz, t // tm),
        in_specs=[
            pl.BlockSpec((1, tm, d), lambda b, i: (b, i, 0)),
            pl.BlockSpec((1, N_MOD, d), lambda b, i: (mod_row(b), 0, 0)),
            y_spec,
            pl.BlockSpec((1, tm, 768), lambda b, i: (b, i, 0)),
            pl.BlockSpec((1, halo, 768), lambda b, i: (b, jnp.maximum(i * hb - 1, 0), 0)),
            pl.BlockSpec((1, halo, 768), lambda b, i: (b, jnp.minimum(i * hb + hb, n_halo - 1), 0)),
            y_spec, y_spec,
            pl.BlockSpec(conv_w.shape, const),
            pl.BlockSpec(w_out.shape, const, pipeline_mode=pl.Buffered(1)),
            pl.BlockSpec((1, d), const),
            pl.BlockSpec((1, d), const),
        ],
        out_specs=pl.BlockSpec((1, tm, d), lambda b, i: (b, i, 0)),
        out_shape=jax.ShapeDtypeStruct(x.shape, F32),
        compiler_params=_params("parallel", "parallel"),
        name="out_projection",
    )(x, mods, ya, g3, g3, g3, yw, yg, conv_w, w_out, ln_g.reshape(1, d), ln_b.reshape(1, d))


def _rope_tables(rows, dim):
    n_freq = dim // 4
    inv_freq = ROPE_BASE ** (-jnp.arange(n_freq, dtype=F32) / n_freq)
    ang_r = jnp.arange(rows, dtype=F32)[:, None] * inv_freq
    ang_c = jnp.arange(GRID_W, dtype=F32)[:, None] * inv_freq
    ang_r = jnp.broadcast_to(ang_r[:, None, :], (rows, GRID_W, n_freq))
    ang_c = jnp.broadcast_to(ang_c[None, :, :], (rows, GRID_W, n_freq))
    ang = jnp.concatenate([ang_r, ang_r, ang_c, ang_c], axis=-1).reshape(rows * GRID_W, dim)
    cos, sin = jnp.cos(ang), jnp.sin(ang)
    reps = 128 // dim
    cos = jnp.tile(cos, (1, reps))
    sin = jnp.tile(sin, (1, reps))
    first_half = (jnp.arange(128) % (dim // 2)) < (dim // 4)
    sin_next = jnp.where(first_half, -sin, 0.0)
    sin_prev = jnp.where(first_half, 0.0, sin)
    return cos, sin_next, sin_prev


def _identity_tables(t):
    return jnp.ones((t, 128), F32), jnp.zeros((t, 128), F32), jnp.zeros((t, 128), F32)


def kernel(x, c, ctx, c_ctx, w_mod, b_mod, ln_g, ln_b, w_gu1, w_dn1, w_in, w_out, conv_w, lam_q1, lam_k1, lam_q2,
           lam_k2, subln_w, sink, qn_w, kn_w, w_gu2, w_dn2):
    bsz, seq, d = x.shape
    ctx_len = ctx.shape[1]
    depth = w_mod.shape[0]
    alpha = (2.0 * depth) ** 0.25
    assert seq % TOK_TILE == 0 and ctx_len % Q_TILE == 0 and bsz < MOD_ROWS

    tabs_a = _rope_tables(seq // GRID_W, A_QK_DIM)
    tabs_h = _rope_tables(seq // GRID_W, HEAD_DIM)
    tabs_id = _identity_tables(ctx_len)
    gsum = (jnp.arange(256)[:, None] // HEAD_DIM == jnp.arange(256)[None, :] // HEAD_DIM).astype(BF16)

    cvec = jnp.zeros((MOD_ROWS, d), F32).at[:bsz].set(c).at[bsz].set(c_ctx)
    mods_all = _mod_vectors(cvec, w_mod, b_mod).reshape(depth, MOD_ROWS, N_MOD, d)
    lat_row = lambda b: b
    ctx_row = lambda b: bsz

    xc = ctx
    for l in range(depth):
        need_ctx = l < depth - 1
        lambda_init = 0.8 - 0.6 * math.exp(-0.3 * l)
        mods = mods_all[l]
        wgu1, wdn1 = w_gu1[l].astype(BF16), w_dn1[l].astype(BF16)
        wgu2, wdn2 = w_gu2[l].astype(BF16), w_dn2[l].astype(BF16)
        win, wout = w_in[l].astype(BF16), w_out[l].astype(BF16)
        qnw = jnp.tile(qn_w[l], 256 // HEAD_DIM).reshape(1, 256)
        knw = jnp.tile(kn_w[l], 128 // HEAD_DIM).reshape(1, 128)
        lamp = jnp.stack([lam_q1[l], lam_k1[l], lam_q2[l], lam_k2[l]])
        subw = subln_w[l].reshape(HEAD_DIM, 1)

        x = _ffn_step(x, mods, lat_row, 0, wgu1, wdn1, ln_g[l, 0], ln_b[l, 0], alpha)
        xc = _ffn_step(xc, mods, ctx_row, 0, wgu1, wdn1, ln_g[l, 0], ln_b[l, 0], alpha)

        qa, ka, vat, g3, qc, kc, vct, qd, kd, vdt = _in_projection(x, mods, lat_row, win, tabs_a, tabs_h, qnw, knw, gsum)
        (qa_c, ka_c, vat_c, g3_c, qc_c, kc_c, vct_c, qd_c, kd_c, vdt_c) = _in_projection(
            xc, mods, ctx_row, win, tabs_id, tabs_id, qnw, knw, gsum)
        ya = _dense_attention(qa, ka_c, vat_c, ka, vat, n_maps=8, dq=A_QK_DIM, lamp=lamp, subw=subw,
                              diff_lambda_init=lambda_init)
        yw = _window_attention(qc, kc_c, vct_c, kc, vct, sink[l])
        yg = _dense_attention(qd, kd_c, vdt_c, kd, vdt, n_maps=4, dq=HEAD_DIM)
        x = _out_projection(x, mods, lat_row, ya, g3, yw, yg, conv_w[l], wout, ln_g[l, 1], ln_b[l, 1], alpha)

        x = _ffn_step(x, mods, lat_row, 6, wgu2, wdn2, ln_g[l, 2], ln_b[l, 2], alpha)

        if need_ctx:
            ya_c = _dense_attention(qa_c, ka_c, vat_c, None, None, n_maps=8, dq=A_QK_DIM, lamp=lamp, subw=subw,
                                    diff_lambda_init=lambda_init)
            yw_c = _dense_attention(qc_c, kc_c, vct_c, None, None, n_maps=4, dq=HEAD_DIM, sink=sink[l])
            yg_c = _dense_attention(qd_c, kd_c, vdt_c, None, None, n_maps=4, dq=HEAD_DIM)
            xc = _out_projection(xc, mods, ctx_row, ya_c, g3_c, yw_c, yg_c, conv_w[l], wout, ln_g[l, 1], ln_b[l, 1],
                                 alpha)
            xc = _ffn_step(xc, mods, ctx_row, 6, wgu2, wdn2, ln_g[l, 2], ln_b[l, 2], alpha)
    return x
```

```python
import functools
import math

import jax
import jax.numpy as jnp
from jax import lax
from jax.experimental import pallas as pl
from jax.experimental.pallas import tpu as pltpu

F32 = jnp.float32
BF16 = jnp.bfloat16

GRID_W = 64
HEAD_DIM = 64
WINDOW = 128
ROPE_BASE = 10000.0
EPS = 1e-6
NEG = -1e30
A_QK_DIM = 32
N_MOD = 9
MOD_ROWS = 8
VMEM_LIMIT = 56 * 1024 * 1024

OFF_AQ, OFF_AK, OFF_AV = 0, 256, 512
OFF_B = 768
OFF_CQ, OFF_CK, OFF_CV = 1536, 1792, 1920
OFF_DQ, OFF_DK, OFF_DV = 2048, 2304, 2432
IN_W = 2560

TOK_TILE = 512
Q_TILE = 256
FF_CHUNK = 256
SCORE_ROWS = 128
PV_ROWS = 256
LOG2E = math.log2(math.e)


def _params(*sem):
    return pltpu.CompilerParams(dimension_semantics=sem, vmem_limit_bytes=VMEM_LIMIT)


def _layer_norm(r, g, b):
    mu = jnp.mean(r, axis=-1, keepdims=True)
    d = r - mu
    var = jnp.mean(d * d, axis=-1, keepdims=True)
    return d * lax.rsqrt(var + EPS) * g + b


def _mod_kernel(c_ref, w_ref, b_ref, o_ref):
    c = c_ref[...]
    a = (c * jax.nn.sigmoid(c)).astype(BF16)
    o_ref[0] = jnp.dot(a, w_ref[0].astype(BF16), preferred_element_type=F32) + b_ref[0]


def _mod_vectors(cvec, w_mod, b_mod):
    depth, d, n = w_mod.shape
    tn = n // 8
    return pl.pallas_call(
        _mod_kernel,
        grid=(depth, n // tn),
        in_specs=[
            pl.BlockSpec((MOD_ROWS, d), lambda l, j: (0, 0)),
            pl.BlockSpec((1, d, tn), lambda l, j: (l, 0, j)),
            pl.BlockSpec((1, 1, tn), lambda l, j: (l, 0, j)),
        ],
        out_specs=pl.BlockSpec((1, MOD_ROWS, tn), lambda l, j: (l, 0, j)),
        out_shape=jax.ShapeDtypeStruct((depth, MOD_ROWS, n), F32),
        compiler_params=_params("parallel", "parallel"),
        name="mod_vectors",
    )(cvec, w_mod, b_mod.reshape(depth, 1, n))


def _ffn_kernel(x_ref, mod_ref, wgu_ref, wdn_ref, g_ref, b_ref, o_ref, acc_ref, *, mod_base, alpha):
    x = x_ref[0]
    shift = mod_ref[0, mod_base:mod_base + 1, :]
    scale = mod_ref[0, mod_base + 1:mod_base + 2, :]
    gate = mod_ref[0, mod_base + 2:mod_base + 3, :]
    h = (x * (1.0 + scale) + shift).astype(BF16)
    d_ff = wdn_ref.shape[0]
    for c in range(d_ff // FF_CHUNK):
        lo = c * FF_CHUNK
        g = jnp.dot(h, wgu_ref[:, lo:lo + FF_CHUNK], preferred_element_type=F32)
        u = jnp.dot(h, wgu_ref[:, d_ff + lo:d_ff + lo + FF_CHUNK], preferred_element_type=F32)
        a = (g * jax.nn.sigmoid(g) * u).astype(BF16)
        y = jnp.dot(a, wdn_ref[lo:lo + FF_CHUNK, :], preferred_element_type=F32)
        if c == 0:
            acc_ref[...] = y
        else:
            acc_ref[...] += y
    r = alpha * x + (0.5 * gate) * acc_ref[...]
    o_ref[0] = _layer_norm(r, g_ref[...], b_ref[...])


def _ffn_step(x, mods, mod_row, mod_base, w_gu, w_dn, ln_g, ln_b, alpha):
    bsz, t, d = x.shape
    d_ff = w_dn.shape[0]
    tm = min(TOK_TILE, t)
    const = lambda b, i: (0, 0)
    return pl.pallas_call(
        functools.partial(_ffn_kernel, mod_base=mod_base, alpha=alpha),
        grid=(bsz, t // tm),
        in_specs=[
            pl.BlockSpec((1, tm, d), lambda b, i: (b, i, 0)),
            pl.BlockSpec((1, N_MOD, d), lambda b, i: (mod_row(b), 0, 0)),
            pl.BlockSpec((d, 2 * d_ff), const, pipeline_mode=pl.Buffered(1)),
            pl.BlockSpec((d_ff, d), const, pipeline_mode=pl.Buffered(1)),
            pl.BlockSpec((1, d), const),
            pl.BlockSpec((1, d), const),
        ],
        out_specs=pl.BlockSpec((1, tm, d), lambda b, i: (b, i, 0)),
        out_shape=jax.ShapeDtypeStruct(x.shape, F32),
        scratch_shapes=[pltpu.VMEM((tm, d), F32)],
        compiler_params=_params("parallel", "parallel"),
        name="ffn_step",
    )(x, mods, w_gu, w_dn, ln_g.reshape(1, d), ln_b.reshape(1, d))


def _rope(x, cos, sin_next, sin_prev, quarter):
    width = x.shape[1]
    reps = width // cos.shape[1]
    if reps > 1:
        cos = jnp.concatenate([cos] * reps, axis=1)
        sin_next = jnp.concatenate([sin_next] * reps, axis=1)
        sin_prev = jnp.concatenate([sin_prev] * reps, axis=1)
    x_next = pltpu.roll(x, width - quarter, axis=1)
    x_prev = pltpu.roll(x, quarter, axis=1)
    return x * cos + x_next * sin_next + x_prev * sin_prev


def _head_rms(x, gsum_ref, w):
    x2 = x * x
    hi = x2.astype(BF16)
    lo = (x2 - hi.astype(F32)).astype(BF16)
    gs = gsum_ref[:x.shape[1], :x.shape[1]]
    ss = jnp.dot(hi, gs, preferred_element_type=F32) + jnp.dot(lo, gs, preferred_element_type=F32)
    return x * lax.rsqrt(ss * (1.0 / HEAD_DIM) + EPS) * w


def _inproj_kernel(x_ref, mod_ref, w_ref, ca_ref, san_ref, sap_ref, ch_ref, shn_ref, shp_ref,
                   qnw_ref, knw_ref, gsum_ref,
                   qa_ref, ka_ref, vat_ref, g3_ref, qc_ref, kc_ref, vct_ref, qd_ref, kd_ref, vdt_ref):
    x = x_ref[0]
    shift = mod_ref[0, 3:4, :]
    scale = mod_ref[0, 4:5, :]
    h = (x * (1.0 + scale) + shift).astype(BF16)

    def seg(lo, hi):
        return jnp.dot(h, w_ref[:, lo:hi], preferred_element_type=F32)

    rope_a = functools.partial(_rope, cos=ca_ref[...], sin_next=san_ref[...], sin_prev=sap_ref[...],
                               quarter=A_QK_DIM // 4)
    rope_h = functools.partial(_rope, cos=ch_ref[...], sin_next=shn_ref[...], sin_prev=shp_ref[...],
                               quarter=HEAD_DIM // 4)
    qa_ref[0] = (rope_a(seg(OFF_AQ, OFF_AK)) * (LOG2E * A_QK_DIM ** -0.5)).astype(BF16)
    ka_ref[0, 0] = rope_a(seg(OFF_AK, OFF_AV)).astype(BF16)
    vat_ref[0, 0] = seg(OFF_AV, OFF_B).T.astype(BF16)
    g3_ref[0] = seg(OFF_B, OFF_CQ)
    qc_ref[0] = (rope_h(seg(OFF_CQ, OFF_CK)) * (LOG2E * HEAD_DIM ** -0.5)).astype(BF16)
    kc_ref[0, 0] = rope_h(seg(OFF_CK, OFF_CV)).astype(BF16)
    vct_ref[0, 0] = seg(OFF_CV, OFF_DQ).T.astype(BF16)
    qd = _head_rms(seg(OFF_DQ, OFF_DK), gsum_ref, qnw_ref[...])
    qd_ref[0] = (rope_h(qd) * (LOG2E * HEAD_DIM ** -0.5)).astype(BF16)
    kd = _head_rms(seg(OFF_DK, OFF_DV), gsum_ref, knw_ref[...])
    kd_ref[0, 0] = rope_h(kd).astype(BF16)
    vdt_ref[0, 0] = seg(OFF_DV, IN_W).T.astype(BF16)


def _in_projection(x, mods, mod_row, w_in, tabs_a, tabs_h, qnw, knw, gsum):
    bsz, t, d = x.shape
    tm = min(TOK_TILE, t)
    nc = t // tm
    const = lambda b, i: (0, 0)
    tab = pl.BlockSpec((tm, 128), lambda b, i: (i, 0))
    q_spec = pl.BlockSpec((1, tm, 256), lambda b, i: (b, i, 0))

    def k_spec(w):
        return pl.BlockSpec((1, 1, tm, w), lambda b, i: (b, i, 0, 0))

    def vt_spec(w):
        return pl.BlockSpec((1, 1, w, tm), lambda b, i: (b, i, 0, 0))

    def q_shape():
        return jax.ShapeDtypeStruct((bsz, t, 256), BF16)

    def k_shape(w):
        return jax.ShapeDtypeStruct((bsz, nc, tm, w), BF16)

    def vt_shape(w):
        return jax.ShapeDtypeStruct((bsz, nc, w, tm), BF16)

    return pl.pallas_call(
        _inproj_kernel,
        grid=(bsz, nc),
        in_specs=[
            pl.BlockSpec((1, tm, d), lambda b, i: (b, i, 0)),
            pl.BlockSpec((1, N_MOD, d), lambda b, i: (mod_row(b), 0, 0)),
            pl.BlockSpec((d, IN_W), const, pipeline_mode=pl.Buffered(1)),
            tab, tab, tab, tab, tab, tab,
            pl.BlockSpec((1, 256), const),
            pl.BlockSpec((1, 128), const),
            pl.BlockSpec((256, 256), const),
        ],
        out_specs=[q_spec, k_spec(256), vt_spec(256),
                   pl.BlockSpec((1, tm, 768), lambda b, i: (b, i, 0)),
                   q_spec, k_spec(128), vt_spec(128),
                   q_spec, k_spec(128), vt_spec(128)],
        out_shape=[q_shape(), k_shape(256), vt_shape(256),
                   jax.ShapeDtypeStruct((bsz, t, 768), F32),
                   q_shape(), k_shape(128), vt_shape(128),
                   q_shape(), k_shape(128), vt_shape(128)],
        compiler_params=_params("parallel", "parallel"),
        name="in_projection",
    )(x, mods, w_in, *tabs_a, *tabs_h, qnw, knw, gsum)


def _fill_qpad(q_ref, qpad_ref, n_maps, dq, kw):
    tq = q_ref.shape[1]
    qt = q_ref[0].astype(F32).T
    for j in range(n_maps):
        row = j * dq if kw == qt.shape[0] else (j // 2) * dq
        parts = []
        if row:
            parts.append(jnp.zeros((row, tq), F32))
        parts.append(qt[j * dq:(j + 1) * dq, :])
        if kw - row - dq:
            parts.append(jnp.zeros((kw - row - dq, tq), F32))
        qpad_ref[j] = jnp.concatenate(parts, axis=0).astype(BF16)


def _softmax_step(k_c, vt_c, qpad_ref, m_ref, l_ref, acc_ref, n_maps, mask=None):
    for j in range(n_maps):
        s = jnp.dot(k_c, qpad_ref[j], preferred_element_type=F32)
        if mask is not None:
            s = jnp.where(mask, s, NEG)
        m_old = m_ref[j]
        m_new = jnp.maximum(m_old, jnp.max(s, axis=0, keepdims=True))
        p = jnp.exp2(s - m_new)
        alpha = jnp.exp2(m_old - m_new)
        l_ref[j] = alpha * l_ref[j] + jnp.sum(p, axis=0, keepdims=True)
        vb = (j // 2) * HEAD_DIM
        pv = jnp.dot(vt_c[vb:vb + HEAD_DIM, :], p.astype(BF16), preferred_element_type=F32)
        acc_ref[j] = alpha * acc_ref[j] + pv
        m_ref[j] = m_new


def _score_unit(k_c, qpad_j_ref, s_ref, bm_ref):
    nk = k_c.shape[0]
    bm = None
    for r in range(nk // SCORE_ROWS):
        rows = slice(r * SCORE_ROWS, (r + 1) * SCORE_ROWS)
        s = jnp.dot(k_c[rows, :], qpad_j_ref[...], preferred_element_type=F32)
        s_ref[rows, :] = s
        part = jnp.max(s, axis=0, keepdims=True)
        bm = part if bm is None else jnp.maximum(bm, part)
    bm_ref[...] = bm


def _softmax_unit(s_ref, bm_ref, vt_c, j, m_ref, l_ref, acc_ref):
    nk = s_ref.shape[0]
    m_old = m_ref[j]
    m_new = jnp.maximum(m_old, bm_ref[...])
    alpha = jnp.exp2(m_old - m_new)
    vb = (j // 2) * HEAD_DIM
    psum = None
    pv = None
    for r in range(nk // PV_ROWS):
        rows = slice(r * PV_ROWS, (r + 1) * PV_ROWS)
        p = jnp.exp2(s_ref[rows, :] - m_new)
        part = jnp.sum(p, axis=0, keepdims=True)
        psum = part if psum is None else psum + part
        d = jnp.dot(vt_c[vb:vb + HEAD_DIM, rows], p.astype(BF16), preferred_element_type=F32)
        pv = d if pv is None else pv + d
    l_ref[j] = alpha * l_ref[j] + psum
    acc_ref[j] = alpha * acc_ref[j] + pv
    m_ref[j] = m_new


def _init_state(m_ref, l_ref, acc_ref, n_maps, sink_ref):
    for j in range(n_maps):
        if sink_ref is None:
            m_ref[j] = jnp.full(m_ref.shape[1:], NEG, F32)
            l_ref[j] = jnp.zeros(l_ref.shape[1:], F32)
        else:
            m_ref[j] = jnp.full(m_ref.shape[1:], sink_ref[j] * LOG2E, F32)
            l_ref[j] = jnp.ones(l_ref.shape[1:], F32)
        acc_ref[j] = jnp.zeros(acc_ref.shape[1:], F32)


def _finish_plain(o_ref, l_ref, acc_ref, n_maps):
    ot = jnp.concatenate([acc_ref[j] / l_ref[j] for j in range(n_maps)], axis=0)
    o_ref[0] = ot.T.astype(BF16)


def _finish_diff(o_ref, l_ref, acc_ref, lamp_ref, subw_ref, n_maps, lambda_init):
    lp = lamp_ref[...]
    lam = (jnp.exp(jnp.sum(lp[0:1] * lp[1:2], axis=1, keepdims=True))
           - jnp.exp(jnp.sum(lp[2:3] * lp[3:4], axis=1, keepdims=True)) + lambda_init)
    outs = []
    for hd in range(n_maps // 2):
        o = acc_ref[2 * hd] / l_ref[2 * hd] - lam * (acc_ref[2 * hd + 1] / l_ref[2 * hd + 1])
        ms = jnp.mean(o * o, axis=0, keepdims=True)
        outs.append(o * lax.rsqrt(ms + EPS) * subw_ref[...] * (1.0 - lambda_init))
    o_ref[0] = jnp.concatenate(outs, axis=0).T.astype(BF16)


def _dense_attn_kernel(*refs, n_maps, dq, kw, has_latent, has_sink, diff_lambda_init):
    refs = list(refs)
    q_ref, kc_ref, vct_ref = refs[:3]
    pos = 3
    k_ref = vt_ref = sink_ref = lamp_ref = subw_ref = None
    if has_latent:
        k_ref, vt_ref = refs[pos:pos + 2]
        pos += 2
    if has_sink:
        sink_ref = refs[pos]
        pos += 1
    if diff_lambda_init is not None:
        lamp_ref, subw_ref = refs[pos:pos + 2]
        pos += 2
    o_ref, qpad_ref, m_ref, l_ref, acc_ref = refs[pos:pos + 5]

    _fill_qpad(q_ref, qpad_ref, n_maps, dq, kw)
    _init_state(m_ref, l_ref, acc_ref, n_maps, sink_ref)
    for c in range(kc_ref.shape[1]):
        _softmax_step(kc_ref[0, c], vct_ref.at[0, c], qpad_ref, m_ref, l_ref, acc_ref, n_maps)
    if has_latent:
        s_bufs, bm_bufs = refs[pos + 5:pos + 7], refs[pos + 7:pos + 9]
        n_chunks = k_ref.shape[1]
        _score_unit(k_ref.at[0, 0], qpad_ref.at[0], s_bufs[0], bm_bufs[0])

        def body(c, carry):
            c_next = jnp.minimum(c + 1, n_chunks - 1)
            for j in range(n_maps):
                nxt = (j + 1) % 2
                if j + 1 < n_maps:
                    _score_unit(k_ref.at[0, c], qpad_ref.at[j + 1], s_bufs[nxt], bm_bufs[nxt])
                else:
                    _score_unit(k_ref.at[0, c_next], qpad_ref.at[0], s_bufs[nxt], bm_bufs[nxt])
                _softmax_unit(s_bufs[j % 2], bm_bufs[j % 2], vt_ref.at[0, c], j, m_ref, l_ref, acc_ref)
            return carry
        lax.fori_loop(0, n_chunks, body, 0)
    if diff_lambda_init is None:
        _finish_plain(o_ref, l_ref, acc_ref, n_maps)
    else:
        _finish_diff(o_ref, l_ref, acc_ref, lamp_ref, subw_ref, n_maps, diff_lambda_init)


def _dense_attention(q, k_ctx, vt_ctx, k_lat, vt_lat, *, n_maps, dq, sink=None, lamp=None, subw=None,
                     diff_lambda_init=None):
    bsz, t, qw = q.shape
    kw = k_ctx.shape[-1]
    vw = vt_ctx.shape[-2]
    tq = min(Q_TILE, t)
    has_latent = k_lat is not None

    def whole(arr):
        return pl.BlockSpec((1,) + arr.shape[1:], lambda b, i: (b, 0, 0, 0), pipeline_mode=pl.Buffered(1))

    in_specs = [pl.BlockSpec((1, tq, qw), lambda b, i: (b, i, 0)), whole(k_ctx), whole(vt_ctx)]
    args = [q, k_ctx, vt_ctx]
    if has_latent:
        in_specs += [whole(k_lat), whole(vt_lat)]
        args += [k_lat, vt_lat]
    if sink is not None:
        in_specs.append(pl.BlockSpec(memory_space=pltpu.SMEM))
        args.append(sink)
    if diff_lambda_init is not None:
        in_specs += [pl.BlockSpec(lamp.shape, lambda b, i: (0, 0)), pl.BlockSpec(subw.shape, lambda b, i: (0, 0))]
        args += [lamp, subw]
    scratch = [pltpu.VMEM((n_maps, kw, tq), BF16),
               pltpu.VMEM((n_maps, 1, tq), F32),
               pltpu.VMEM((n_maps, 1, tq), F32),
               pltpu.VMEM((n_maps, HEAD_DIM, tq), F32)]
    if has_latent:
        tk = k_lat.shape[2]
        assert n_maps % 2 == 0 and tk % SCORE_ROWS == 0 and tk % PV_ROWS == 0
        scratch += [pltpu.VMEM((tk, tq), F32), pltpu.VMEM((tk, tq), F32),
                    pltpu.VMEM((1, tq), F32), pltpu.VMEM((1, tq), F32)]
    return pl.pallas_call(
        functools.partial(_dense_attn_kernel, n_maps=n_maps, dq=dq, kw=kw, has_latent=has_latent,
                          has_sink=sink is not None, diff_lambda_init=diff_lambda_init),
        grid=(bsz, t // tq),
        in_specs=in_specs,
        out_specs=pl.BlockSpec((1, tq, 256), lambda b, i: (b, i, 0)),
        out_shape=jax.ShapeDtypeStruct((bsz, t, 256), BF16),
        scratch_shapes=scratch,
        compiler_params=_params("parallel", "parallel"),
        name="dense_attention",
    )(*args)


def _window_attn_kernel(q_ref, kp_ref, kc_ref, kn_ref, vtp_ref, vtc_ref, vtn_ref, kctx_ref, vtctx_ref, sink_ref,
                        o_ref, qpad_ref, m_ref, l_ref, acc_ref, *, n_maps, seq):
    tq = q_ref.shape[1]
    q0 = pl.program_id(1) * tq
    _fill_qpad(q_ref, qpad_ref, n_maps, HEAD_DIM, kctx_ref.shape[-1])
    _init_state(m_ref, l_ref, acc_ref, n_maps, sink_ref)
    for c in range(kctx_ref.shape[1]):
        _softmax_step(kctx_ref[0, c], vtctx_ref.at[0, c], qpad_ref, m_ref, l_ref, acc_ref, n_maps)
    k_win = jnp.concatenate([kp_ref[0], kc_ref[0], kn_ref[0]], axis=0)
    vt_win = jnp.concatenate([vtp_ref[0, 0], vtc_ref[0, 0], vtn_ref[0, 0]], axis=1)
    nk = k_win.shape[0]
    kpos = q0 - WINDOW + lax.broadcasted_iota(jnp.int32, (nk, tq), 0)
    qpos = q0 + lax.broadcasted_iota(jnp.int32, (nk, tq), 1)
    valid = (jnp.abs(kpos - qpos) <= WINDOW) & (kpos >= 0) & (kpos < seq)
    _softmax_step(k_win, vt_win, qpad_ref, m_ref, l_ref, acc_ref, n_maps, mask=valid)
    _finish_plain(o_ref, l_ref, acc_ref, n_maps)


def _window_attention(q, k_ctx, vt_ctx, k_lat, vt_lat, sink):
    bsz, t, qw = q.shape
    nc, tk, kw = k_lat.shape[1:]
    tq = Q_TILE
    n_maps = 4
    k_flat = k_lat.reshape(bsz, t, kw)
    wb = tq // WINDOW
    nwb = t // WINDOW
    per_chunk = tk // WINDOW

    def whole(arr):
        return pl.BlockSpec((1,) + arr.shape[1:], lambda b, i: (b, 0, 0, 0), pipeline_mode=pl.Buffered(1))

    def prev_blk(i):
        return jnp.maximum(i * wb - 1, 0)

    def next_blk(i):
        return jnp.minimum(i * wb + wb, nwb - 1)

    in_specs = [
        pl.BlockSpec((1, tq, qw), lambda b, i: (b, i, 0)),
        pl.BlockSpec((1, WINDOW, kw), lambda b, i: (b, prev_blk(i), 0)),
        pl.BlockSpec((1, tq, kw), lambda b, i: (b, i, 0)),
        pl.BlockSpec((1, WINDOW, kw), lambda b, i: (b, next_blk(i), 0)),
        pl.BlockSpec((1, 1, kw, WINDOW), lambda b, i: (b, prev_blk(i) // per_chunk, 0, prev_blk(i) % per_chunk)),
        pl.BlockSpec((1, 1, kw, tq), lambda b, i: (b, (i * tq) // tk, 0, ((i * tq) % tk) // tq)),
        pl.BlockSpec((1, 1, kw, WINDOW), lambda b, i: (b, next_blk(i) // per_chunk, 0, next_blk(i) % per_chunk)),
        whole(k_ctx), whole(vt_ctx),
        pl.BlockSpec(memory_space=pltpu.SMEM),
    ]
    return pl.pallas_call(
        functools.partial(_window_attn_kernel, n_maps=n_maps, seq=t),
        grid=(bsz, t // tq),
        in_specs=in_specs,
        out_specs=pl.BlockSpec((1, tq, 256), lambda b, i: (b, i, 0)),
        out_shape=jax.ShapeDtypeStruct((bsz, t, 256), BF16),
        scratch_shapes=[pltpu.VMEM((n_maps, kw, tq), BF16),
                        pltpu.VMEM((n_maps, 1, tq), F32),
                        pltpu.VMEM((n_maps, 1, tq), F32),
                        pltpu.VMEM((n_maps, HEAD_DIM, tq), F32)],
        compiler_params=_params("parallel", "parallel"),
        name="window_attention",
    )(q, k_flat, k_flat, k_flat, vt_lat, vt_lat, vt_lat, k_ctx, vt_ctx, sink)


def _outproj_kernel(x_ref, mod_ref, ya_ref, g3_ref, g3p_ref, g3n_ref, yw_ref, yg_ref, cw_ref, w_ref, g_ref, b_ref,
                    o_ref, *, alpha):
    i = pl.program_id(1)
    last = pl.num_programs(1) - 1
    tm = x_ref.shape[1]
    gb = g3_ref[0, :, 0:256]
    hid = g3_ref[0, :, 256:512] * g3_ref[0, :, 512:768]
    halo = g3p_ref.shape[1]
    h_prev = g3p_ref[0, halo - 1:halo, 256:512] * g3p_ref[0, halo - 1:halo, 512:768]
    h_next = g3n_ref[0, 0:1, 256:512] * g3n_ref[0, 0:1, 512:768]
    h_prev = jnp.where(i == 0, 0.0, h_prev)
    h_next = jnp.where(i == last, 0.0, h_next)
    row = lax.broadcasted_iota(jnp.int32, hid.shape, 0)
    below = jnp.where(row == 0, h_prev, pltpu.roll(hid, 1, axis=0))
    above = jnp.where(row == tm - 1, h_next, pltpu.roll(hid, tm - 1, axis=0))
    yb = gb * (cw_ref[0:1, :] * below + cw_ref[1:2, :] * hid + cw_ref[2:3, :] * above)
    y = jnp.dot(ya_ref[0], w_ref[0:256, :], preferred_element_type=F32)
    y += jnp.dot(yb.astype(BF16), w_ref[256:512, :], preferred_element_type=F32)
    y += jnp.dot(yw_ref[0], w_ref[512:768, :], preferred_element_type=F32)
    y += jnp.dot(yg_ref[0], w_ref[768:1024, :], preferred_element_type=F32)
    gate = mod_ref[0, 5:6, :]
    o_ref[0] = _layer_norm(alpha * x_ref[0] + gate * y, g_ref[...], b_ref[...])


def _out_projection(x, mods, mod_row, ya, g3, yw, yg, conv_w, w_out, ln_g, ln_b, alpha):
    bsz, t, d = x.shape
    tm = min(TOK_TILE, t)
    halo = 8
    hb = tm // halo
    n_halo = t // halo
    const = lambda b, i: (0, 0)
    y_spec = pl.BlockSpec((1, tm, 256), lambda b, i: (b, i, 0))
    return pl.pallas_call(
        functools.partial(_outproj_kernel, alpha=alpha),
        grid=(bsz, t // tm),
        in_specs=[
            pl.BlockSpec((1, tm, d), lambda b, i: (b, i, 0)),
            pl.BlockSpec((1, N_MOD, d), lambda b, i: (mod_row(b), 0, 0)),
            y_spec,
            pl.BlockSpec((1, tm, 768), lambda b, i: (b, i, 0)),
            pl.BlockSpec((1, halo, 768), lambda b, i: (b, jnp.maximum(i * hb - 1, 0), 0)),
            pl.BlockSpec((1, halo, 768), lambda b, i: (b, jnp.minimum(i * hb + hb, n_halo - 1), 0)),
            y_spec, y_spec,
            pl.BlockSpec(conv_w.shape, const),
            pl.BlockSpec(w_out.shape, const, pipeline_mode=pl.Buffered(1)),
            pl.BlockSpec((1, d), const),
            pl.BlockSpec((1, d), const),
        ],
        out_specs=pl.BlockSpec((1, tm, d), lambda b, i: (b, i, 0)),
        out_shape=jax.ShapeDtypeStruct(x.shape, F32),
        compiler_params=_params("parallel", "parallel"),
        name="out_projection",
    )(x, mods, ya, g3, g3, g3, yw, yg, conv_w, w_out, ln_g.reshape(1, d), ln_b.reshape(1, d))


def _rope_tables(rows, dim):
    n_freq = dim // 4
    inv_freq = ROPE_BASE ** (-jnp.arange(n_freq, dtype=F32) / n_freq)
    ang_r = jnp.arange(rows, dtype=F32)[:, None] * inv_freq
    ang_c = jnp.arange(GRID_W, dtype=F32)[:, None] * inv_freq
    ang_r = jnp.broadcast_to(ang_r[:, None, :], (rows, GRID_W, n_freq))
    ang_c = jnp.broadcast_to(ang_c[None, :, :], (rows, GRID_W, n_freq))
    ang = jnp.concatenate([ang_r, ang_r, ang_c, ang_c], axis=-1).reshape(rows * GRID_W, dim)
    cos, sin = jnp.cos(ang), jnp.sin(ang)
    reps = 128 // dim
    cos = jnp.tile(cos, (1, reps))
    sin = jnp.tile(sin, (1, reps))
    first_half = (jnp.arange(128) % (dim // 2)) < (dim // 4)
    sin_next = jnp.where(first_half, -sin, 0.0)
    sin_prev = jnp.where(first_half, 0.0, sin)
    return cos, sin_next, sin_prev


def _identity_tables(t):
    return jnp.ones((t, 128), F32), jnp.zeros((t, 128), F32), jnp.zeros((t, 128), F32)


def kernel(x, c, ctx, c_ctx, w_mod, b_mod, ln_g, ln_b, w_gu1, w_dn1, w_in, w_out, conv_w, lam_q1, lam_k1, lam_q2,
           lam_k2, subln_w, sink, qn_w, kn_w, w_gu2, w_dn2):
    bsz, seq, d = x.shape
    ctx_len = ctx.shape[1]
    depth = w_mod.shape[0]
    alpha = (2.0 * depth) ** 0.25
    assert seq % TOK_TILE == 0 and ctx_len % Q_TILE == 0 and bsz < MOD_ROWS

    tabs_a = _rope_tables(seq // GRID_W, A_QK_DIM)
    tabs_h = _rope_tables(seq // GRID_W, HEAD_DIM)
    tabs_id = _identity_tables(ctx_len)
    gsum = (jnp.arange(256)[:, None] // HEAD_DIM == jnp.arange(256)[None, :] // HEAD_DIM).astype(BF16)

    cvec = jnp.zeros((MOD_ROWS, d), F32).at[:bsz].set(c).at[bsz].set(c_ctx)
    mods_all = _mod_vectors(cvec, w_mod, b_mod).reshape(depth, MOD_ROWS, N_MOD, d)
    lat_row = lambda b: b
    ctx_row = lambda b: bsz

    xc = ctx
    for l in range(depth):
        need_ctx = l < depth - 1
        lambda_init = 0.8 - 0.6 * math.exp(-0.3 * l)
        mods = mods_all[l]
        wgu1, wdn1 = w_gu1[l].astype(BF16), w_dn1[l].astype(BF16)
        wgu2, wdn2 = w_gu2[l].astype(BF16), w_dn2[l].astype(BF16)
        win, wout = w_in[l].astype(BF16), w_out[l].astype(BF16)
        qnw = jnp.tile(qn_w[l], 256 // HEAD_DIM).reshape(1, 256)
        knw = jnp.tile(kn_w[l], 128 // HEAD_DIM).reshape(1, 128)
        lamp = jnp.stack([lam_q1[l], lam_k1[l], lam_q2[l], lam_k2[l]])
        subw = subln_w[l].reshape(HEAD_DIM, 1)

        x = _ffn_step(x, mods, lat_row, 0, wgu1, wdn1, ln_g[l, 0], ln_b[l, 0], alpha)
        xc = _ffn_step(xc, mods, ctx_row, 0, wgu1, wdn1, ln_g[l, 0], ln_b[l, 0], alpha)

        qa, ka, vat, g3, qc, kc, vct, qd, kd, vdt = _in_projection(x, mods, lat_row, win, tabs_a, tabs_h, qnw, knw, gsum)
        (qa_c, ka_c, vat_c, g3_c, qc_c, kc_c, vct_c, qd_c, kd_c, vdt_c) = _in_projection(
            xc, mods, ctx_row, win, tabs_id, tabs_id, qnw, knw, gsum)
        ya = _dense_attention(qa, ka_c, vat_c, ka, vat, n_maps=8, dq=A_QK_DIM, lamp=lamp, subw=subw,
                              diff_lambda_init=lambda_init)
        yw = _window_attention(qc, kc_c, vct_c, kc, vct, sink[l])
        yg = _dense_attention(qd, kd_c, vdt_c, kd, vdt, n_maps=4, dq=HEAD_DIM)
        x = _out_projection(x, mods, lat_row, ya, g3, yw, yg, conv_w[l], wout, ln_g[l, 1], ln_b[l, 1], alpha)

        x = _ffn_step(x, mods, lat_row, 6, wgu2, wdn2, ln_g[l, 2], ln_b[l, 2], alpha)

        if need_ctx:
            ya_c = _dense_attention(qa_c, ka_c, vat_c, None, None, n_maps=8, dq=A_QK_DIM, lamp=lamp, subw=subw,
                                    diff_lambda_init=lambda_init)
            yw_c = _dense_attention(qc_c, kc_c, vct_c, None, None, n_maps=4, dq=HEAD_DIM, sink=sink[l])
            yg_c = _dense_attention(qd_c, kd_c, vdt_c, None, None, n_maps=4, dq=HEAD_DIM)
            xc = _out_projection(xc, mods, ctx_row, ya_c, g3_c, yw_c, yg_c, conv_w[l], wout, ln_g[l, 1], ln_b[l, 1],
                                 alpha)
            xc = _ffn_step(xc, mods, ctx_row, 6, wgu2, wdn2, ln_g[l, 2], ln_b[l, 2], alpha)
    return x
```

```python
import functools
import math

import jax
import jax.numpy as jnp
from jax import lax
from jax.experimental import pallas as pl
from jax.experimental.pallas import tpu as pltpu

F32 = jnp.float32
BF16 = jnp.bfloat16

GRID_W = 64
HEAD_DIM = 64
WINDOW = 128
ROPE_BASE = 10000.0
EPS = 1e-6
NEG = -1e30
A_QK_DIM = 32
N_MOD = 9
MOD_ROWS = 8
VMEM_LIMIT = 56 * 1024 * 1024

OFF_AQ, OFF_AK, OFF_AV = 0, 256, 512
OFF_B = 768
OFF_CQ, OFF_CK, OFF_CV = 1536, 1792, 1920
OFF_DQ, OFF_DK, OFF_DV = 2048, 2304, 2432
IN_W = 2560

TOK_TILE = 512
Q_TILE = 256
FF_CHUNK = 256
SCORE_ROWS = 128
PV_ROWS = 256
UNIT_CHUNKS = 2
STAGES = 4
K_TILE = 128
ONES_ROWS = 16
V_ROWS = HEAD_DIM + ONES_ROWS
LOG2E = math.log2(math.e)


def _params(*sem):
    return pltpu.CompilerParams(dimension_semantics=sem, vmem_limit_bytes=VMEM_LIMIT)


def _layer_norm(r, g, b):
    mu = jnp.mean(r, axis=-1, keepdims=True)
    d = r - mu
    var = jnp.mean(d * d, axis=-1, keepdims=True)
    return d * lax.rsqrt(var + EPS) * g + b


def _mod_kernel(c_ref, w_ref, b_ref, o_ref):
    c = c_ref[...]
    a = (c * jax.nn.sigmoid(c)).astype(BF16)
    o_ref[0] = jnp.dot(a, w_ref[0].astype(BF16), preferred_element_type=F32) + b_ref[0]


def _mod_vectors(cvec, w_mod, b_mod):
    depth, d, n = w_mod.shape
    tn = n // 8
    return pl.pallas_call(
        _mod_kernel,
        grid=(depth, n // tn),
        in_specs=[
            pl.BlockSpec((MOD_ROWS, d), lambda l, j: (0, 0)),
            pl.BlockSpec((1, d, tn), lambda l, j: (l, 0, j)),
            pl.BlockSpec((1, 1, tn), lambda l, j: (l, 0, j)),
        ],
        out_specs=pl.BlockSpec((1, MOD_ROWS, tn), lambda l, j: (l, 0, j)),
        out_shape=jax.ShapeDtypeStruct((depth, MOD_ROWS, n), F32),
        compiler_params=_params("parallel", "parallel"),
        name="mod_vectors",
    )(cvec, w_mod, b_mod.reshape(depth, 1, n))


def _ffn_kernel(x_ref, mod_ref, wgu_ref, wdn_ref, g_ref, b_ref, o_ref, acc_ref, *, mod_base, alpha):
    x = x_ref[0]
    shift = mod_ref[0, mod_base:mod_base + 1, :]
    scale = mod_ref[0, mod_base + 1:mod_base + 2, :]
    gate = mod_ref[0, mod_base + 2:mod_base + 3, :]
    h = (x * (1.0 + scale) + shift).astype(BF16)
    d_ff = wdn_ref.shape[0]
    for c in range(d_ff // FF_CHUNK):
        lo = c * FF_CHUNK
        g = jnp.dot(h, wgu_ref[:, lo:lo + FF_CHUNK], preferred_element_type=F32)
        u = jnp.dot(h, wgu_ref[:, d_ff + lo:d_ff + lo + FF_CHUNK], preferred_element_type=F32)
        a = (g * jax.nn.sigmoid(g) * u).astype(BF16)
        y = jnp.dot(a, wdn_ref[lo:lo + FF_CHUNK, :], preferred_element_type=F32)
        if c == 0:
            acc_ref[...] = y
        else:
            acc_ref[...] += y
    r = alpha * x + (0.5 * gate) * acc_ref[...]
    o_ref[0] = _layer_norm(r, g_ref[...], b_ref[...])


def _ffn_step(x, mods, mod_row, mod_base, w_gu, w_dn, ln_g, ln_b, alpha):
    bsz, t, d = x.shape
    d_ff = w_dn.shape[0]
    tm = min(TOK_TILE, t)
    const = lambda b, i: (0, 0)
    return pl.pallas_call(
        functools.partial(_ffn_kernel, mod_base=mod_base, alpha=alpha),
        grid=(bsz, t // tm),
        in_specs=[
            pl.BlockSpec((1, tm, d), lambda b, i: (b, i, 0)),
            pl.BlockSpec((1, N_MOD, d), lambda b, i: (mod_row(b), 0, 0)),
            pl.BlockSpec((d, 2 * d_ff), const, pipeline_mode=pl.Buffered(1)),
            pl.BlockSpec((d_ff, d), const, pipeline_mode=pl.Buffered(1)),
            pl.BlockSpec((1, d), const),
            pl.BlockSpec((1, d), const),
        ],
        out_specs=pl.BlockSpec((1, tm, d), lambda b, i: (b, i, 0)),
        out_shape=jax.ShapeDtypeStruct(x.shape, F32),
        scratch_shapes=[pltpu.VMEM((tm, d), F32)],
        compiler_params=_params("parallel", "parallel"),
        name="ffn_step",
    )(x, mods, w_gu, w_dn, ln_g.reshape(1, d), ln_b.reshape(1, d))


def _rope(x, cos, sin_next, sin_prev, quarter):
    width = x.shape[1]
    reps = width // cos.shape[1]
    if reps > 1:
        cos = jnp.concatenate([cos] * reps, axis=1)
        sin_next = jnp.concatenate([sin_next] * reps, axis=1)
        sin_prev = jnp.concatenate([sin_prev] * reps, axis=1)
    x_next = pltpu.roll(x, width - quarter, axis=1)
    x_prev = pltpu.roll(x, quarter, axis=1)
    return x * cos + x_next * sin_next + x_prev * sin_prev


def _head_rms(x, gsum_ref, w):
    x2 = x * x
    hi = x2.astype(BF16)
    lo = (x2 - hi.astype(F32)).astype(BF16)
    gs = gsum_ref[:x.shape[1], :x.shape[1]]
    ss = jnp.dot(hi, gs, preferred_element_type=F32) + jnp.dot(lo, gs, preferred_element_type=F32)
    return x * lax.rsqrt(ss * (1.0 / HEAD_DIM) + EPS) * w


def _store_k(k_ref, k):
    for kt in range(k.shape[1] // K_TILE):
        k_ref[0, 0, kt] = k[:, kt * K_TILE:(kt + 1) * K_TILE].astype(BF16)


def _store_vt(vt_ref, v):
    vt = v.T
    tm = v.shape[0]
    for hd in range(v.shape[1] // HEAD_DIM):
        vt_ref[0, 0, hd, 0:HEAD_DIM, :] = vt[hd * HEAD_DIM:(hd + 1) * HEAD_DIM, :].astype(BF16)
        vt_ref[0, 0, hd, HEAD_DIM:V_ROWS, :] = jnp.ones((ONES_ROWS, tm), BF16)


def _inproj_kernel(x_ref, mod_ref, w_ref, ca_ref, san_ref, sap_ref, ch_ref, shn_ref, shp_ref,
                   qnw_ref, knw_ref, gsum_ref,
                   qa_ref, ka_ref, vat_ref, g3_ref, qc_ref, kc_ref, vct_ref, qd_ref, kd_ref, vdt_ref):
    x = x_ref[0]
    shift = mod_ref[0, 3:4, :]
    scale = mod_ref[0, 4:5, :]
    h = (x * (1.0 + scale) + shift).astype(BF16)

    def seg(lo, hi):
        return jnp.dot(h, w_ref[:, lo:hi], preferred_element_type=F32)

    rope_a = functools.partial(_rope, cos=ca_ref[...], sin_next=san_ref[...], sin_prev=sap_ref[...],
                               quarter=A_QK_DIM // 4)
    rope_h = functools.partial(_rope, cos=ch_ref[...], sin_next=shn_ref[...], sin_prev=shp_ref[...],
                               quarter=HEAD_DIM // 4)
    qa_ref[0] = (rope_a(seg(OFF_AQ, OFF_AK)) * (LOG2E * A_QK_DIM ** -0.5)).astype(BF16)
    _store_k(ka_ref, rope_a(seg(OFF_AK, OFF_AV)))
    _store_vt(vat_ref, seg(OFF_AV, OFF_B))
    g3_ref[0] = seg(OFF_B, OFF_CQ)
    qc_ref[0] = (rope_h(seg(OFF_CQ, OFF_CK)) * (LOG2E * HEAD_DIM ** -0.5)).astype(BF16)
    _store_k(kc_ref, rope_h(seg(OFF_CK, OFF_CV)))
    _store_vt(vct_ref, seg(OFF_CV, OFF_DQ))
    qd = _head_rms(seg(OFF_DQ, OFF_DK), gsum_ref, qnw_ref[...])
    qd_ref[0] = (rope_h(qd) * (LOG2E * HEAD_DIM ** -0.5)).astype(BF16)
    kd = _head_rms(seg(OFF_DK, OFF_DV), gsum_ref, knw_ref[...])
    _store_k(kd_ref, rope_h(kd))
    _store_vt(vdt_ref, seg(OFF_DV, IN_W))


def _in_projection(x, mods, mod_row, w_in, tabs_a, tabs_h, qnw, knw, gsum):
    bsz, t, d = x.shape
    tm = min(TOK_TILE, t)
    nc = t // tm
    const = lambda b, i: (0, 0)
    tab = pl.BlockSpec((tm, 128), lambda b, i: (i, 0))
    q_spec = pl.BlockSpec((1, tm, 256), lambda b, i: (b, i, 0))

    def k_spec(w):
        return pl.BlockSpec((1, 1, w // K_TILE, tm, K_TILE), lambda b, i: (b, i, 0, 0, 0))

    def vt_spec(w):
        return pl.BlockSpec((1, 1, w // HEAD_DIM, V_ROWS, tm), lambda b, i: (b, i, 0, 0, 0))

    def q_shape():
        return jax.ShapeDtypeStruct((bsz, t, 256), BF16)

    def k_shape(w):
        return jax.ShapeDtypeStruct((bsz, nc, w // K_TILE, tm, K_TILE), BF16)

    def vt_shape(w):
        return jax.ShapeDtypeStruct((bsz, nc, w // HEAD_DIM, V_ROWS, tm), BF16)

    return pl.pallas_call(
        _inproj_kernel,
        grid=(bsz, nc),
        in_specs=[
            pl.BlockSpec((1, tm, d), lambda b, i: (b, i, 0)),
            pl.BlockSpec((1, N_MOD, d), lambda b, i: (mod_row(b), 0, 0)),
            pl.BlockSpec((d, IN_W), const, pipeline_mode=pl.Buffered(1)),
            tab, tab, tab, tab, tab, tab,
            pl.BlockSpec((1, 256), const),
            pl.BlockSpec((1, 128), const),
            pl.BlockSpec((256, 256), const),
        ],
        out_specs=[q_spec, k_spec(256), vt_spec(256),
                   pl.BlockSpec((1, tm, 768), lambda b, i: (b, i, 0)),
                   q_spec, k_spec(128), vt_spec(128),
                   q_spec, k_spec(128), vt_spec(128)],
        out_shape=[q_shape(), k_shape(256), vt_shape(256),
                   jax.ShapeDtypeStruct((bsz, t, 768), F32),
                   q_shape(), k_shape(128), vt_shape(128),
                   q_shape(), k_shape(128), vt_shape(128)],
        compiler_params=_params("parallel", "parallel"),
        name="in_projection",
    )(x, mods, w_in, *tabs_a, *tabs_h, qnw, knw, gsum)


def _whole_per_batch(arr):
    zeros = (0,) * (arr.ndim - 1)
    return pl.BlockSpec((1,) + arr.shape[1:], lambda b, i: (b,) + zeros, pipeline_mode=pl.Buffered(1))


def _state_scratch(n_maps, tq):
    return [pltpu.VMEM((n_maps, K_TILE, tq), BF16),
            pltpu.VMEM((n_maps, 1, tq), F32),
            pltpu.VMEM((n_maps, V_ROWS, tq), F32)]


def _key_offset(j, dq, grouped):
    return (j // 2) * dq if grouped else j * dq


def _fill_qpad(q_ref, qpad_ref, n_maps, dq, grouped):
    tq = q_ref.shape[1]
    qt = q_ref[0].astype(F32).T
    for j in range(n_maps):
        row = _key_offset(j, dq, grouped) % K_TILE
        parts = []
        if row:
            parts.append(jnp.zeros((row, tq), F32))
        parts.append(qt[j * dq:(j + 1) * dq, :])
        if K_TILE - row - dq:
            parts.append(jnp.zeros((K_TILE - row - dq, tq), F32))
        qpad_ref[j] = jnp.concatenate(parts, axis=0).astype(BF16)


def _softmax_step(k_c, vt_c, qpad_ref, m_ref, acc_ref, n_maps, dq, grouped, mask=None):
    for j in range(n_maps):
        k_tile = k_c[_key_offset(j, dq, grouped) // K_TILE]
        s = jnp.dot(k_tile, qpad_ref[j], preferred_element_type=F32)
        if mask is not None:
            s = jnp.where(mask, s, NEG)
        m_old = m_ref[j]
        m_new = jnp.maximum(m_old, jnp.max(s, axis=0, keepdims=True))
        p = jnp.exp2(s - m_new)
        alpha = jnp.exp2(m_old - m_new)
        pv = jnp.dot(vt_c[j // 2], p.astype(BF16), preferred_element_type=F32)
        acc_ref[j] = alpha * acc_ref[j] + pv
        m_ref[j] = m_new


def _score_unit(k_ref, u, qpad_ref, s_ref, bm_ref, n_maps, dq, grouped, group):
    j = u % n_maps
    c0 = (u // n_maps) * group
    kt = _key_offset(j, dq, grouped) // K_TILE
    tk = k_ref.shape[3]
    tq = s_ref.shape[1]
    w = qpad_ref[j]
    bm = None
    for g in range(group):
        for r in range(tk // SCORE_ROWS):
            s = jnp.dot(k_ref[0, c0 + g, kt, r * SCORE_ROWS:(r + 1) * SCORE_ROWS, :], w,
                        preferred_element_type=F32)
            s_ref[g * tk + r * SCORE_ROWS:g * tk + (r + 1) * SCORE_ROWS, :] = s
            part = jnp.max(s.reshape(SCORE_ROWS // 8, 8, tq), axis=0)
            bm = part if bm is None else jnp.maximum(bm, part)
    bm_ref[...] = bm


def _softmax_unit(s_ref, bm_ref, vt_ref, u, m_ref, acc_ref, n_maps, group):
    j = u % n_maps
    c0 = (u // n_maps) * group
    tk = vt_ref.shape[-1]
    m_old = m_ref[j]
    m_new = jnp.maximum(m_old, jnp.max(bm_ref[...], axis=0, keepdims=True))
    alpha = jnp.exp2(m_old - m_new)
    pv = None
    for g in range(group):
        for r in range(tk // PV_ROWS):
            p = jnp.exp2(s_ref[g * tk + r * PV_ROWS:g * tk + (r + 1) * PV_ROWS, :] - m_new)
            d = jnp.dot(vt_ref[0, c0 + g, j // 2, :, r * PV_ROWS:(r + 1) * PV_ROWS], p.astype(BF16),
                        preferred_element_type=F32)
            pv = d if pv is None else pv + d
    acc_ref[j] = alpha * acc_ref[j] + pv
    m_ref[j] = m_new


def _init_state(m_ref, acc_ref, n_maps, sink_ref):
    tq = m_ref.shape[-1]
    for j in range(n_maps):
        if sink_ref is None:
            m_ref[j] = jnp.full((1, tq), NEG, F32)
            acc_ref[j] = jnp.zeros((V_ROWS, tq), F32)
        else:
            m_ref[j] = jnp.full((1, tq), sink_ref[j] * LOG2E, F32)
            acc_ref[j] = jnp.concatenate([jnp.zeros((HEAD_DIM, tq), F32), jnp.ones((ONES_ROWS, tq), F32)], axis=0)


def _normalised(acc_ref, j):
    acc = acc_ref[j]
    return acc[0:HEAD_DIM, :] / acc[HEAD_DIM:HEAD_DIM + 1, :]


def _finish_plain(o_ref, acc_ref, n_maps):
    ot = jnp.concatenate([_normalised(acc_ref, j) for j in range(n_maps)], axis=0)
    o_ref[0] = ot.T.astype(BF16)


def _finish_diff(o_ref, acc_ref, lamp_ref, subw_ref, n_maps, lambda_init):
    lp = lamp_ref[...]
    lam = (jnp.exp(jnp.sum(lp[0:1] * lp[1:2], axis=1, keepdims=True))
           - jnp.exp(jnp.sum(lp[2:3] * lp[3:4], axis=1, keepdims=True)) + lambda_init)
    outs = []
    for hd in range(n_maps // 2):
        o = _normalised(acc_ref, 2 * hd) - lam * _normalised(acc_ref, 2 * hd + 1)
        ms = jnp.mean(o * o, axis=0, keepdims=True)
        outs.append(o * lax.rsqrt(ms + EPS) * subw_ref[...] * (1.0 - lambda_init))
    o_ref[0] = jnp.concatenate(outs, axis=0).T.astype(BF16)


def _dense_attn_kernel(*refs, n_maps, dq, grouped, has_latent, has_sink, diff_lambda_init):
    refs = list(refs)
    q_ref, kc_ref, vct_ref = refs[:3]
    pos = 3
    k_ref = vt_ref = sink_ref = lamp_ref = subw_ref = None
    if has_latent:
        k_ref, vt_ref = refs[pos:pos + 2]
        pos += 2
    if has_sink:
        sink_ref = refs[pos]
        pos += 1
    if diff_lambda_init is not None:
        lamp_ref, subw_ref = refs[pos:pos + 2]
        pos += 2
    o_ref, qpad_ref, m_ref, acc_ref = refs[pos:pos + 4]

    _fill_qpad(q_ref, qpad_ref, n_maps, dq, grouped)
    _init_state(m_ref, acc_ref, n_maps, sink_ref)
    for c in range(kc_ref.shape[1]):
        _softmax_step(kc_ref.at[0, c], vct_ref.at[0, c], qpad_ref, m_ref, acc_ref, n_maps, dq, grouped)
    if has_latent:
        s_refs = refs[pos + 4:pos + 4 + STAGES]
        bm_refs = refs[pos + 4 + STAGES:pos + 4 + 2 * STAGES]
        ahead = STAGES // 2
        group = s_refs[0].shape[0] // k_ref.shape[3]
        n_units = n_maps * (k_ref.shape[1] // group)
        score = functools.partial(_score_unit, k_ref, qpad_ref=qpad_ref, n_maps=n_maps, dq=dq, grouped=grouped,
                                  group=group)
        softmax = functools.partial(_softmax_unit, vt_ref=vt_ref, m_ref=m_ref, acc_ref=acc_ref, n_maps=n_maps,
                                    group=group)
        for k in range(ahead):
            score(k, s_ref=s_refs[k], bm_ref=bm_refs[k])

        def body(i, carry):
            u = STAGES * i
            for k in range(STAGES):
                nxt = (k + ahead) % STAGES
                score(jnp.minimum(u + k + ahead, n_units - 1), s_ref=s_refs[nxt], bm_ref=bm_refs[nxt])
                softmax(s_refs[k], bm_refs[k], u=u + k)
            return carry
        lax.fori_loop(0, n_units // STAGES, body, 0)
    if diff_lambda_init is None:
        _finish_plain(o_ref, acc_ref, n_maps)
    else:
        _finish_diff(o_ref, acc_ref, lamp_ref, subw_ref, n_maps, diff_lambda_init)


def _dense_attention(q, k_ctx, vt_ctx, k_lat, vt_lat, *, n_maps, dq, sink=None, lamp=None, subw=None,
                     diff_lambda_init=None):
    bsz, t, qw = q.shape
    grouped = k_ctx.shape[2] * K_TILE != n_maps * dq
    tq = min(Q_TILE, t)
    has_latent = k_lat is not None
    whole = _whole_per_batch

    in_specs = [pl.BlockSpec((1, tq, qw), lambda b, i: (b, i, 0)), whole(k_ctx), whole(vt_ctx)]
    args = [q, k_ctx, vt_ctx]
    if has_latent:
        in_specs += [whole(k_lat), whole(vt_lat)]
        args += [k_lat, vt_lat]
    if sink is not None:
        in_specs.append(pl.BlockSpec(memory_space=pltpu.SMEM))
        args.append(sink)
    if diff_lambda_init is not None:
        in_specs += [pl.BlockSpec(lamp.shape, lambda b, i: (0, 0)), pl.BlockSpec(subw.shape, lambda b, i: (0, 0))]
        args += [lamp, subw]
    scratch = _state_scratch(n_maps, tq)
    if has_latent:
        n_chunks, tk = k_lat.shape[1], k_lat.shape[3]
        group = math.gcd(UNIT_CHUNKS, n_chunks)
        assert (n_maps * (n_chunks // group)) % STAGES == 0 and tk % SCORE_ROWS == 0 and tk % PV_ROWS == 0
        scratch += [pltpu.VMEM((group * tk, tq), F32)] * STAGES + [pltpu.VMEM((8, tq), F32)] * STAGES
    return pl.pallas_call(
        functools.partial(_dense_attn_kernel, n_maps=n_maps, dq=dq, grouped=grouped, has_latent=has_latent,
                          has_sink=sink is not None, diff_lambda_init=diff_lambda_init),
        grid=(bsz, t // tq),
        in_specs=in_specs,
        out_specs=pl.BlockSpec((1, tq, 256), lambda b, i: (b, i, 0)),
        out_shape=jax.ShapeDtypeStruct((bsz, t, 256), BF16),
        scratch_shapes=scratch,
        compiler_params=_params("parallel", "parallel"),
        name="dense_attention",
    )(*args)


def _window_attn_kernel(q_ref, kp_ref, kc_ref, kn_ref, vtp_ref, vtc_ref, vtn_ref, kctx_ref, vtctx_ref, sink_ref,
                        o_ref, qpad_ref, m_ref, acc_ref, *, n_maps, seq):
    tq = q_ref.shape[1]
    q0 = pl.program_id(1) * tq
    step = functools.partial(_softmax_step, qpad_ref=qpad_ref, m_ref=m_ref, acc_ref=acc_ref, n_maps=n_maps,
                             dq=HEAD_DIM, grouped=True)
    _fill_qpad(q_ref, qpad_ref, n_maps, HEAD_DIM, True)
    _init_state(m_ref, acc_ref, n_maps, sink_ref)
    for c in range(kctx_ref.shape[1]):
        step(kctx_ref.at[0, c], vtctx_ref.at[0, c])
    k_win = jnp.concatenate([kp_ref[0], kc_ref[0], kn_ref[0]], axis=0)[None]
    vt_win = jnp.concatenate([vtp_ref[0, 0], vtc_ref[0, 0], vtn_ref[0, 0]], axis=2)
    nk = k_win.shape[1]
    kpos = q0 - WINDOW + lax.broadcasted_iota(jnp.int32, (nk, tq), 0)
    qpos = q0 + lax.broadcasted_iota(jnp.int32, (nk, tq), 1)
    valid = (jnp.abs(kpos - qpos) <= WINDOW) & (kpos >= 0) & (kpos < seq)
    step(k_win, vt_win, mask=valid)
    _finish_plain(o_ref, acc_ref, n_maps)


def _window_attention(q, k_ctx, vt_ctx, k_lat, vt_lat, sink):
    bsz, t, qw = q.shape
    nc, key_tiles, tk, kw = k_lat.shape[1:]
    assert key_tiles == 1
    tq = Q_TILE
    n_maps = 4
    k_flat = k_lat.reshape(bsz, t, kw)
    wb = tq // WINDOW
    nwb = t // WINDOW
    per_chunk = tk // WINDOW
    heads = vt_lat.shape[2]
    whole = _whole_per_batch

    def vt_blk(width, chunk, col):
        return pl.BlockSpec((1, 1, heads, V_ROWS, width), lambda b, i: (b, chunk(i), 0, 0, col(i)))

    def prev_blk(i):
        return jnp.maximum(i * wb - 1, 0)

    def next_blk(i):
        return jnp.minimum(i * wb + wb, nwb - 1)

    in_specs = [
        pl.BlockSpec((1, tq, qw), lambda b, i: (b, i, 0)),
        pl.BlockSpec((1, WINDOW, kw), lambda b, i: (b, prev_blk(i), 0)),
        pl.BlockSpec((1, tq, kw), lambda b, i: (b, i, 0)),
        pl.BlockSpec((1, WINDOW, kw), lambda b, i: (b, next_blk(i), 0)),
        vt_blk(WINDOW, lambda i: prev_blk(i) // per_chunk, lambda i: prev_blk(i) % per_chunk),
        vt_blk(tq, lambda i: (i * tq) // tk, lambda i: ((i * tq) % tk) // tq),
        vt_blk(WINDOW, lambda i: next_blk(i) // per_chunk, lambda i: next_blk(i) % per_chunk),
        whole(k_ctx), whole(vt_ctx),
        pl.BlockSpec(memory_space=pltpu.SMEM),
    ]
    return pl.pallas_call(
        functools.partial(_window_attn_kernel, n_maps=n_maps, seq=t),
        grid=(bsz, t // tq),
        in_specs=in_specs,
        out_specs=pl.BlockSpec((1, tq, 256), lambda b, i: (b, i, 0)),
        out_shape=jax.ShapeDtypeStruct((bsz, t, 256), BF16),
        scratch_shapes=_state_scratch(n_maps, tq),
        compiler_params=_params("parallel", "parallel"),
        name="window_attention",
    )(q, k_flat, k_flat, k_flat, vt_lat, vt_lat, vt_lat, k_ctx, vt_ctx, sink)


def _outproj_kernel(x_ref, mod_ref, ya_ref, g3_ref, g3p_ref, g3n_ref, yw_ref, yg_ref, cw_ref, w_ref, g_ref, b_ref,
                    o_ref, *, alpha):
    i = pl.program_id(1)
    last = pl.num_programs(1) - 1
    tm = x_ref.shape[1]
    gb = g3_ref[0, :, 0:256]
    hid = g3_ref[0, :, 256:512] * g3_ref[0, :, 512:768]
    halo = g3p_ref.shape[1]
    h_prev = g3p_ref[0, halo - 1:halo, 256:512] * g3p_ref[0, halo - 1:halo, 512:768]
    h_next = g3n_ref[0, 0:1, 256:512] * g3n_ref[0, 0:1, 512:768]
    h_prev = jnp.where(i == 0, 0.0, h_prev)
    h_next = jnp.where(i == last, 0.0, h_next)
    row = lax.broadcasted_iota(jnp.int32, hid.shape, 0)
    below = jnp.where(row == 0, h_prev, pltpu.roll(hid, 1, axis=0))
    above = jnp.where(row == tm - 1, h_next, pltpu.roll(hid, tm - 1, axis=0))
    yb = gb * (cw_ref[0:1, :] * below + cw_ref[1:2, :] * hid + cw_ref[2:3, :] * above)
    y = jnp.dot(ya_ref[0], w_ref[0:256, :], preferred_element_type=F32)
    y += jnp.dot(yb.astype(BF16), w_ref[256:512, :], preferred_element_type=F32)
    y += jnp.dot(yw_ref[0], w_ref[512:768, :], preferred_element_type=F32)
    y += jnp.dot(yg_ref[0], w_ref[768:1024, :], preferred_element_type=F32)
    gate = mod_ref[0, 5:6, :]
    o_ref[0] = _layer_norm(alpha * x_ref[0] + gate * y, g_ref[...], b_ref[...])


def _out_projection(x, mods, mod_row, ya, g3, yw, yg, conv_w, w_out, ln_g, ln_b, alpha):
    bsz, t, d = x.shape
    tm = min(TOK_TILE, t)
    halo = 8
    hb = tm // halo
    n_halo = t // halo
    const = lambda b, i: (0, 0)
    y_spec = pl.BlockSpec((1, tm, 256), lambda b, i: (b, i, 0))
    return pl.pallas_call(
        functools.partial(_outproj_kernel, alpha=alpha),
        grid=(bsz, t // tm),
        in_specs=[
            pl.BlockSpec((1, tm, d), lambda b, i: (b, i, 0)),
            pl.BlockSpec((1, N_MOD, d), lambda b, i: (mod_row(b), 0, 0)),
            y_spec,
            pl.BlockSpec((1, tm, 768), lambda b, i: (b, i, 0)),
            pl.BlockSpec((1, halo, 768), lambda b, i: (b, jnp.maximum(i * hb - 1, 0), 0)),
            pl.BlockSpec((1, halo, 768), lambda b, i: (b, jnp.minimum(i * hb + hb, n_halo - 1), 0)),
            y_spec, y_spec,
            pl.BlockSpec(conv_w.shape, const),
            pl.BlockSpec(w_out.shape, const, pipeline_mode=pl.Buffered(1)),
            pl.BlockSpec((1, d), const),
            pl.BlockSpec((1, d), const),
        ],
        out_specs=pl.BlockSpec((1, tm, d), lambda b, i: (b, i, 0)),
        out_shape=jax.ShapeDtypeStruct(x.shape, F32),
        compiler_params=_params("parallel", "parallel"),
        name="out_projection",
    )(x, mods, ya, g3, g3, g3, yw, yg, conv_w, w_out, ln_g.reshape(1, d), ln_b.reshape(1, d))


def _rope_tables(rows, dim):
    n_freq = dim // 4
    inv_freq = ROPE_BASE ** (-jnp.arange(n_freq, dtype=F32) / n_freq)
    ang_r = jnp.arange(rows, dtype=F32)[:, None] * inv_freq
    ang_c = jnp.arange(GRID_W, dtype=F32)[:, None] * inv_freq
    ang_r = jnp.broadcast_to(ang_r[:, None, :], (rows, GRID_W, n_freq))
    ang_c = jnp.broadcast_to(ang_c[None, :, :], (rows, GRID_W, n_freq))
    ang = jnp.concatenate([ang_r, ang_r, ang_c, ang_c], axis=-1).reshape(rows * GRID_W, dim)
    cos, sin = jnp.cos(ang), jnp.sin(ang)
    reps = 128 // dim
    cos = jnp.tile(cos, (1, reps))
    sin = jnp.tile(sin, (1, reps))
    first_half = (jnp.arange(128) % (dim // 2)) < (dim // 4)
    sin_next = jnp.where(first_half, -sin, 0.0)
    sin_prev = jnp.where(first_half, 0.0, sin)
    return cos, sin_next, sin_prev


def _identity_tables(t):
    return jnp.ones((t, 128), F32), jnp.zeros((t, 128), F32), jnp.zeros((t, 128), F32)


def kernel(x, c, ctx, c_ctx, w_mod, b_mod, ln_g, ln_b, w_gu1, w_dn1, w_in, w_out, conv_w, lam_q1, lam_k1, lam_q2,
           lam_k2, subln_w, sink, qn_w, kn_w, w_gu2, w_dn2):
    bsz, seq, d = x.shape
    ctx_len = ctx.shape[1]
    depth = w_mod.shape[0]
    alpha = (2.0 * depth) ** 0.25
    assert seq % TOK_TILE == 0 and ctx_len % Q_TILE == 0 and bsz < MOD_ROWS

    tabs_a = _rope_tables(seq // GRID_W, A_QK_DIM)
    tabs_h = _rope_tables(seq // GRID_W, HEAD_DIM)
    tabs_id = _identity_tables(ctx_len)
    gsum = (jnp.arange(256)[:, None] // HEAD_DIM == jnp.arange(256)[None, :] // HEAD_DIM).astype(BF16)

    cvec = jnp.zeros((MOD_ROWS, d), F32).at[:bsz].set(c).at[bsz].set(c_ctx)
    mods_all = _mod_vectors(cvec, w_mod, b_mod).reshape(depth, MOD_ROWS, N_MOD, d)
    lat_row = lambda b: b
    ctx_row = lambda b: bsz

    xc = ctx
    for l in range(depth):
        need_ctx = l < depth - 1
        lambda_init = 0.8 - 0.6 * math.exp(-0.3 * l)
        mods = mods_all[l]
        wgu1, wdn1 = w_gu1[l].astype(BF16), w_dn1[l].astype(BF16)
        wgu2, wdn2 = w_gu2[l].astype(BF16), w_dn2[l].astype(BF16)
        win, wout = w_in[l].astype(BF16), w_out[l].astype(BF16)
        qnw = jnp.tile(qn_w[l], 256 // HEAD_DIM).reshape(1, 256)
        knw = jnp.tile(kn_w[l], 128 // HEAD_DIM).reshape(1, 128)
        lamp = jnp.stack([lam_q1[l], lam_k1[l], lam_q2[l], lam_k2[l]])
        subw = subln_w[l].reshape(HEAD_DIM, 1)

        x = _ffn_step(x, mods, lat_row, 0, wgu1, wdn1, ln_g[l, 0], ln_b[l, 0], alpha)
        xc = _ffn_step(xc, mods, ctx_row, 0, wgu1, wdn1, ln_g[l, 0], ln_b[l, 0], alpha)

        qa, ka, vat, g3, qc, kc, vct, qd, kd, vdt = _in_projection(x, mods, lat_row, win, tabs_a, tabs_h, qnw, knw, gsum)
        (qa_c, ka_c, vat_c, g3_c, qc_c, kc_c, vct_c, qd_c, kd_c, vdt_c) = _in_projection(
            xc, mods, ctx_row, win, tabs_id, tabs_id, qnw, knw, gsum)
        ya = _dense_attention(qa, ka_c, vat_c, ka, vat, n_maps=8, dq=A_QK_DIM, lamp=lamp, subw=subw,
                              diff_lambda_init=lambda_init)
        yw = _window_attention(qc, kc_c, vct_c, kc, vct, sink[l])
        yg = _dense_attention(qd, kd_c, vdt_c, kd, vdt, n_maps=4, dq=HEAD_DIM)
        x = _out_projection(x, mods, lat_row, ya, g3, yw, yg, conv_w[l], wout, ln_g[l, 1], ln_b[l, 1], alpha)

        x = _ffn_step(x, mods, lat_row, 6, wgu2, wdn2, ln_g[l, 2], ln_b[l, 2], alpha)

        if need_ctx:
            ya_c = _dense_attention(qa_c, ka_c, vat_c, None, None, n_maps=8, dq=A_QK_DIM, lamp=lamp, subw=subw,
                                    diff_lambda_init=lambda_init)
            yw_c = _dense_attention(qc_c, kc_c, vct_c, None, None, n_maps=4, dq=HEAD_DIM, sink=sink[l])
            yg_c = _dense_attention(qd_c, kd_c, vdt_c, None, None, n_maps=4, dq=HEAD_DIM)
            xc = _out_projection(xc, mods, ctx_row, ya_c, g3_c, yw_c, yg_c, conv_w[l], wout, ln_g[l, 1], ln_b[l, 1],
                                 alpha)
            xc = _ffn_step(xc, mods, ctx_row, 6, wgu2, wdn2, ln_g[l, 2], ln_b[l, 2], alpha)
    return x
```

```python
import functools
import math

import jax
import jax.numpy as jnp
from jax import lax
from jax.experimental import pallas as pl
from jax.experimental.pallas import tpu as pltpu

F32 = jnp.float32
BF16 = jnp.bfloat16

GRID_W = 64
HEAD_DIM = 64
WINDOW = 128
ROPE_BASE = 10000.0
EPS = 1e-6
NEG = -1e30
A_QK_DIM = 32
N_MOD = 9
MOD_ROWS = 8
VMEM_LIMIT = 56 * 1024 * 1024

OFF_AQ, OFF_AK, OFF_AV = 0, 256, 512
OFF_B = 768
OFF_CQ, OFF_CK, OFF_CV = 1536, 1792, 1920
OFF_DQ, OFF_DK, OFF_DV = 2048, 2304, 2432
IN_W = 2560

TOK_TILE = 512
Q_TILE = 256
FF_CHUNK = 256
SCORE_ROWS = 128
PV_ROWS = 256
UNIT_CHUNKS = 2
STAGES = 4
LOOKAHEAD = 2
ROUNDS_PER_ITER = 4
K_TILE = 128
ONES_ROWS = 16
V_ROWS = HEAD_DIM + ONES_ROWS
LOG2E = math.log2(math.e)


def _params(*sem):
    return pltpu.CompilerParams(dimension_semantics=sem, vmem_limit_bytes=VMEM_LIMIT)


def _layer_norm(r, g, b):
    mu = jnp.mean(r, axis=-1, keepdims=True)
    d = r - mu
    var = jnp.mean(d * d, axis=-1, keepdims=True)
    return d * lax.rsqrt(var + EPS) * g + b


def _mod_kernel(c_ref, w_ref, b_ref, o_ref):
    c = c_ref[...]
    a = (c * jax.nn.sigmoid(c)).astype(BF16)
    o_ref[0] = jnp.dot(a, w_ref[0].astype(BF16), preferred_element_type=F32) + b_ref[0]


def _mod_vectors(cvec, w_mod, b_mod):
    depth, d, n = w_mod.shape
    tn = n // 8
    return pl.pallas_call(
        _mod_kernel,
        grid=(depth, n // tn),
        in_specs=[
            pl.BlockSpec((MOD_ROWS, d), lambda l, j: (0, 0)),
            pl.BlockSpec((1, d, tn), lambda l, j: (l, 0, j)),
            pl.BlockSpec((1, 1, tn), lambda l, j: (l, 0, j)),
        ],
        out_specs=pl.BlockSpec((1, MOD_ROWS, tn), lambda l, j: (l, 0, j)),
        out_shape=jax.ShapeDtypeStruct((depth, MOD_ROWS, n), F32),
        compiler_params=_params("parallel", "parallel"),
        name="mod_vectors",
    )(cvec, w_mod, b_mod.reshape(depth, 1, n))


def _ffn_kernel(x_ref, mod_ref, wgu_ref, wdn_ref, g_ref, b_ref, o_ref, acc_ref, *, mod_base, alpha):
    x = x_ref[0]
    shift = mod_ref[0, mod_base:mod_base + 1, :]
    scale = mod_ref[0, mod_base + 1:mod_base + 2, :]
    gate = mod_ref[0, mod_base + 2:mod_base + 3, :]
    h = (x * (1.0 + scale) + shift).astype(BF16)
    d_ff = wdn_ref.shape[0]
    for c in range(d_ff // FF_CHUNK):
        lo = c * FF_CHUNK
        g = jnp.dot(h, wgu_ref[:, lo:lo + FF_CHUNK], preferred_element_type=F32)
        u = jnp.dot(h, wgu_ref[:, d_ff + lo:d_ff + lo + FF_CHUNK], preferred_element_type=F32)
        a = (g * jax.nn.sigmoid(g) * u).astype(BF16)
        y = jnp.dot(a, wdn_ref[lo:lo + FF_CHUNK, :], preferred_element_type=F32)
        if c == 0:
            acc_ref[...] = y
        else:
            acc_ref[...] += y
    r = alpha * x + (0.5 * gate) * acc_ref[...]
    o_ref[0] = _layer_norm(r, g_ref[...], b_ref[...])


def _ffn_step(x, mods, mod_row, mod_base, w_gu, w_dn, ln_g, ln_b, alpha):
    bsz, t, d = x.shape
    d_ff = w_dn.shape[0]
    tm = min(TOK_TILE, t)
    const = lambda b, i: (0, 0)
    return pl.pallas_call(
        functools.partial(_ffn_kernel, mod_base=mod_base, alpha=alpha),
        grid=(bsz, t // tm),
        in_specs=[
            pl.BlockSpec((1, tm, d), lambda b, i: (b, i, 0)),
            pl.BlockSpec((1, N_MOD, d), lambda b, i: (mod_row(b), 0, 0)),
            pl.BlockSpec((d, 2 * d_ff), const, pipeline_mode=pl.Buffered(1)),
            pl.BlockSpec((d_ff, d), const, pipeline_mode=pl.Buffered(1)),
            pl.BlockSpec((1, d), const),
            pl.BlockSpec((1, d), const),
        ],
        out_specs=pl.BlockSpec((1, tm, d), lambda b, i: (b, i, 0)),
        out_shape=jax.ShapeDtypeStruct(x.shape, F32),
        scratch_shapes=[pltpu.VMEM((tm, d), F32)],
        compiler_params=_params("parallel", "parallel"),
        name="ffn_step",
    )(x, mods, w_gu, w_dn, ln_g.reshape(1, d), ln_b.reshape(1, d))


def _rope(x, cos, sin_next, sin_prev, quarter):
    width = x.shape[1]
    reps = width // cos.shape[1]
    if reps > 1:
        cos = jnp.concatenate([cos] * reps, axis=1)
        sin_next = jnp.concatenate([sin_next] * reps, axis=1)
        sin_prev = jnp.concatenate([sin_prev] * reps, axis=1)
    x_next = pltpu.roll(x, width - quarter, axis=1)
    x_prev = pltpu.roll(x, quarter, axis=1)
    return x * cos + x_next * sin_next + x_prev * sin_prev


def _head_rms(x, gsum_ref, w):
    x2 = x * x
    hi = x2.astype(BF16)
    lo = (x2 - hi.astype(F32)).astype(BF16)
    gs = gsum_ref[:x.shape[1], :x.shape[1]]
    ss = jnp.dot(hi, gs, preferred_element_type=F32) + jnp.dot(lo, gs, preferred_element_type=F32)
    return x * lax.rsqrt(ss * (1.0 / HEAD_DIM) + EPS) * w


def _store_k(k_ref, k):
    for kt in range(k.shape[1] // K_TILE):
        k_ref[0, 0, kt] = k[:, kt * K_TILE:(kt + 1) * K_TILE].astype(BF16)


def _store_vt(vt_ref, v):
    vt = v.T
    tm = v.shape[0]
    for hd in range(v.shape[1] // HEAD_DIM):
        vt_ref[0, 0, hd, 0:HEAD_DIM, :] = vt[hd * HEAD_DIM:(hd + 1) * HEAD_DIM, :].astype(BF16)
        vt_ref[0, 0, hd, HEAD_DIM:V_ROWS, :] = jnp.ones((ONES_ROWS, tm), BF16)


def _inproj_kernel(x_ref, mod_ref, w_ref, ca_ref, san_ref, sap_ref, ch_ref, shn_ref, shp_ref,
                   qnw_ref, knw_ref, gsum_ref,
                   qa_ref, ka_ref, vat_ref, g3_ref, qc_ref, kc_ref, vct_ref, qd_ref, kd_ref, vdt_ref):
    x = x_ref[0]
    shift = mod_ref[0, 3:4, :]
    scale = mod_ref[0, 4:5, :]
    h = (x * (1.0 + scale) + shift).astype(BF16)

    def seg(lo, hi):
        return jnp.dot(h, w_ref[:, lo:hi], preferred_element_type=F32)

    rope_a = functools.partial(_rope, cos=ca_ref[...], sin_next=san_ref[...], sin_prev=sap_ref[...],
                               quarter=A_QK_DIM // 4)
    rope_h = functools.partial(_rope, cos=ch_ref[...], sin_next=shn_ref[...], sin_prev=shp_ref[...],
                               quarter=HEAD_DIM // 4)
    qa_ref[0] = (rope_a(seg(OFF_AQ, OFF_AK)) * (LOG2E * A_QK_DIM ** -0.5)).astype(BF16)
    _store_k(ka_ref, rope_a(seg(OFF_AK, OFF_AV)))
    _store_vt(vat_ref, seg(OFF_AV, OFF_B))
    g3_ref[0] = seg(OFF_B, OFF_CQ)
    qc_ref[0] = (rope_h(seg(OFF_CQ, OFF_CK)) * (LOG2E * HEAD_DIM ** -0.5)).astype(BF16)
    _store_k(kc_ref, rope_h(seg(OFF_CK, OFF_CV)))
    _store_vt(vct_ref, seg(OFF_CV, OFF_DQ))
    qd = _head_rms(seg(OFF_DQ, OFF_DK), gsum_ref, qnw_ref[...])
    qd_ref[0] = (rope_h(qd) * (LOG2E * HEAD_DIM ** -0.5)).astype(BF16)
    kd = _head_rms(seg(OFF_DK, OFF_DV), gsum_ref, knw_ref[...])
    _store_k(kd_ref, rope_h(kd))
    _store_vt(vdt_ref, seg(OFF_DV, IN_W))


def _in_projection(x, mods, mod_row, w_in, tabs_a, tabs_h, qnw, knw, gsum):
    bsz, t, d = x.shape
    tm = min(TOK_TILE, t)
    nc = t // tm
    const = lambda b, i: (0, 0)
    tab = pl.BlockSpec((tm, 128), lambda b, i: (i, 0))
    q_spec = pl.BlockSpec((1, tm, 256), lambda b, i: (b, i, 0))

    def k_spec(w):
        return pl.BlockSpec((1, 1, w // K_TILE, tm, K_TILE), lambda b, i: (b, i, 0, 0, 0))

    def vt_spec(w):
        return pl.BlockSpec((1, 1, w // HEAD_DIM, V_ROWS, tm), lambda b, i: (b, i, 0, 0, 0))

    def q_shape():
        return jax.ShapeDtypeStruct((bsz, t, 256), BF16)

    def k_shape(w):
        return jax.ShapeDtypeStruct((bsz, nc, w // K_TILE, tm, K_TILE), BF16)

    def vt_shape(w):
        return jax.ShapeDtypeStruct((bsz, nc, w // HEAD_DIM, V_ROWS, tm), BF16)

    return pl.pallas_call(
        _inproj_kernel,
        grid=(bsz, nc),
        in_specs=[
            pl.BlockSpec((1, tm, d), lambda b, i: (b, i, 0)),
            pl.BlockSpec((1, N_MOD, d), lambda b, i: (mod_row(b), 0, 0)),
            pl.BlockSpec((d, IN_W), const, pipeline_mode=pl.Buffered(1)),
            tab, tab, tab, tab, tab, tab,
            pl.BlockSpec((1, 256), const),
            pl.BlockSpec((1, 128), const),
            pl.BlockSpec((256, 256), const),
        ],
        out_specs=[q_spec, k_spec(256), vt_spec(256),
                   pl.BlockSpec((1, tm, 768), lambda b, i: (b, i, 0)),
                   q_spec, k_spec(128), vt_spec(128),
                   q_spec, k_spec(128), vt_spec(128)],
        out_shape=[q_shape(), k_shape(256), vt_shape(256),
                   jax.ShapeDtypeStruct((bsz, t, 768), F32),
                   q_shape(), k_shape(128), vt_shape(128),
                   q_shape(), k_shape(128), vt_shape(128)],
        compiler_params=_params("parallel", "parallel"),
        name="in_projection",
    )(x, mods, w_in, *tabs_a, *tabs_h, qnw, knw, gsum)


def _whole_per_batch(arr):
    zeros = (0,) * (arr.ndim - 1)
    return pl.BlockSpec((1,) + arr.shape[1:], lambda b, i: (b,) + zeros, pipeline_mode=pl.Buffered(1))


def _state_scratch(n_maps, tq):
    return [pltpu.VMEM((n_maps, K_TILE, tq), BF16),
            pltpu.VMEM((n_maps, 1, tq), F32),
            pltpu.VMEM((n_maps, V_ROWS, tq), F32)]


def _key_offset(j, dq, grouped):
    return (j // 2) * dq if grouped else j * dq


def _fill_qpad(q_ref, qpad_ref, n_maps, dq, grouped):
    tq = q_ref.shape[1]
    qt = q_ref[0].astype(F32).T
    for j in range(n_maps):
        row = _key_offset(j, dq, grouped) % K_TILE
        parts = []
        if row:
            parts.append(jnp.zeros((row, tq), F32))
        parts.append(qt[j * dq:(j + 1) * dq, :])
        if K_TILE - row - dq:
            parts.append(jnp.zeros((K_TILE - row - dq, tq), F32))
        qpad_ref[j] = jnp.concatenate(parts, axis=0).astype(BF16)


def _softmax_step(k_c, vt_c, qpad_ref, m_ref, acc_ref, n_maps, dq, grouped, mask=None):
    for j in range(n_maps):
        k_tile = k_c[_key_offset(j, dq, grouped) // K_TILE]
        s = jnp.dot(k_tile, qpad_ref[j], preferred_element_type=F32)
        if mask is not None:
            s = jnp.where(mask, s, NEG)
        m_old = m_ref[j]
        m_new = jnp.maximum(m_old, jnp.max(s, axis=0, keepdims=True))
        p = jnp.exp2(s - m_new)
        alpha = jnp.exp2(m_old - m_new)
        pv = jnp.dot(vt_c[j // 2], p.astype(BF16), preferred_element_type=F32)
        acc_ref[j] = alpha * acc_ref[j] + pv
        m_ref[j] = m_new


def _score_unit(k_ref, u, qpad_ref, s_ref, bm_ref, n_maps, dq, grouped, group):
    j = u % n_maps
    c0 = (u // n_maps) * group
    kt = _key_offset(j, dq, grouped) // K_TILE
    tk = k_ref.shape[3]
    tq = s_ref.shape[1]
    w = qpad_ref[j]
    bm = None
    for g in range(group):
        for r in range(tk // SCORE_ROWS):
            s = jnp.dot(k_ref[0, c0 + g, kt, r * SCORE_ROWS:(r + 1) * SCORE_ROWS, :], w,
                        preferred_element_type=F32)
            s_ref[g * tk + r * SCORE_ROWS:g * tk + (r + 1) * SCORE_ROWS, :] = s
            part = jnp.max(s.reshape(SCORE_ROWS // 8, 8, tq), axis=0)
            bm = part if bm is None else jnp.maximum(bm, part)
    bm_ref[...] = bm


def _softmax_unit(s_ref, bm_ref, vt_ref, u, m_ref, acc_ref, n_maps, group, row0):
    j = u % n_maps
    c0 = (u // n_maps) * group
    tk = vt_ref.shape[-1]
    m_old = m_ref[j]
    m_new = jnp.maximum(m_old, jnp.max(bm_ref[...], axis=0, keepdims=True))
    alpha = jnp.exp2(m_old - m_new)
    pv = None
    for g in range(group):
        for r in range(tk // PV_ROWS):
            rows = pl.ds(pl.multiple_of(row0 + (g * tk + r * PV_ROWS), PV_ROWS), PV_ROWS)
            p = jnp.exp2(s_ref[rows, :] - m_new)
            d = jnp.dot(vt_ref[0, c0 + g, j // 2, :, r * PV_ROWS:(r + 1) * PV_ROWS], p.astype(BF16),
                        preferred_element_type=F32)
            pv = d if pv is None else pv + d
    acc_ref[j] = alpha * acc_ref[j] + pv
    m_ref[j] = m_new


def _init_state(m_ref, acc_ref, n_maps, sink_ref):
    tq = m_ref.shape[-1]
    for j in range(n_maps):
        if sink_ref is None:
            m_ref[j] = jnp.full((1, tq), NEG, F32)
            acc_ref[j] = jnp.zeros((V_ROWS, tq), F32)
        else:
            m_ref[j] = jnp.full((1, tq), sink_ref[j] * LOG2E, F32)
            acc_ref[j] = jnp.concatenate([jnp.zeros((HEAD_DIM, tq), F32), jnp.ones((ONES_ROWS, tq), F32)], axis=0)


def _normalised(acc_ref, j):
    acc = acc_ref[j]
    return acc[0:HEAD_DIM, :] / acc[HEAD_DIM:HEAD_DIM + 1, :]


def _finish_plain(o_ref, acc_ref, n_maps):
    ot = jnp.concatenate([_normalised(acc_ref, j) for j in range(n_maps)], axis=0)
    o_ref[0] = ot.T.astype(BF16)


def _finish_diff(o_ref, acc_ref, lamp_ref, subw_ref, n_maps, lambda_init):
    lp = lamp_ref[...]
    lam = (jnp.exp(jnp.sum(lp[0:1] * lp[1:2], axis=1, keepdims=True))
           - jnp.exp(jnp.sum(lp[2:3] * lp[3:4], axis=1, keepdims=True)) + lambda_init)
    outs = []
    for hd in range(n_maps // 2):
        o = _normalised(acc_ref, 2 * hd) - lam * _normalised(acc_ref, 2 * hd + 1)
        ms = jnp.mean(o * o, axis=0, keepdims=True)
        outs.append(o * lax.rsqrt(ms + EPS) * subw_ref[...] * (1.0 - lambda_init))
    o_ref[0] = jnp.concatenate(outs, axis=0).T.astype(BF16)


def _dense_attn_kernel(*refs, n_maps, dq, grouped, has_latent, has_sink, diff_lambda_init):
    refs = list(refs)
    q_ref, kc_ref, vct_ref = refs[:3]
    pos = 3
    k_ref = vt_ref = sink_ref = lamp_ref = subw_ref = None
    if has_latent:
        k_ref, vt_ref, zero_ref = refs[pos:pos + 3]
        pos += 3
    if has_sink:
        sink_ref = refs[pos]
        pos += 1
    if diff_lambda_init is not None:
        lamp_ref, subw_ref = refs[pos:pos + 2]
        pos += 2
    o_ref, qpad_ref, m_ref, acc_ref = refs[pos:pos + 4]

    _fill_qpad(q_ref, qpad_ref, n_maps, dq, grouped)
    _init_state(m_ref, acc_ref, n_maps, sink_ref)
    for c in range(kc_ref.shape[1]):
        _softmax_step(kc_ref.at[0, c], vct_ref.at[0, c], qpad_ref, m_ref, acc_ref, n_maps, dq, grouped)
    if has_latent:
        s_refs = refs[pos + 4:pos + 4 + STAGES]
        bm_refs = refs[pos + 4 + STAGES:pos + 4 + 2 * STAGES]
        ahead = LOOKAHEAD
        group = s_refs[0].shape[0] // k_ref.shape[3]
        n_units = n_maps * (k_ref.shape[1] // group)
        score = functools.partial(_score_unit, k_ref, qpad_ref=qpad_ref, n_maps=n_maps, dq=dq, grouped=grouped,
                                  group=group)
        softmax = functools.partial(_softmax_unit, vt_ref=vt_ref, m_ref=m_ref, acc_ref=acc_ref, n_maps=n_maps,
                                    group=group, row0=zero_ref[0])
        for k in range(ahead):
            score(k, s_ref=s_refs[k], bm_ref=bm_refs[k])

        per_iter = STAGES * math.gcd(n_units // STAGES, ROUNDS_PER_ITER)

        def body(i, carry):
            u = per_iter * i
            for k in range(per_iter):
                cur, nxt = k % STAGES, (k + ahead) % STAGES
                score(jnp.minimum(u + k + ahead, n_units - 1), s_ref=s_refs[nxt], bm_ref=bm_refs[nxt])
                softmax(s_refs[cur], bm_refs[cur], u=u + k)
            return carry
        lax.fori_loop(0, n_units // per_iter, body, 0)
    if diff_lambda_init is None:
        _finish_plain(o_ref, acc_ref, n_maps)
    else:
        _finish_diff(o_ref, acc_ref, lamp_ref, subw_ref, n_maps, diff_lambda_init)


def _dense_attention(q, k_ctx, vt_ctx, k_lat, vt_lat, *, n_maps, dq, sink=None, lamp=None, subw=None,
                     diff_lambda_init=None):
    bsz, t, qw = q.shape
    grouped = k_ctx.shape[2] * K_TILE != n_maps * dq
    tq = min(Q_TILE, t)
    has_latent = k_lat is not None
    whole = _whole_per_batch

    in_specs = [pl.BlockSpec((1, tq, qw), lambda b, i: (b, i, 0)), whole(k_ctx), whole(vt_ctx)]
    args = [q, k_ctx, vt_ctx]
    if has_latent:
        in_specs += [whole(k_lat), whole(vt_lat), pl.BlockSpec(memory_space=pltpu.SMEM)]
        args += [k_lat, vt_lat, jnp.zeros((1,), jnp.int32)]
    if sink is not None:
        in_specs.append(pl.BlockSpec(memory_space=pltpu.SMEM))
        args.append(sink)
    if diff_lambda_init is not None:
        in_specs += [pl.BlockSpec(lamp.shape, lambda b, i: (0, 0)), pl.BlockSpec(subw.shape, lambda b, i: (0, 0))]
        args += [lamp, subw]
    scratch = _state_scratch(n_maps, tq)
    if has_latent:
        n_chunks, tk = k_lat.shape[1], k_lat.shape[3]
        group = math.gcd(UNIT_CHUNKS, n_chunks)
        assert (n_maps * (n_chunks // group)) % STAGES == 0 and tk % SCORE_ROWS == 0 and tk % PV_ROWS == 0
        scratch += [pltpu.VMEM((group * tk, tq), F32)] * STAGES + [pltpu.VMEM((8, tq), F32)] * STAGES
    return pl.pallas_call(
        functools.partial(_dense_attn_kernel, n_maps=n_maps, dq=dq, grouped=grouped, has_latent=has_latent,
                          has_sink=sink is not None, diff_lambda_init=diff_lambda_init),
        grid=(bsz, t // tq),
        in_specs=in_specs,
        out_specs=pl.BlockSpec((1, tq, 256), lambda b, i: (b, i, 0)),
        out_shape=jax.ShapeDtypeStruct((bsz, t, 256), BF16),
        scratch_shapes=scratch,
        compiler_params=_params("parallel", "parallel"),
        name="dense_attention",
    )(*args)


def _window_attn_kernel(q_ref, kp_ref, kc_ref, kn_ref, vtp_ref, vtc_ref, vtn_ref, kctx_ref, vtctx_ref, sink_ref,
                        o_ref, qpad_ref, m_ref, acc_ref, *, n_maps, seq):
    tq = q_ref.shape[1]
    q0 = pl.program_id(1) * tq
    step = functools.partial(_softmax_step, qpad_ref=qpad_ref, m_ref=m_ref, acc_ref=acc_ref, n_maps=n_maps,
                             dq=HEAD_DIM, grouped=True)
    _fill_qpad(q_ref, qpad_ref, n_maps, HEAD_DIM, True)
    _init_state(m_ref, acc_ref, n_maps, sink_ref)
    for c in range(kctx_ref.shape[1]):
        step(kctx_ref.at[0, c], vtctx_ref.at[0, c])
    k_win = jnp.concatenate([kp_ref[0], kc_ref[0], kn_ref[0]], axis=0)[None]
    vt_win = jnp.concatenate([vtp_ref[0, 0], vtc_ref[0, 0], vtn_ref[0, 0]], axis=2)
    nk = k_win.shape[1]
    kpos = q0 - WINDOW + lax.broadcasted_iota(jnp.int32, (nk, tq), 0)
    qpos = q0 + lax.broadcasted_iota(jnp.int32, (nk, tq), 1)
    valid = (jnp.abs(kpos - qpos) <= WINDOW) & (kpos >= 0) & (kpos < seq)
    step(k_win, vt_win, mask=valid)
    _finish_plain(o_ref, acc_ref, n_maps)


def _window_attention(q, k_ctx, vt_ctx, k_lat, vt_lat, sink):
    bsz, t, qw = q.shape
    nc, key_tiles, tk, kw = k_lat.shape[1:]
    assert key_tiles == 1
    tq = Q_TILE
    n_maps = 4
    k_flat = k_lat.reshape(bsz, t, kw)
    wb = tq // WINDOW
    nwb = t // WINDOW
    per_chunk = tk // WINDOW
    heads = vt_lat.shape[2]
    whole = _whole_per_batch

    def vt_blk(width, chunk, col):
        return pl.BlockSpec((1, 1, heads, V_ROWS, width), lambda b, i: (b, chunk(i), 0, 0, col(i)))

    def prev_blk(i):
        return jnp.maximum(i * wb - 1, 0)

    def next_blk(i):
        return jnp.minimum(i * wb + wb, nwb - 1)

    in_specs = [
        pl.BlockSpec((1, tq, qw), lambda b, i: (b, i, 0)),
        pl.BlockSpec((1, WINDOW, kw), lambda b, i: (b, prev_blk(i), 0)),
        pl.BlockSpec((1, tq, kw), lambda b, i: (b, i, 0)),
        pl.BlockSpec((1, WINDOW, kw), lambda b, i: (b, next_blk(i), 0)),
        vt_blk(WINDOW, lambda i: prev_blk(i) // per_chunk, lambda i: prev_blk(i) % per_chunk),
        vt_blk(tq, lambda i: (i * tq) // tk, lambda i: ((i * tq) % tk) // tq),
        vt_blk(WINDOW, lambda i: next_blk(i) // per_chunk, lambda i: next_blk(i) % per_chunk),
        whole(k_ctx), whole(vt_ctx),
        pl.BlockSpec(memory_space=pltpu.SMEM),
    ]
    return pl.pallas_call(
        functools.partial(_window_attn_kernel, n_maps=n_maps, seq=t),
        grid=(bsz, t // tq),
        in_specs=in_specs,
        out_specs=pl.BlockSpec((1, tq, 256), lambda b, i: (b, i, 0)),
        out_shape=jax.ShapeDtypeStruct((bsz, t, 256), BF16),
        scratch_shapes=_state_scratch(n_maps, tq),
        compiler_params=_params("parallel", "parallel"),
        name="window_attention",
    )(q, k_flat, k_flat, k_flat, vt_lat, vt_lat, vt_lat, k_ctx, vt_ctx, sink)


def _outproj_kernel(x_ref, mod_ref, ya_ref, g3_ref, g3p_ref, g3n_ref, yw_ref, yg_ref, cw_ref, w_ref, g_ref, b_ref,
                    o_ref, *, alpha):
    i = pl.program_id(1)
    last = pl.num_programs(1) - 1
    tm = x_ref.shape[1]
    gb = g3_ref[0, :, 0:256]
    hid = g3_ref[0, :, 256:512] * g3_ref[0, :, 512:768]
    halo = g3p_ref.shape[1]
    h_prev = g3p_ref[0, halo - 1:halo, 256:512] * g3p_ref[0, halo - 1:halo, 512:768]
    h_next = g3n_ref[0, 0:1, 256:512] * g3n_ref[0, 0:1, 512:768]
    h_prev = jnp.where(i == 0, 0.0, h_prev)
    h_next = jnp.where(i == last, 0.0, h_next)
    row = lax.broadcasted_iota(jnp.int32, hid.shape, 0)
    below = jnp.where(row == 0, h_prev, pltpu.roll(hid, 1, axis=0))
    above = jnp.where(row == tm - 1, h_next, pltpu.roll(hid, tm - 1, axis=0))
    yb = gb * (cw_ref[0:1, :] * below + cw_ref[1:2, :] * hid + cw_ref[2:3, :] * above)
    y = jnp.dot(ya_ref[0], w_ref[0:256, :], preferred_element_type=F32)
    y += jnp.dot(yb.astype(BF16), w_ref[256:512, :], preferred_element_type=F32)
    y += jnp.dot(yw_ref[0], w_ref[512:768, :], preferred_element_type=F32)
    y += jnp.dot(yg_ref[0], w_ref[768:1024, :], preferred_element_type=F32)
    gate = mod_ref[0, 5:6, :]
    o_ref[0] = _layer_norm(alpha * x_ref[0] + gate * y, g_ref[...], b_ref[...])


def _out_projection(x, mods, mod_row, ya, g3, yw, yg, conv_w, w_out, ln_g, ln_b, alpha):
    bsz, t, d = x.shape
    tm = min(TOK_TILE, t)
    halo = 8
    hb = tm // halo
    n_halo = t // halo
    const = lambda b, i: (0, 0)
    y_spec = pl.BlockSpec((1, tm, 256), lambda b, i: (b, i, 0))
    return pl.pallas_call(
        functools.partial(_outproj_kernel, alpha=alpha),
        grid=(bsz, t // tm),
        in_specs=[
            pl.BlockSpec((1, tm, d), lambda b, i: (b, i, 0)),
            pl.BlockSpec((1, N_MOD, d), lambda b, i: (mod_row(b), 0, 0)),
            y_spec,
            pl.BlockSpec((1, tm, 768), lambda b, i: (b, i, 0)),
            pl.BlockSpec((1, halo, 768), lambda b, i: (b, jnp.maximum(i * hb - 1, 0), 0)),
            pl.BlockSpec((1, halo, 768), lambda b, i: (b, jnp.minimum(i * hb + hb, n_halo - 1), 0)),
            y_spec, y_spec,
            pl.BlockSpec(conv_w.shape, const),
            pl.BlockSpec(w_out.shape, const, pipeline_mode=pl.Buffered(1)),
            pl.BlockSpec((1, d), const),
            pl.BlockSpec((1, d), const),
        ],
        out_specs=pl.BlockSpec((1, tm, d), lambda b, i: (b, i, 0)),
        out_shape=jax.ShapeDtypeStruct(x.shape, F32),
        compiler_params=_params("parallel", "parallel"),
        name="out_projection",
    )(x, mods, ya, g3, g3, g3, yw, yg, conv_w, w_out, ln_g.reshape(1, d), ln_b.reshape(1, d))


def _rope_tables(rows, dim):
    n_freq = dim // 4
    inv_freq = ROPE_BASE ** (-jnp.arange(n_freq, dtype=F32) / n_freq)
    ang_r = jnp.arange(rows, dtype=F32)[:, None] * inv_freq
    ang_c = jnp.arange(GRID_W, dtype=F32)[:, None] * inv_freq
    ang_r = jnp.broadcast_to(ang_r[:, None, :], (rows, GRID_W, n_freq))
    ang_c = jnp.broadcast_to(ang_c[None, :, :], (rows, GRID_W, n_freq))
    ang = jnp.concatenate([ang_r, ang_r, ang_c, ang_c], axis=-1).reshape(rows * GRID_W, dim)
    cos, sin = jnp.cos(ang), jnp.sin(ang)
    reps = 128 // dim
    cos = jnp.tile(cos, (1, reps))
    sin = jnp.tile(sin, (1, reps))
    first_half = (jnp.arange(128) % (dim // 2)) < (dim // 4)
    sin_next = jnp.where(first_half, -sin, 0.0)
    sin_prev = jnp.where(first_half, 0.0, sin)
    return cos, sin_next, sin_prev


def _identity_tables(t):
    return jnp.ones((t, 128), F32), jnp.zeros((t, 128), F32), jnp.zeros((t, 128), F32)


def kernel(x, c, ctx, c_ctx, w_mod, b_mod, ln_g, ln_b, w_gu1, w_dn1, w_in, w_out, conv_w, lam_q1, lam_k1, lam_q2,
           lam_k2, subln_w, sink, qn_w, kn_w, w_gu2, w_dn2):
    bsz, seq, d = x.shape
    ctx_len = ctx.shape[1]
    depth = w_mod.shape[0]
    alpha = (2.0 * depth) ** 0.25
    assert seq % TOK_TILE == 0 and ctx_len % Q_TILE == 0 and bsz < MOD_ROWS

    tabs_a = _rope_tables(seq // GRID_W, A_QK_DIM)
    tabs_h = _rope_tables(seq // GRID_W, HEAD_DIM)
    tabs_id = _identity_tables(ctx_len)
    gsum = (jnp.arange(256)[:, None] // HEAD_DIM == jnp.arange(256)[None, :] // HEAD_DIM).astype(BF16)

    cvec = jnp.zeros((MOD_ROWS, d), F32).at[:bsz].set(c).at[bsz].set(c_ctx)
    mods_all = _mod_vectors(cvec, w_mod, b_mod).reshape(depth, MOD_ROWS, N_MOD, d)
    lat_row = lambda b: b
    ctx_row = lambda b: bsz

    xc = ctx
    for l in range(depth):
        need_ctx = l < depth - 1
        lambda_init = 0.8 - 0.6 * math.exp(-0.3 * l)
        mods = mods_all[l]
        wgu1, wdn1 = w_gu1[l].astype(BF16), w_dn1[l].astype(BF16)
        wgu2, wdn2 = w_gu2[l].astype(BF16), w_dn2[l].astype(BF16)
        win, wout = w_in[l].astype(BF16), w_out[l].astype(BF16)
        qnw = jnp.tile(qn_w[l], 256 // HEAD_DIM).reshape(1, 256)
        knw = jnp.tile(kn_w[l], 128 // HEAD_DIM).reshape(1, 128)
        lamp = jnp.stack([lam_q1[l], lam_k1[l], lam_q2[l], lam_k2[l]])
        subw = subln_w[l].reshape(HEAD_DIM, 1)

        x = _ffn_step(x, mods, lat_row, 0, wgu1, wdn1, ln_g[l, 0], ln_b[l, 0], alpha)
        xc = _ffn_step(xc, mods, ctx_row, 0, wgu1, wdn1, ln_g[l, 0], ln_b[l, 0], alpha)

        qa, ka, vat, g3, qc, kc, vct, qd, kd, vdt = _in_projection(x, mods, lat_row, win, tabs_a, tabs_h, qnw, knw, gsum)
        (qa_c, ka_c, vat_c, g3_c, qc_c, kc_c, vct_c, qd_c, kd_c, vdt_c) = _in_projection(
            xc, mods, ctx_row, win, tabs_id, tabs_id, qnw, knw, gsum)
        ya = _dense_attention(qa, ka_c, vat_c, ka, vat, n_maps=8, dq=A_QK_DIM, lamp=lamp, subw=subw,
                              diff_lambda_init=lambda_init)
        yw = _window_attention(qc, kc_c, vct_c, kc, vct, sink[l])
        yg = _dense_attention(qd, kd_c, vdt_c, kd, vdt, n_maps=4, dq=HEAD_DIM)
        x = _out_projection(x, mods, lat_row, ya, g3, yw, yg, conv_w[l], wout, ln_g[l, 1], ln_b[l, 1], alpha)

        x = _ffn_step(x, mods, lat_row, 6, wgu2, wdn2, ln_g[l, 2], ln_b[l, 2], alpha)

        if need_ctx:
            ya_c = _dense_attention(qa_c, ka_c, vat_c, None, None, n_maps=8, dq=A_QK_DIM, lamp=lamp, subw=subw,
                                    diff_lambda_init=lambda_init)
            yw_c = _dense_attention(qc_c, kc_c, vct_c, None, None, n_maps=4, dq=HEAD_DIM, sink=sink[l])
            yg_c = _dense_attention(qd_c, kd_c, vdt_c, None, None, n_maps=4, dq=HEAD_DIM)
            xc = _out_projection(xc, mods, ctx_row, ya_c, g3_c, yw_c, yg_c, conv_w[l], wout, ln_g[l, 1], ln_b[l, 1],
                                 alpha)
            xc = _ffn_step(xc, mods, ctx_row, 6, wgu2, wdn2, ln_g[l, 2], ln_b[l, 2], alpha)
    return x
```

```python
import functools
import math

import jax
import jax.numpy as jnp
from jax import lax
from jax.experimental import pallas as pl
from jax.experimental.pallas import tpu as pltpu

F32 = jnp.float32
BF16 = jnp.bfloat16

GRID_W = 64
HEAD_DIM = 64
WINDOW = 128
ROPE_BASE = 10000.0
EPS = 1e-6
NEG = -1e30
A_QK_DIM = 32
N_MOD = 9
MOD_ROWS = 8
VMEM_LIMIT = 56 * 1024 * 1024

OFF_AQ, OFF_AK, OFF_AV = 0, 256, 512
OFF_B = 768
OFF_CQ, OFF_CK, OFF_CV = 1536, 1792, 1920
OFF_DQ, OFF_DK, OFF_DV = 2048, 2304, 2432
IN_W = 2560

TOK_TILE = 512
Q_TILE = 256
FF_CHUNK = 256
SCORE_ROWS = 128
PV_ROWS = 256
ATTN_CHUNK = 1024
STAGES = 4
LOOKAHEAD = 2
ROUNDS_PER_ITER = 4
K_TILE = 128
ONES_ROWS = 16
V_ROWS = HEAD_DIM + ONES_ROWS
LOG2E = math.log2(math.e)


def _params(*sem):
    return pltpu.CompilerParams(dimension_semantics=sem, vmem_limit_bytes=VMEM_LIMIT)


def _layer_norm(r, g, b):
    mu = jnp.mean(r, axis=-1, keepdims=True)
    d = r - mu
    var = jnp.mean(d * d, axis=-1, keepdims=True)
    return d * lax.rsqrt(var + EPS) * g + b


def _mod_kernel(c_ref, w_ref, b_ref, o_ref):
    c = c_ref[...]
    a = (c * jax.nn.sigmoid(c)).astype(BF16)
    o_ref[0] = jnp.dot(a, w_ref[0].astype(BF16), preferred_element_type=F32) + b_ref[0]


def _mod_vectors(cvec, w_mod, b_mod):
    depth, d, n = w_mod.shape
    tn = n // 8
    return pl.pallas_call(
        _mod_kernel,
        grid=(depth, n // tn),
        in_specs=[
            pl.BlockSpec((MOD_ROWS, d), lambda l, j: (0, 0)),
            pl.BlockSpec((1, d, tn), lambda l, j: (l, 0, j)),
            pl.BlockSpec((1, 1, tn), lambda l, j: (l, 0, j)),
        ],
        out_specs=pl.BlockSpec((1, MOD_ROWS, tn), lambda l, j: (l, 0, j)),
        out_shape=jax.ShapeDtypeStruct((depth, MOD_ROWS, n), F32),
        compiler_params=_params("parallel", "parallel"),
        name="mod_vectors",
    )(cvec, w_mod, b_mod.reshape(depth, 1, n))


def _ffn_kernel(x_ref, mod_ref, wgu_ref, wdn_ref, g_ref, b_ref, o_ref, acc_ref, *, mod_base, alpha):
    x = x_ref[0]
    shift = mod_ref[0, mod_base:mod_base + 1, :]
    scale = mod_ref[0, mod_base + 1:mod_base + 2, :]
    gate = mod_ref[0, mod_base + 2:mod_base + 3, :]
    h = (x * (1.0 + scale) + shift).astype(BF16)
    d_ff = wdn_ref.shape[0]
    for c in range(d_ff // FF_CHUNK):
        lo = c * FF_CHUNK
        g = jnp.dot(h, wgu_ref[:, lo:lo + FF_CHUNK], preferred_element_type=F32)
        u = jnp.dot(h, wgu_ref[:, d_ff + lo:d_ff + lo + FF_CHUNK], preferred_element_type=F32)
        a = (g * jax.nn.sigmoid(g) * u).astype(BF16)
        y = jnp.dot(a, wdn_ref[lo:lo + FF_CHUNK, :], preferred_element_type=F32)
        if c == 0:
            acc_ref[...] = y
        else:
            acc_ref[...] += y
    r = alpha * x + (0.5 * gate) * acc_ref[...]
    o_ref[0] = _layer_norm(r, g_ref[...], b_ref[...])


def _ffn_step(x, mods, mod_row, mod_base, w_gu, w_dn, ln_g, ln_b, alpha):
    bsz, t, d = x.shape
    d_ff = w_dn.shape[0]
    tm = min(TOK_TILE, t)
    const = lambda b, i: (0, 0)
    return pl.pallas_call(
        functools.partial(_ffn_kernel, mod_base=mod_base, alpha=alpha),
        grid=(bsz, t // tm),
        in_specs=[
            pl.BlockSpec((1, tm, d), lambda b, i: (b, i, 0)),
            pl.BlockSpec((1, N_MOD, d), lambda b, i: (mod_row(b), 0, 0)),
            pl.BlockSpec((d, 2 * d_ff), const, pipeline_mode=pl.Buffered(1)),
            pl.BlockSpec((d_ff, d), const, pipeline_mode=pl.Buffered(1)),
            pl.BlockSpec((1, d), const),
            pl.BlockSpec((1, d), const),
        ],
        out_specs=pl.BlockSpec((1, tm, d), lambda b, i: (b, i, 0)),
        out_shape=jax.ShapeDtypeStruct(x.shape, F32),
        scratch_shapes=[pltpu.VMEM((tm, d), F32)],
        compiler_params=_params("parallel", "parallel"),
        name="ffn_step",
    )(x, mods, w_gu, w_dn, ln_g.reshape(1, d), ln_b.reshape(1, d))


def _rope(x, cos, sin_next, sin_prev, quarter):
    width = x.shape[1]
    reps = width // cos.shape[1]
    if reps > 1:
        cos = jnp.concatenate([cos] * reps, axis=1)
        sin_next = jnp.concatenate([sin_next] * reps, axis=1)
        sin_prev = jnp.concatenate([sin_prev] * reps, axis=1)
    x_next = pltpu.roll(x, width - quarter, axis=1)
    x_prev = pltpu.roll(x, quarter, axis=1)
    return x * cos + x_next * sin_next + x_prev * sin_prev


def _head_rms(x, gsum_ref, w):
    x2 = x * x
    hi = x2.astype(BF16)
    lo = (x2 - hi.astype(F32)).astype(BF16)
    gs = gsum_ref[:x.shape[1], :x.shape[1]]
    ss = jnp.dot(hi, gs, preferred_element_type=F32) + jnp.dot(lo, gs, preferred_element_type=F32)
    return x * lax.rsqrt(ss * (1.0 / HEAD_DIM) + EPS) * w


def _store_k(k_ref, k):
    for kt in range(k.shape[1] // K_TILE):
        k_ref[0, kt] = k[:, kt * K_TILE:(kt + 1) * K_TILE].astype(BF16)


def _store_vt(vt_ref, v):
    vt = v.T
    tm = v.shape[0]
    for hd in range(v.shape[1] // HEAD_DIM):
        vt_ref[0, hd, 0:HEAD_DIM, :] = vt[hd * HEAD_DIM:(hd + 1) * HEAD_DIM, :].astype(BF16)
        vt_ref[0, hd, HEAD_DIM:V_ROWS, :] = jnp.ones((ONES_ROWS, tm), BF16)


def _inproj_kernel(x_ref, mod_ref, w_ref, ca_ref, san_ref, sap_ref, ch_ref, shn_ref, shp_ref,
                   qnw_ref, knw_ref, gsum_ref,
                   qa_ref, ka_ref, vat_ref, g3_ref, qc_ref, kc_ref, vct_ref, qd_ref, kd_ref, vdt_ref):
    x = x_ref[0]
    shift = mod_ref[0, 3:4, :]
    scale = mod_ref[0, 4:5, :]
    h = (x * (1.0 + scale) + shift).astype(BF16)

    def seg(lo, hi):
        return jnp.dot(h, w_ref[:, lo:hi], preferred_element_type=F32)

    rope_a = functools.partial(_rope, cos=ca_ref[...], sin_next=san_ref[...], sin_prev=sap_ref[...],
                               quarter=A_QK_DIM // 4)
    rope_h = functools.partial(_rope, cos=ch_ref[...], sin_next=shn_ref[...], sin_prev=shp_ref[...],
                               quarter=HEAD_DIM // 4)
    qa_ref[0] = (rope_a(seg(OFF_AQ, OFF_AK)) * (LOG2E * A_QK_DIM ** -0.5)).astype(BF16)
    _store_k(ka_ref, rope_a(seg(OFF_AK, OFF_AV)))
    _store_vt(vat_ref, seg(OFF_AV, OFF_B))
    g3_ref[0] = seg(OFF_B, OFF_CQ)
    qc_ref[0] = (rope_h(seg(OFF_CQ, OFF_CK)) * (LOG2E * HEAD_DIM ** -0.5)).astype(BF16)
    _store_k(kc_ref, rope_h(seg(OFF_CK, OFF_CV)))
    _store_vt(vct_ref, seg(OFF_CV, OFF_DQ))
    qd = _head_rms(seg(OFF_DQ, OFF_DK), gsum_ref, qnw_ref[...])
    qd_ref[0] = (rope_h(qd) * (LOG2E * HEAD_DIM ** -0.5)).astype(BF16)
    kd = _head_rms(seg(OFF_DK, OFF_DV), gsum_ref, knw_ref[...])
    _store_k(kd_ref, rope_h(kd))
    _store_vt(vdt_ref, seg(OFF_DV, IN_W))


def _in_projection(x, mods, mod_row, w_in, tabs_a, tabs_h, qnw, knw, gsum):
    bsz, t, d = x.shape
    tm = min(TOK_TILE, t)
    nc = t // tm
    const = lambda b, i: (0, 0)
    tab = pl.BlockSpec((tm, 128), lambda b, i: (i, 0))
    q_spec = pl.BlockSpec((1, tm, 256), lambda b, i: (b, i, 0))

    def k_spec(w):
        return pl.BlockSpec((1, w // K_TILE, tm, K_TILE), lambda b, i: (b, 0, i, 0))

    def vt_spec(w):
        return pl.BlockSpec((1, w // HEAD_DIM, V_ROWS, tm), lambda b, i: (b, 0, 0, i))

    def q_shape():
        return jax.ShapeDtypeStruct((bsz, t, 256), BF16)

    def k_shape(w):
        return jax.ShapeDtypeStruct((bsz, w // K_TILE, t, K_TILE), BF16)

    def vt_shape(w):
        return jax.ShapeDtypeStruct((bsz, w // HEAD_DIM, V_ROWS, t), BF16)

    return pl.pallas_call(
        _inproj_kernel,
        grid=(bsz, nc),
        in_specs=[
            pl.BlockSpec((1, tm, d), lambda b, i: (b, i, 0)),
            pl.BlockSpec((1, N_MOD, d), lambda b, i: (mod_row(b), 0, 0)),
            pl.BlockSpec((d, IN_W), const, pipeline_mode=pl.Buffered(1)),
            tab, tab, tab, tab, tab, tab,
            pl.BlockSpec((1, 256), const),
            pl.BlockSpec((1, 128), const),
            pl.BlockSpec((256, 256), const),
        ],
        out_specs=[q_spec, k_spec(256), vt_spec(256),
                   pl.BlockSpec((1, tm, 768), lambda b, i: (b, i, 0)),
                   q_spec, k_spec(128), vt_spec(128),
                   q_spec, k_spec(128), vt_spec(128)],
        out_shape=[q_shape(), k_shape(256), vt_shape(256),
                   jax.ShapeDtypeStruct((bsz, t, 768), F32),
                   q_shape(), k_shape(128), vt_shape(128),
                   q_shape(), k_shape(128), vt_shape(128)],
        compiler_params=_params("parallel", "parallel"),
        name="in_projection",
    )(x, mods, w_in, *tabs_a, *tabs_h, qnw, knw, gsum)


def _whole_per_batch(arr):
    zeros = (0,) * (arr.ndim - 1)
    return pl.BlockSpec((1,) + arr.shape[1:], lambda b, i: (b,) + zeros, pipeline_mode=pl.Buffered(1))


def _state_scratch(n_maps, tq):
    return [pltpu.VMEM((n_maps, K_TILE, tq), BF16),
            pltpu.VMEM((n_maps, 1, tq), F32),
            pltpu.VMEM((n_maps, V_ROWS, tq), F32)]


def _key_offset(j, dq, grouped):
    return (j // 2) * dq if grouped else j * dq


def _fill_qpad(q_ref, qpad_ref, n_maps, dq, grouped):
    tq = q_ref.shape[1]
    qt = q_ref[0].astype(F32).T
    for j in range(n_maps):
        row = _key_offset(j, dq, grouped) % K_TILE
        parts = []
        if row:
            parts.append(jnp.zeros((row, tq), F32))
        parts.append(qt[j * dq:(j + 1) * dq, :])
        if K_TILE - row - dq:
            parts.append(jnp.zeros((K_TILE - row - dq, tq), F32))
        qpad_ref[j] = jnp.concatenate(parts, axis=0).astype(BF16)


def _softmax_step(k_c, vt_c, qpad_ref, m_ref, acc_ref, n_maps, dq, grouped, mask=None):
    for j in range(n_maps):
        k_tile = k_c[_key_offset(j, dq, grouped) // K_TILE]
        s = jnp.dot(k_tile, qpad_ref[j], preferred_element_type=F32)
        if mask is not None:
            s = jnp.where(mask, s, NEG)
        m_old = m_ref[j]
        m_new = jnp.maximum(m_old, jnp.max(s, axis=0, keepdims=True))
        p = jnp.exp2(s - m_new)
        alpha = jnp.exp2(m_old - m_new)
        pv = jnp.dot(vt_c[j // 2], p.astype(BF16), preferred_element_type=F32)
        acc_ref[j] = alpha * acc_ref[j] + pv
        m_ref[j] = m_new


def _score_unit(keys, j, nk, qpad_ref, s_ref, bm_ref):
    tq = s_ref.shape[1]
    w = qpad_ref[j]
    bm = None
    for r in range(nk // SCORE_ROWS):
        rows = slice(r * SCORE_ROWS, (r + 1) * SCORE_ROWS)
        s = jnp.dot(keys(rows), w, preferred_element_type=F32)
        s_ref[rows, :] = s
        part = jnp.max(s.reshape(SCORE_ROWS // 8, 8, tq), axis=0)
        bm = part if bm is None else jnp.maximum(bm, part)
    bm_ref[...] = bm


def _softmax_unit(values, j, nk, s_ref, bm_ref, m_ref, acc_ref, row0):
    m_old = m_ref[j]
    m_new = jnp.maximum(m_old, jnp.max(bm_ref[...], axis=0, keepdims=True))
    alpha = jnp.exp2(m_old - m_new)
    pv = None
    for r in range(nk // PV_ROWS):
        rows = pl.ds(pl.multiple_of(row0 + r * PV_ROWS, PV_ROWS), PV_ROWS)
        p = jnp.exp2(s_ref[rows, :] - m_new)
        d = jnp.dot(values(slice(r * PV_ROWS, (r + 1) * PV_ROWS)), p.astype(BF16), preferred_element_type=F32)
        pv = d if pv is None else pv + d
    acc_ref[j] = alpha * acc_ref[j] + pv
    m_ref[j] = m_new


def _init_state(m_ref, acc_ref, n_maps, sink_ref):
    tq = m_ref.shape[-1]
    for j in range(n_maps):
        if sink_ref is None:
            m_ref[j] = jnp.full((1, tq), NEG, F32)
            acc_ref[j] = jnp.zeros((V_ROWS, tq), F32)
        else:
            m_ref[j] = jnp.full((1, tq), sink_ref[j] * LOG2E, F32)
            acc_ref[j] = jnp.concatenate([jnp.zeros((HEAD_DIM, tq), F32), jnp.ones((ONES_ROWS, tq), F32)], axis=0)


def _normalised(acc_ref, j):
    acc = acc_ref[j]
    return acc[0:HEAD_DIM, :] / acc[HEAD_DIM:HEAD_DIM + 1, :]


def _finish_plain(o_ref, acc_ref, n_maps):
    ot = jnp.concatenate([_normalised(acc_ref, j) for j in range(n_maps)], axis=0)
    o_ref[0] = ot.T.astype(BF16)


def _finish_diff(o_ref, acc_ref, lamp_ref, subw_ref, n_maps, lambda_init):
    lp = lamp_ref[...]
    lam = (jnp.exp(jnp.sum(lp[0:1] * lp[1:2], axis=1, keepdims=True))
           - jnp.exp(jnp.sum(lp[2:3] * lp[3:4], axis=1, keepdims=True)) + lambda_init)
    outs = []
    for hd in range(n_maps // 2):
        o = _normalised(acc_ref, 2 * hd) - lam * _normalised(acc_ref, 2 * hd + 1)
        ms = jnp.mean(o * o, axis=0, keepdims=True)
        outs.append(o * lax.rsqrt(ms + EPS) * subw_ref[...] * (1.0 - lambda_init))
    o_ref[0] = jnp.concatenate(outs, axis=0).T.astype(BF16)


def _dense_attn_kernel(*refs, n_maps, dq, grouped, has_latent, has_sink, diff_lambda_init):
    refs = list(refs)
    q_ref, kc_ref, vct_ref, zero_ref = refs[:4]
    pos = 4
    k_ref = vt_ref = sink_ref = lamp_ref = subw_ref = None
    if has_latent:
        k_ref, vt_ref = refs[pos:pos + 2]
        pos += 2
    if has_sink:
        sink_ref = refs[pos]
        pos += 1
    if diff_lambda_init is not None:
        lamp_ref, subw_ref = refs[pos:pos + 2]
        pos += 2
    o_ref, qpad_ref, m_ref, acc_ref = refs[pos:pos + 4]
    s_refs = refs[pos + 4:pos + 4 + STAGES]
    bm_refs = refs[pos + 4 + STAGES:pos + 4 + 2 * STAGES]

    _fill_qpad(q_ref, qpad_ref, n_maps, dq, grouped)
    _init_state(m_ref, acc_ref, n_maps, sink_ref)
    n_ctx = kc_ref.shape[2]
    tk = k_ref.shape[3] if has_latent else 0
    n_lat = n_maps * k_ref.shape[2] if has_latent else 0
    row0 = zero_ref[0]

    def score(v, slot):
        if isinstance(v, int) and v < n_maps:
            kt = _key_offset(v, dq, grouped) // K_TILE
            _score_unit(lambda rows: kc_ref[0, kt, rows, :], v, n_ctx, qpad_ref, s_refs[slot], bm_refs[slot])
        else:
            u = v - n_maps
            j, c = u % n_maps, u // n_maps
            kt = _key_offset(j, dq, grouped) // K_TILE
            _score_unit(lambda rows: k_ref[0, kt, c, rows, :], j, tk, qpad_ref, s_refs[slot], bm_refs[slot])

    def softmax(v, slot):
        if isinstance(v, int) and v < n_maps:
            _softmax_unit(lambda cols: vct_ref[0, v // 2, :, cols], v, n_ctx, s_refs[slot], bm_refs[slot],
                          m_ref, acc_ref, row0)
        else:
            u = v - n_maps
            j, c = u % n_maps, u // n_maps
            _softmax_unit(lambda cols: vt_ref[0, c, j // 2, :, cols], j, tk, s_refs[slot], bm_refs[slot],
                          m_ref, acc_ref, row0)

    n_units = n_maps + n_lat
    for v in range(LOOKAHEAD):
        score(v, v % STAGES)
    for v in range(n_maps):
        if v + LOOKAHEAD < n_units:
            score(v + LOOKAHEAD, (v + LOOKAHEAD) % STAGES)
        softmax(v, v % STAGES)

    if has_latent:
        per_iter = STAGES * math.gcd(n_lat // STAGES, ROUNDS_PER_ITER)

        def body(i, carry):
            v0 = n_maps + per_iter * i
            for k in range(per_iter):
                score(jnp.minimum(v0 + k + LOOKAHEAD, n_units - 1), (k + LOOKAHEAD) % STAGES)
                softmax(v0 + k, k % STAGES)
            return carry
        lax.fori_loop(0, n_lat // per_iter, body, 0)
    if diff_lambda_init is None:
        _finish_plain(o_ref, acc_ref, n_maps)
    else:
        _finish_diff(o_ref, acc_ref, lamp_ref, subw_ref, n_maps, diff_lambda_init)


def _dense_attention(q, k_ctx, vt_ctx, k_lat, vt_lat, *, n_maps, dq, sink=None, lamp=None, subw=None,
                     diff_lambda_init=None):
    bsz, t, qw = q.shape
    key_tiles, n_ctx = k_ctx.shape[1:3]
    heads = vt_ctx.shape[1]
    grouped = key_tiles * K_TILE != n_maps * dq
    tq = min(Q_TILE, t)
    has_latent = k_lat is not None
    whole = _whole_per_batch
    assert n_maps % STAGES == 0 and n_ctx % PV_ROWS == 0

    in_specs = [pl.BlockSpec((1, tq, qw), lambda b, i: (b, i, 0)), whole(k_ctx), whole(vt_ctx),
                pl.BlockSpec(memory_space=pltpu.SMEM)]
    args = [q, k_ctx, vt_ctx, jnp.zeros((1,), jnp.int32)]
    stage_rows = n_ctx
    if has_latent:
        n_keys = k_lat.shape[2]
        tk = math.gcd(ATTN_CHUNK, n_keys)
        n_chunks = n_keys // tk
        assert tk % PV_ROWS == 0 and tk % SCORE_ROWS == 0
        k_lat = k_lat.reshape(bsz, key_tiles, n_chunks, tk, K_TILE)
        vt_lat = vt_lat.reshape(bsz, heads, V_ROWS, n_chunks, tk).transpose(0, 3, 1, 2, 4)
        in_specs += [whole(k_lat), whole(vt_lat)]
        args += [k_lat, vt_lat]
        stage_rows = max(n_ctx, tk)
    if sink is not None:
        in_specs.append(pl.BlockSpec(memory_space=pltpu.SMEM))
        args.append(sink)
    if diff_lambda_init is not None:
        in_specs += [pl.BlockSpec(lamp.shape, lambda b, i: (0, 0)), pl.BlockSpec(subw.shape, lambda b, i: (0, 0))]
        args += [lamp, subw]
    scratch = (_state_scratch(n_maps, tq) + [pltpu.VMEM((stage_rows, tq), F32)] * STAGES
               + [pltpu.VMEM((8, tq), F32)] * STAGES)
    return pl.pallas_call(
        functools.partial(_dense_attn_kernel, n_maps=n_maps, dq=dq, grouped=grouped, has_latent=has_latent,
                          has_sink=sink is not None, diff_lambda_init=diff_lambda_init),
        grid=(bsz, t // tq),
        in_specs=in_specs,
        out_specs=pl.BlockSpec((1, tq, 256), lambda b, i: (b, i, 0)),
        out_shape=jax.ShapeDtypeStruct((bsz, t, 256), BF16),
        scratch_shapes=scratch,
        compiler_params=_params("parallel", "parallel"),
        name="dense_attention",
    )(*args)


def _window_attn_kernel(q_ref, kp_ref, kc_ref, kn_ref, vtp_ref, vtc_ref, vtn_ref, kctx_ref, vtctx_ref, sink_ref,
                        o_ref, qpad_ref, m_ref, acc_ref, *, n_maps, seq):
    tq = q_ref.shape[1]
    q0 = pl.program_id(1) * tq
    step = functools.partial(_softmax_step, qpad_ref=qpad_ref, m_ref=m_ref, acc_ref=acc_ref, n_maps=n_maps,
                             dq=HEAD_DIM, grouped=True)
    _fill_qpad(q_ref, qpad_ref, n_maps, HEAD_DIM, True)
    _init_state(m_ref, acc_ref, n_maps, sink_ref)
    step(kctx_ref.at[0], vtctx_ref.at[0])
    k_win = jnp.concatenate([kp_ref[0], kc_ref[0], kn_ref[0]], axis=0)[None]
    vt_win = jnp.concatenate([vtp_ref[0], vtc_ref[0], vtn_ref[0]], axis=2)
    nk = k_win.shape[1]
    kpos = q0 - WINDOW + lax.broadcasted_iota(jnp.int32, (nk, tq), 0)
    qpos = q0 + lax.broadcasted_iota(jnp.int32, (nk, tq), 1)
    valid = (jnp.abs(kpos - qpos) <= WINDOW) & (kpos >= 0) & (kpos < seq)
    step(k_win, vt_win, mask=valid)
    _finish_plain(o_ref, acc_ref, n_maps)


def _window_attention(q, k_ctx, vt_ctx, k_lat, vt_lat, sink):
    bsz, t, qw = q.shape
    key_tiles, _, kw = k_lat.shape[1:]
    assert key_tiles == 1
    tq = Q_TILE
    n_maps = 4
    k_flat = k_lat.reshape(bsz, t, kw)
    wb = tq // WINDOW
    nwb = t // WINDOW
    heads = vt_lat.shape[1]
    whole = _whole_per_batch

    def vt_blk(width, col):
        return pl.BlockSpec((1, heads, V_ROWS, width), lambda b, i: (b, 0, 0, col(i)))

    def prev_blk(i):
        return jnp.maximum(i * wb - 1, 0)

    def next_blk(i):
        return jnp.minimum(i * wb + wb, nwb - 1)

    in_specs = [
        pl.BlockSpec((1, tq, qw), lambda b, i: (b, i, 0)),
        pl.BlockSpec((1, WINDOW, kw), lambda b, i: (b, prev_blk(i), 0)),
        pl.BlockSpec((1, tq, kw), lambda b, i: (b, i, 0)),
        pl.BlockSpec((1, WINDOW, kw), lambda b, i: (b, next_blk(i), 0)),
        vt_blk(WINDOW, prev_blk),
        vt_blk(tq, lambda i: i),
        vt_blk(WINDOW, next_blk),
        whole(k_ctx), whole(vt_ctx),
        pl.BlockSpec(memory_space=pltpu.SMEM),
    ]
    return pl.pallas_call(
        functools.partial(_window_attn_kernel, n_maps=n_maps, seq=t),
        grid=(bsz, t // tq),
        in_specs=in_specs,
        out_specs=pl.BlockSpec((1, tq, 256), lambda b, i: (b, i, 0)),
        out_shape=jax.ShapeDtypeStruct((bsz, t, 256), BF16),
        scratch_shapes=_state_scratch(n_maps, tq),
        compiler_params=_params("parallel", "parallel"),
        name="window_attention",
    )(q, k_flat, k_flat, k_flat, vt_lat, vt_lat, vt_lat, k_ctx, vt_ctx, sink)


def _outproj_kernel(x_ref, mod_ref, ya_ref, g3_ref, g3p_ref, g3n_ref, yw_ref, yg_ref, cw_ref, w_ref, g_ref, b_ref,
                    o_ref, *, alpha):
    i = pl.program_id(1)
    last = pl.num_programs(1) - 1
    tm = x_ref.shape[1]
    gb = g3_ref[0, :, 0:256]
    hid = g3_ref[0, :, 256:512] * g3_ref[0, :, 512:768]
    halo = g3p_ref.shape[1]
    h_prev = g3p_ref[0, halo - 1:halo, 256:512] * g3p_ref[0, halo - 1:halo, 512:768]
    h_next = g3n_ref[0, 0:1, 256:512] * g3n_ref[0, 0:1, 512:768]
    h_prev = jnp.where(i == 0, 0.0, h_prev)
    h_next = jnp.where(i == last, 0.0, h_next)
    row = lax.broadcasted_iota(jnp.int32, hid.shape, 0)
    below = jnp.where(row == 0, h_prev, pltpu.roll(hid, 1, axis=0))
    above = jnp.where(row == tm - 1, h_next, pltpu.roll(hid, tm - 1, axis=0))
    yb = gb * (cw_ref[0:1, :] * below + cw_ref[1:2, :] * hid + cw_ref[2:3, :] * above)
    y = jnp.dot(ya_ref[0], w_ref[0:256, :], preferred_element_type=F32)
    y += jnp.dot(yb.astype(BF16), w_ref[256:512, :], preferred_element_type=F32)
    y += jnp.dot(yw_ref[0], w_ref[512:768, :], preferred_element_type=F32)
    y += jnp.dot(yg_ref[0], w_ref[768:1024, :], preferred_element_type=F32)
    gate = mod_ref[0, 5:6, :]
    o_ref[0] = _layer_norm(alpha * x_ref[0] + gate * y, g_ref[...], b_ref[...])


def _out_projection(x, mods, mod_row, ya, g3, yw, yg, conv_w, w_out, ln_g, ln_b, alpha):
    bsz, t, d = x.shape
    tm = min(TOK_TILE, t)
    halo = 8
    hb = tm // halo
    n_halo = t // halo
    const = lambda b, i: (0, 0)
    y_spec = pl.BlockSpec((1, tm, 256), lambda b, i: (b, i, 0))
    return pl.pallas_call(
        functools.partial(_outproj_kernel, alpha=alpha),
        grid=(bsz, t // tm),
        in_specs=[
            pl.BlockSpec((1, tm, d), lambda b, i: (b, i, 0)),
            pl.BlockSpec((1, N_MOD, d), lambda b, i: (mod_row(b), 0, 0)),
            y_spec,
            pl.BlockSpec((1, tm, 768), lambda b, i: (b, i, 0)),
            pl.BlockSpec((1, halo, 768), lambda b, i: (b, jnp.maximum(i * hb - 1, 0), 0)),
            pl.BlockSpec((1, halo, 768), lambda b, i: (b, jnp.minimum(i * hb + hb, n_halo - 1), 0)),
            y_spec, y_spec,
            pl.BlockSpec(conv_w.shape, const),
            pl.BlockSpec(w_out.shape, const, pipeline_mode=pl.Buffered(1)),
            pl.BlockSpec((1, d), const),
            pl.BlockSpec((1, d), const),
        ],
        out_specs=pl.BlockSpec((1, tm, d), lambda b, i: (b, i, 0)),
        out_shape=jax.ShapeDtypeStruct(x.shape, F32),
        compiler_params=_params("parallel", "parallel"),
        name="out_projection",
    )(x, mods, ya, g3, g3, g3, yw, yg, conv_w, w_out, ln_g.reshape(1, d), ln_b.reshape(1, d))


def _rope_tables(rows, dim):
    n_freq = dim // 4
    inv_freq = ROPE_BASE ** (-jnp.arange(n_freq, dtype=F32) / n_freq)
    ang_r = jnp.arange(rows, dtype=F32)[:, None] * inv_freq
    ang_c = jnp.arange(GRID_W, dtype=F32)[:, None] * inv_freq
    ang_r = jnp.broadcast_to(ang_r[:, None, :], (rows, GRID_W, n_freq))
    ang_c = jnp.broadcast_to(ang_c[None, :, :], (rows, GRID_W, n_freq))
    ang = jnp.concatenate([ang_r, ang_r, ang_c, ang_c], axis=-1).reshape(rows * GRID_W, dim)
    cos, sin = jnp.cos(ang), jnp.sin(ang)
    reps = 128 // dim
    cos = jnp.tile(cos, (1, reps))
    sin = jnp.tile(sin, (1, reps))
    first_half = (jnp.arange(128) % (dim // 2)) < (dim // 4)
    sin_next = jnp.where(first_half, -sin, 0.0)
    sin_prev = jnp.where(first_half, 0.0, sin)
    return cos, sin_next, sin_prev


def _identity_tables(t):
    return jnp.ones((t, 128), F32), jnp.zeros((t, 128), F32), jnp.zeros((t, 128), F32)


def kernel(x, c, ctx, c_ctx, w_mod, b_mod, ln_g, ln_b, w_gu1, w_dn1, w_in, w_out, conv_w, lam_q1, lam_k1, lam_q2,
           lam_k2, subln_w, sink, qn_w, kn_w, w_gu2, w_dn2):
    bsz, seq, d = x.shape
    ctx_len = ctx.shape[1]
    depth = w_mod.shape[0]
    alpha = (2.0 * depth) ** 0.25
    assert seq % TOK_TILE == 0 and ctx_len % Q_TILE == 0 and bsz < MOD_ROWS

    tabs_a = _rope_tables(seq // GRID_W, A_QK_DIM)
    tabs_h = _rope_tables(seq // GRID_W, HEAD_DIM)
    tabs_id = _identity_tables(ctx_len)
    gsum = (jnp.arange(256)[:, None] // HEAD_DIM == jnp.arange(256)[None, :] // HEAD_DIM).astype(BF16)

    cvec = jnp.zeros((MOD_ROWS, d), F32).at[:bsz].set(c).at[bsz].set(c_ctx)
    mods_all = _mod_vectors(cvec, w_mod, b_mod).reshape(depth, MOD_ROWS, N_MOD, d)
    lat_row = lambda b: b
    ctx_row = lambda b: bsz

    xc = ctx
    for l in range(depth):
        need_ctx = l < depth - 1
        lambda_init = 0.8 - 0.6 * math.exp(-0.3 * l)
        mods = mods_all[l]
        wgu1, wdn1 = w_gu1[l].astype(BF16), w_dn1[l].astype(BF16)
        wgu2, wdn2 = w_gu2[l].astype(BF16), w_dn2[l].astype(BF16)
        win, wout = w_in[l].astype(BF16), w_out[l].astype(BF16)
        qnw = jnp.tile(qn_w[l], 256 // HEAD_DIM).reshape(1, 256)
        knw = jnp.tile(kn_w[l], 128 // HEAD_DIM).reshape(1, 128)
        lamp = jnp.stack([lam_q1[l], lam_k1[l], lam_q2[l], lam_k2[l]])
        subw = subln_w[l].reshape(HEAD_DIM, 1)

        x = _ffn_step(x, mods, lat_row, 0, wgu1, wdn1, ln_g[l, 0], ln_b[l, 0], alpha)
        xc = _ffn_step(xc, mods, ctx_row, 0, wgu1, wdn1, ln_g[l, 0], ln_b[l, 0], alpha)

        qa, ka, vat, g3, qc, kc, vct, qd, kd, vdt = _in_projection(x, mods, lat_row, win, tabs_a, tabs_h, qnw, knw, gsum)
        (qa_c, ka_c, vat_c, g3_c, qc_c, kc_c, vct_c, qd_c, kd_c, vdt_c) = _in_projection(
            xc, mods, ctx_row, win, tabs_id, tabs_id, qnw, knw, gsum)
        ya = _dense_attention(qa, ka_c, vat_c, ka, vat, n_maps=8, dq=A_QK_DIM, lamp=lamp, subw=subw,
                              diff_lambda_init=lambda_init)
        yw = _window_attention(qc, kc_c, vct_c, kc, vct, sink[l])
        yg = _dense_attention(qd, kd_c, vdt_c, kd, vdt, n_maps=4, dq=HEAD_DIM)
        x = _out_projection(x, mods, lat_row, ya, g3, yw, yg, conv_w[l], wout, ln_g[l, 1], ln_b[l, 1], alpha)

        x = _ffn_step(x, mods, lat_row, 6, wgu2, wdn2, ln_g[l, 2], ln_b[l, 2], alpha)

        if need_ctx:
            ya_c = _dense_attention(qa_c, ka_c, vat_c, None, None, n_maps=8, dq=A_QK_DIM, lamp=lamp, subw=subw,
                                    diff_lambda_init=lambda_init)
            yw_c = _dense_attention(qc_c, kc_c, vct_c, None, None, n_maps=4, dq=HEAD_DIM, sink=sink[l])
            yg_c = _dense_attention(qd_c, kd_c, vdt_c, None, None, n_maps=4, dq=HEAD_DIM)
            xc = _out_projection(xc, mods, ctx_row, ya_c, g3_c, yw_c, yg_c, conv_w[l], wout, ln_g[l, 1], ln_b[l, 1],
                                 alpha)
            xc = _ffn_step(xc, mods, ctx_row, 6, wgu2, wdn2, ln_g[l, 2], ln_b[l, 2], alpha)
    return x
```

```python
import functools
import math

import jax
import jax.numpy as jnp
from jax import lax
from jax.experimental import pallas as pl
from jax.experimental.pallas import tpu as pltpu

F32 = jnp.float32
BF16 = jnp.bfloat16

GRID_W = 64
HEAD_DIM = 64
WINDOW = 128
ROPE_BASE = 10000.0
EPS = 1e-6
NEG = -1e30
A_QK_DIM = 32
N_MOD = 9
MOD_ROWS = 8
VMEM_LIMIT = 56 * 1024 * 1024

OFF_AQ, OFF_AK, OFF_AV = 0, 256, 512
OFF_B = 768
OFF_CQ, OFF_CK, OFF_CV = 1536, 1792, 1920
OFF_DQ, OFF_DK, OFF_DV = 2048, 2304, 2432
IN_W = 2560

TOK_TILE = 512
Q_TILE = 256
FF_CHUNK = 256
SCORE_ROWS = 128
PV_ROWS = 256
ATTN_CHUNK = 1024
STAGES = 4
LOOKAHEAD = 2
ROUNDS_PER_ITER = 4
K_TILE = 128
ONES_ROWS = 16
V_ROWS = HEAD_DIM + ONES_ROWS
LOG2E = math.log2(math.e)


def _params(*sem):
    return pltpu.CompilerParams(dimension_semantics=sem, vmem_limit_bytes=VMEM_LIMIT)


def _layer_norm(r, g, b):
    mu = jnp.mean(r, axis=-1, keepdims=True)
    d = r - mu
    var = jnp.mean(d * d, axis=-1, keepdims=True)
    return d * lax.rsqrt(var + EPS) * g + b


def _mod_kernel(c_ref, w_ref, b_ref, o_ref):
    c = c_ref[...]
    a = (c * jax.nn.sigmoid(c)).astype(BF16)
    o_ref[0] = jnp.dot(a, w_ref[0].astype(BF16), preferred_element_type=F32) + b_ref[0]


def _mod_vectors(cvec, w_mod, b_mod):
    depth, d, n = w_mod.shape
    tn = n // 8
    return pl.pallas_call(
        _mod_kernel,
        grid=(depth, n // tn),
        in_specs=[
            pl.BlockSpec((MOD_ROWS, d), lambda l, j: (0, 0)),
            pl.BlockSpec((1, d, tn), lambda l, j: (l, 0, j)),
            pl.BlockSpec((1, 1, tn), lambda l, j: (l, 0, j)),
        ],
        out_specs=pl.BlockSpec((1, MOD_ROWS, tn), lambda l, j: (l, 0, j)),
        out_shape=jax.ShapeDtypeStruct((depth, MOD_ROWS, n), F32),
        compiler_params=_params("parallel", "parallel"),
        name="mod_vectors",
    )(cvec, w_mod, b_mod.reshape(depth, 1, n))


def _ffn_kernel(x_ref, mod_ref, wgu_ref, wdn_ref, g_ref, b_ref, o_ref, acc_ref, *, mod_base, alpha):
    x = x_ref[0]
    shift = mod_ref[0, mod_base:mod_base + 1, :]
    scale = mod_ref[0, mod_base + 1:mod_base + 2, :]
    gate = mod_ref[0, mod_base + 2:mod_base + 3, :]
    h = (x * (1.0 + scale) + shift).astype(BF16)
    d_ff = wdn_ref.shape[0]
    for c in range(d_ff // FF_CHUNK):
        lo = c * FF_CHUNK
        g = jnp.dot(h, wgu_ref[:, lo:lo + FF_CHUNK], preferred_element_type=F32)
        u = jnp.dot(h, wgu_ref[:, d_ff + lo:d_ff + lo + FF_CHUNK], preferred_element_type=F32)
        a = (g * jax.nn.sigmoid(g) * u).astype(BF16)
        y = jnp.dot(a, wdn_ref[lo:lo + FF_CHUNK, :], preferred_element_type=F32)
        if c == 0:
            acc_ref[...] = y
        else:
            acc_ref[...] += y
    r = alpha * x + (0.5 * gate) * acc_ref[...]
    o_ref[0] = _layer_norm(r, g_ref[...], b_ref[...])


def _ffn_step(x, mods, mod_row, mod_base, w_gu, w_dn, ln_g, ln_b, alpha):
    bsz, t, d = x.shape
    d_ff = w_dn.shape[0]
    tm = min(TOK_TILE, t)
    const = lambda b, i: (0, 0)
    return pl.pallas_call(
        functools.partial(_ffn_kernel, mod_base=mod_base, alpha=alpha),
        grid=(bsz, t // tm),
        in_specs=[
            pl.BlockSpec((1, tm, d), lambda b, i: (b, i, 0)),
            pl.BlockSpec((1, N_MOD, d), lambda b, i: (mod_row(b), 0, 0)),
            pl.BlockSpec((d, 2 * d_ff), const, pipeline_mode=pl.Buffered(1)),
            pl.BlockSpec((d_ff, d), const, pipeline_mode=pl.Buffered(1)),
            pl.BlockSpec((1, d), const),
            pl.BlockSpec((1, d), const),
        ],
        out_specs=pl.BlockSpec((1, tm, d), lambda b, i: (b, i, 0)),
        out_shape=jax.ShapeDtypeStruct(x.shape, F32),
        scratch_shapes=[pltpu.VMEM((tm, d), F32)],
        compiler_params=_params("parallel", "parallel"),
        name="ffn_step",
    )(x, mods, w_gu, w_dn, ln_g.reshape(1, d), ln_b.reshape(1, d))


def _rope(x, cos, sin_next, sin_prev, quarter):
    width = x.shape[1]
    reps = width // cos.shape[1]
    if reps > 1:
        cos = jnp.concatenate([cos] * reps, axis=1)
        sin_next = jnp.concatenate([sin_next] * reps, axis=1)
        sin_prev = jnp.concatenate([sin_prev] * reps, axis=1)
    x_next = pltpu.roll(x, width - quarter, axis=1)
    x_prev = pltpu.roll(x, quarter, axis=1)
    return x * cos + x_next * sin_next + x_prev * sin_prev


def _head_rms(x, gsum_ref, w):
    x2 = x * x
    hi = x2.astype(BF16)
    lo = (x2 - hi.astype(F32)).astype(BF16)
    gs = gsum_ref[:x.shape[1], :x.shape[1]]
    ss = jnp.dot(hi, gs, preferred_element_type=F32) + jnp.dot(lo, gs, preferred_element_type=F32)
    return x * lax.rsqrt(ss * (1.0 / HEAD_DIM) + EPS) * w


def _store_k(k_ref, k):
    for kt in range(k.shape[1] // K_TILE):
        k_ref[0, kt] = k[:, kt * K_TILE:(kt + 1) * K_TILE].astype(BF16)


def _store_vt(vt_ref, v):
    vt = v.T
    tm = v.shape[0]
    for hd in range(v.shape[1] // HEAD_DIM):
        vt_ref[0, hd, 0:HEAD_DIM, :] = vt[hd * HEAD_DIM:(hd + 1) * HEAD_DIM, :].astype(BF16)
        vt_ref[0, hd, HEAD_DIM:V_ROWS, :] = jnp.ones((ONES_ROWS, tm), BF16)


def _inproj_kernel(x_ref, mod_ref, w_ref, ca_ref, san_ref, sap_ref, ch_ref, shn_ref, shp_ref,
                   qnw_ref, knw_ref, gsum_ref,
                   qa_ref, ka_ref, vat_ref, g3_ref, qc_ref, kc_ref, vct_ref, qd_ref, kd_ref, vdt_ref):
    x = x_ref[0]
    shift = mod_ref[0, 3:4, :]
    scale = mod_ref[0, 4:5, :]
    h = (x * (1.0 + scale) + shift).astype(BF16)

    def seg(lo, hi):
        return jnp.dot(h, w_ref[:, lo:hi], preferred_element_type=F32)

    rope_a = functools.partial(_rope, cos=ca_ref[...], sin_next=san_ref[...], sin_prev=sap_ref[...],
                               quarter=A_QK_DIM // 4)
    rope_h = functools.partial(_rope, cos=ch_ref[...], sin_next=shn_ref[...], sin_prev=shp_ref[...],
                               quarter=HEAD_DIM // 4)
    qa_ref[0] = (rope_a(seg(OFF_AQ, OFF_AK)) * (LOG2E * A_QK_DIM ** -0.5)).astype(BF16)
    _store_k(ka_ref, rope_a(seg(OFF_AK, OFF_AV)))
    _store_vt(vat_ref, seg(OFF_AV, OFF_B))
    g3_ref[0] = seg(OFF_B, OFF_CQ)
    qc_ref[0] = (rope_h(seg(OFF_CQ, OFF_CK)) * (LOG2E * HEAD_DIM ** -0.5)).astype(BF16)
    kv = seg(OFF_CK, OFF_DQ)
    _store_k(kc_ref, rope_h(kv[:, :OFF_CV - OFF_CK]))
    _store_vt(vct_ref, kv[:, OFF_CV - OFF_CK:])
    qd = _head_rms(seg(OFF_DQ, OFF_DK), gsum_ref, qnw_ref[...])
    qd_ref[0] = (rope_h(qd) * (LOG2E * HEAD_DIM ** -0.5)).astype(BF16)
    kv = seg(OFF_DK, IN_W)
    kd = _head_rms(kv[:, :OFF_DV - OFF_DK], gsum_ref, knw_ref[...])
    _store_k(kd_ref, rope_h(kd))
    _store_vt(vdt_ref, kv[:, OFF_DV - OFF_DK:])


def _in_projection(x, mods, mod_row, w_in, tabs_a, tabs_h, qnw, knw, gsum):
    bsz, t, d = x.shape
    tm = min(TOK_TILE, t)
    nc = t // tm
    const = lambda b, i: (0, 0)
    tab = pl.BlockSpec((tm, 128), lambda b, i: (i, 0))
    q_spec = pl.BlockSpec((1, tm, 256), lambda b, i: (b, i, 0))

    def k_spec(w):
        return pl.BlockSpec((1, w // K_TILE, tm, K_TILE), lambda b, i: (b, 0, i, 0))

    def vt_spec(w):
        return pl.BlockSpec((1, w // HEAD_DIM, V_ROWS, tm), lambda b, i: (b, 0, 0, i))

    def q_shape():
        return jax.ShapeDtypeStruct((bsz, t, 256), BF16)

    def k_shape(w):
        return jax.ShapeDtypeStruct((bsz, w // K_TILE, t, K_TILE), BF16)

    def vt_shape(w):
        return jax.ShapeDtypeStruct((bsz, w // HEAD_DIM, V_ROWS, t), BF16)

    return pl.pallas_call(
        _inproj_kernel,
        grid=(bsz, nc),
        in_specs=[
            pl.BlockSpec((1, tm, d), lambda b, i: (b, i, 0)),
            pl.BlockSpec((1, N_MOD, d), lambda b, i: (mod_row(b), 0, 0)),
            pl.BlockSpec((d, IN_W), const, pipeline_mode=pl.Buffered(1)),
            tab, tab, tab, tab, tab, tab,
            pl.BlockSpec((1, 256), const),
            pl.BlockSpec((1, 128), const),
            pl.BlockSpec((256, 256), const),
        ],
        out_specs=[q_spec, k_spec(256), vt_spec(256),
                   pl.BlockSpec((1, tm, 768), lambda b, i: (b, i, 0)),
                   q_spec, k_spec(128), vt_spec(128),
                   q_spec, k_spec(128), vt_spec(128)],
        out_shape=[q_shape(), k_shape(256), vt_shape(256),
                   jax.ShapeDtypeStruct((bsz, t, 768), F32),
                   q_shape(), k_shape(128), vt_shape(128),
                   q_shape(), k_shape(128), vt_shape(128)],
        compiler_params=_params("parallel", "parallel"),
        name="in_projection",
    )(x, mods, w_in, *tabs_a, *tabs_h, qnw, knw, gsum)


def _whole_per_batch(arr):
    zeros = (0,) * (arr.ndim - 1)
    return pl.BlockSpec((1,) + arr.shape[1:], lambda b, i: (b,) + zeros, pipeline_mode=pl.Buffered(1))


def _attn_scratch(n_maps, tq, stage_rows):
    shapes = ([((n_maps, K_TILE, tq), BF16), ((n_maps, V_ROWS, tq), F32), ((n_maps, 8, tq), F32)]
              + [((stage_rows, tq), F32)] * STAGES + [((8, tq), F32)] * STAGES)
    return [pltpu.VMEM(shape, dtype) for shape, dtype in shapes]


def _key_offset(j, dq, grouped):
    return (j // 2) * dq if grouped else j * dq


def _fill_qpad(q_ref, qpad_ref, n_maps, dq, grouped):
    tq = q_ref.shape[1]
    qt = q_ref[0].astype(F32).T
    for j in range(n_maps):
        row = _key_offset(j, dq, grouped) % K_TILE
        parts = []
        if row:
            parts.append(jnp.zeros((row, tq), F32))
        parts.append(qt[j * dq:(j + 1) * dq, :])
        if K_TILE - row - dq:
            parts.append(jnp.zeros((K_TILE - row - dq, tq), F32))
        qpad_ref[j] = jnp.concatenate(parts, axis=0).astype(BF16)


def _score_unit(keys, j, nk, qpad_ref, s_ref, bm_ref, valid=None):
    tq = s_ref.shape[1]
    w = qpad_ref[j]
    bm = None
    for r in range(nk // SCORE_ROWS):
        rows = slice(r * SCORE_ROWS, (r + 1) * SCORE_ROWS)
        s = jnp.dot(keys(rows), w, preferred_element_type=F32)
        if valid is not None:
            s = jnp.where(valid(rows), s, NEG)
        s_ref[rows, :] = s
        part = jnp.max(s.reshape(SCORE_ROWS // 8, 8, tq), axis=0)
        bm = part if bm is None else jnp.maximum(bm, part)
    bm_ref[...] = bm


def _softmax_unit(values, j, nk, s_ref, bm_ref, m_ref, acc_ref, row0):
    m_old = m_ref[j, 0:1, :]
    m_new = jnp.maximum(m_old, jnp.max(bm_ref[...], axis=0, keepdims=True))
    alpha = jnp.exp2(m_old - m_new)
    pv = None
    for r in range(nk // PV_ROWS):
        rows = pl.ds(pl.multiple_of(row0 + r * PV_ROWS, PV_ROWS), PV_ROWS)
        p = jnp.exp2(s_ref[rows, :] - m_new)
        d = jnp.dot(values(slice(r * PV_ROWS, (r + 1) * PV_ROWS)), p.astype(BF16), preferred_element_type=F32)
        pv = d if pv is None else pv + d
    acc_ref[j] = alpha * acc_ref[j] + pv
    m_ref[j] = jnp.broadcast_to(m_new, m_ref.shape[1:])


def _init_state(m_ref, acc_ref, n_maps, sink_ref):
    tq = m_ref.shape[-1]
    for j in range(n_maps):
        if sink_ref is None:
            m_ref[j] = jnp.full(m_ref.shape[1:], NEG, F32)
            acc_ref[j] = jnp.zeros((V_ROWS, tq), F32)
        else:
            m_ref[j] = jnp.full(m_ref.shape[1:], sink_ref[j] * LOG2E, F32)
            acc_ref[j] = jnp.concatenate([jnp.zeros((HEAD_DIM, tq), F32), jnp.ones((ONES_ROWS, tq), F32)], axis=0)


def _normalised(acc_ref, j):
    acc = acc_ref[j]
    return acc[0:HEAD_DIM, :] / acc[HEAD_DIM:HEAD_DIM + 1, :]


def _finish_plain(o_ref, acc_ref, n_maps):
    ot = jnp.concatenate([_normalised(acc_ref, j) for j in range(n_maps)], axis=0)
    o_ref[0] = ot.T.astype(BF16)


def _finish_diff(o_ref, acc_ref, lamp_ref, subw_ref, n_maps, lambda_init):
    lp = lamp_ref[...]
    lam = (jnp.exp(jnp.sum(lp[0:1] * lp[1:2], axis=1, keepdims=True))
           - jnp.exp(jnp.sum(lp[2:3] * lp[3:4], axis=1, keepdims=True)) + lambda_init)
    outs = []
    for hd in range(n_maps // 2):
        o = _normalised(acc_ref, 2 * hd) - lam * _normalised(acc_ref, 2 * hd + 1)
        ms = jnp.mean(o * o, axis=0, keepdims=True)
        outs.append(o * lax.rsqrt(ms + EPS) * subw_ref[...] * (1.0 - lambda_init))
    o_ref[0] = jnp.concatenate(outs, axis=0).T.astype(BF16)


def _dense_attn_kernel(*refs, n_maps, dq, grouped, has_latent, has_sink, diff_lambda_init):
    refs = list(refs)
    q_ref, kc_ref, vct_ref, zero_ref = refs[:4]
    pos = 4
    k_ref = vt_ref = sink_ref = lamp_ref = subw_ref = None
    if has_latent:
        k_ref, vt_ref = refs[pos:pos + 2]
        pos += 2
    if has_sink:
        sink_ref = refs[pos]
        pos += 1
    if diff_lambda_init is not None:
        lamp_ref, subw_ref = refs[pos:pos + 2]
        pos += 2
    o_ref, qpad_ref, acc_ref, m_ref = refs[pos:pos + 4]
    s_refs = refs[pos + 4:pos + 4 + STAGES]
    bm_refs = refs[pos + 4 + STAGES:pos + 4 + 2 * STAGES]

    _fill_qpad(q_ref, qpad_ref, n_maps, dq, grouped)
    _init_state(m_ref, acc_ref, n_maps, sink_ref)
    n_ctx = kc_ref.shape[2]
    tk = k_ref.shape[3] if has_latent else 0
    n_lat = n_maps * k_ref.shape[2] if has_latent else 0
    row0 = zero_ref[0]

    def score(v, slot):
        if isinstance(v, int) and v < n_maps:
            kt = _key_offset(v, dq, grouped) // K_TILE
            _score_unit(lambda rows: kc_ref[0, kt, rows, :], v, n_ctx, qpad_ref, s_refs[slot], bm_refs[slot])
        else:
            u = v - n_maps
            j, c = u % n_maps, u // n_maps
            kt = _key_offset(j, dq, grouped) // K_TILE
            _score_unit(lambda rows: k_ref[0, kt, c, rows, :], j, tk, qpad_ref, s_refs[slot], bm_refs[slot])

    def softmax(v, slot):
        if isinstance(v, int) and v < n_maps:
            _softmax_unit(lambda cols: vct_ref[0, v // 2, :, cols], v, n_ctx, s_refs[slot], bm_refs[slot],
                          m_ref, acc_ref, row0)
        else:
            u = v - n_maps
            j, c = u % n_maps, u // n_maps
            _softmax_unit(lambda cols: vt_ref[0, c, j // 2, :, cols], j, tk, s_refs[slot], bm_refs[slot],
                          m_ref, acc_ref, row0)

    n_units = n_maps + n_lat
    for v in range(LOOKAHEAD):
        score(v, v % STAGES)
    for v in range(n_maps):
        if v + LOOKAHEAD < n_units:
            score(v + LOOKAHEAD, (v + LOOKAHEAD) % STAGES)
        softmax(v, v % STAGES)

    if has_latent:
        per_iter = STAGES * math.gcd(n_lat // STAGES, ROUNDS_PER_ITER)

        def body(i, carry):
            v0 = n_maps + per_iter * i
            for k in range(per_iter):
                score(jnp.minimum(v0 + k + LOOKAHEAD, n_units - 1), (k + LOOKAHEAD) % STAGES)
                softmax(v0 + k, k % STAGES)
            return carry
        lax.fori_loop(0, n_lat // per_iter, body, 0)
    if diff_lambda_init is None:
        _finish_plain(o_ref, acc_ref, n_maps)
    else:
        _finish_diff(o_ref, acc_ref, lamp_ref, subw_ref, n_maps, diff_lambda_init)


def _dense_attention(q, k_ctx, vt_ctx, k_lat, vt_lat, *, n_maps, dq, sink=None, lamp=None, subw=None,
                     diff_lambda_init=None):
    bsz, t, qw = q.shape
    key_tiles, n_ctx = k_ctx.shape[1:3]
    heads = vt_ctx.shape[1]
    grouped = key_tiles * K_TILE != n_maps * dq
    tq = min(Q_TILE, t)
    has_latent = k_lat is not None
    whole = _whole_per_batch
    assert n_maps % STAGES == 0 and n_ctx % PV_ROWS == 0

    in_specs = [pl.BlockSpec((1, tq, qw), lambda b, i: (b, i, 0)), whole(k_ctx), whole(vt_ctx),
                pl.BlockSpec(memory_space=pltpu.SMEM)]
    args = [q, k_ctx, vt_ctx, jnp.zeros((1,), jnp.int32)]
    stage_rows = n_ctx
    if has_latent:
        n_keys = k_lat.shape[2]
        tk = math.gcd(ATTN_CHUNK, n_keys)
        n_chunks = n_keys // tk
        assert tk % PV_ROWS == 0 and tk % SCORE_ROWS == 0
        k_lat = k_lat.reshape(bsz, key_tiles, n_chunks, tk, K_TILE)
        vt_lat = vt_lat.reshape(bsz, heads, V_ROWS, n_chunks, tk).transpose(0, 3, 1, 2, 4)
        in_specs += [whole(k_lat), whole(vt_lat)]
        args += [k_lat, vt_lat]
        stage_rows = max(n_ctx, tk)
    if sink is not None:
        in_specs.append(pl.BlockSpec(memory_space=pltpu.SMEM))
        args.append(sink)
    if diff_lambda_init is not None:
        in_specs += [pl.BlockSpec(lamp.shape, lambda b, i: (0, 0)), pl.BlockSpec(subw.shape, lambda b, i: (0, 0))]
        args += [lamp, subw]
    scratch = _attn_scratch(n_maps, tq, stage_rows)
    return pl.pallas_call(
        functools.partial(_dense_attn_kernel, n_maps=n_maps, dq=dq, grouped=grouped, has_latent=has_latent,
                          has_sink=sink is not None, diff_lambda_init=diff_lambda_init),
        grid=(bsz, t // tq),
        in_specs=in_specs,
        out_specs=pl.BlockSpec((1, tq, 256), lambda b, i: (b, i, 0)),
        out_shape=jax.ShapeDtypeStruct((bsz, t, 256), BF16),
        scratch_shapes=scratch,
        compiler_params=_params("parallel", "parallel"),
        name="dense_attention",
    )(*args)


def _window_attn_kernel(q_ref, kp_ref, kc_ref, kn_ref, vtp_ref, vtc_ref, vtn_ref, kctx_ref, vtctx_ref, zero_ref,
                        sink_ref, o_ref, qpad_ref, acc_ref, m_ref, *stage_refs, n_maps, seq):
    s_refs, bm_refs = stage_refs[:STAGES], stage_refs[STAGES:]
    tq = q_ref.shape[1]
    q0 = pl.program_id(1) * tq
    _fill_qpad(q_ref, qpad_ref, n_maps, HEAD_DIM, True)
    _init_state(m_ref, acc_ref, n_maps, sink_ref)
    k_win = jnp.concatenate([kp_ref[0], kc_ref[0], kn_ref[0]], axis=0)
    vt_win = jnp.concatenate([vtp_ref[0], vtc_ref[0], vtn_ref[0]], axis=2)
    n_ctx, n_win = kctx_ref.shape[2], k_win.shape[0]
    row0 = zero_ref[0]

    def in_window(rows):
        kpos = q0 - WINDOW + rows.start + lax.broadcasted_iota(jnp.int32, (rows.stop - rows.start, tq), 0)
        qpos = q0 + lax.broadcasted_iota(jnp.int32, (rows.stop - rows.start, tq), 1)
        return (jnp.abs(kpos - qpos) <= WINDOW) & (kpos >= 0) & (kpos < seq)

    def score(v, slot):
        if v < n_maps:
            _score_unit(lambda rows: kctx_ref[0, 0, rows, :], v, n_ctx, qpad_ref, s_refs[slot], bm_refs[slot])
        else:
            _score_unit(lambda rows: k_win[rows, :], v - n_maps, n_win, qpad_ref, s_refs[slot], bm_refs[slot],
                        valid=in_window)

    def softmax(v, slot):
        if v < n_maps:
            _softmax_unit(lambda cols: vtctx_ref[0, v // 2, :, cols], v, n_ctx, s_refs[slot], bm_refs[slot],
                          m_ref, acc_ref, row0)
        else:
            j = v - n_maps
            _softmax_unit(lambda cols: vt_win[j // 2][:, cols], j, n_win, s_refs[slot], bm_refs[slot],
                          m_ref, acc_ref, row0)

    for v in range(LOOKAHEAD):
        score(v, v % STAGES)
    for v in range(2 * n_maps):
        if v + LOOKAHEAD < 2 * n_maps:
            score(v + LOOKAHEAD, (v + LOOKAHEAD) % STAGES)
        softmax(v, v % STAGES)
    _finish_plain(o_ref, acc_ref, n_maps)


def _window_attention(q, k_ctx, vt_ctx, k_lat, vt_lat, sink):
    bsz, t, qw = q.shape
    key_tiles, _, kw = k_lat.shape[1:]
    assert key_tiles == 1
    tq = Q_TILE
    n_maps = 4
    k_flat = k_lat.reshape(bsz, t, kw)
    wb = tq // WINDOW
    nwb = t // WINDOW
    heads = vt_lat.shape[1]
    whole = _whole_per_batch

    def vt_blk(width, col):
        return pl.BlockSpec((1, heads, V_ROWS, width), lambda b, i: (b, 0, 0, col(i)))

    def prev_blk(i):
        return jnp.maximum(i * wb - 1, 0)

    def next_blk(i):
        return jnp.minimum(i * wb + wb, nwb - 1)

    in_specs = [
        pl.BlockSpec((1, tq, qw), lambda b, i: (b, i, 0)),
        pl.BlockSpec((1, WINDOW, kw), lambda b, i: (b, prev_blk(i), 0)),
        pl.BlockSpec((1, tq, kw), lambda b, i: (b, i, 0)),
        pl.BlockSpec((1, WINDOW, kw), lambda b, i: (b, next_blk(i), 0)),
        vt_blk(WINDOW, prev_blk),
        vt_blk(tq, lambda i: i),
        vt_blk(WINDOW, next_blk),
        whole(k_ctx), whole(vt_ctx),
        pl.BlockSpec(memory_space=pltpu.SMEM),
        pl.BlockSpec(memory_space=pltpu.SMEM),
    ]
    stage_rows = max(k_ctx.shape[2], tq + 2 * WINDOW)
    return pl.pallas_call(
        functools.partial(_window_attn_kernel, n_maps=n_maps, seq=t),
        grid=(bsz, t // tq),
        in_specs=in_specs,
        out_specs=pl.BlockSpec((1, tq, 256), lambda b, i: (b, i, 0)),
        out_shape=jax.ShapeDtypeStruct((bsz, t, 256), BF16),
        scratch_shapes=_attn_scratch(n_maps, tq, stage_rows),
        compiler_params=_params("parallel", "parallel"),
        name="window_attention",
    )(q, k_flat, k_flat, k_flat, vt_lat, vt_lat, vt_lat, k_ctx, vt_ctx, jnp.zeros((1,), jnp.int32), sink)


def _outproj_kernel(x_ref, mod_ref, ya_ref, g3_ref, g3p_ref, g3n_ref, yw_ref, yg_ref, cw_ref, w_ref, g_ref, b_ref,
                    o_ref, *, alpha):
    i = pl.program_id(1)
    last = pl.num_programs(1) - 1
    tm = x_ref.shape[1]
    gb = g3_ref[0, :, 0:256]
    hid = g3_ref[0, :, 256:512] * g3_ref[0, :, 512:768]
    halo = g3p_ref.shape[1]
    h_prev = g3p_ref[0, halo - 1:halo, 256:512] * g3p_ref[0, halo - 1:halo, 512:768]
    h_next = g3n_ref[0, 0:1, 256:512] * g3n_ref[0, 0:1, 512:768]
    h_prev = jnp.where(i == 0, 0.0, h_prev)
    h_next = jnp.where(i == last, 0.0, h_next)
    row = lax.broadcasted_iota(jnp.int32, hid.shape, 0)
    below = jnp.where(row == 0, h_prev, pltpu.roll(hid, 1, axis=0))
    above = jnp.where(row == tm - 1, h_next, pltpu.roll(hid, tm - 1, axis=0))
    yb = gb * (cw_ref[0:1, :] * below + cw_ref[1:2, :] * hid + cw_ref[2:3, :] * above)
    y = jnp.dot(ya_ref[0], w_ref[0:256, :], preferred_element_type=F32)
    y += jnp.dot(yb.astype(BF16), w_ref[256:512, :], preferred_element_type=F32)
    y += jnp.dot(yw_ref[0], w_ref[512:768, :], preferred_element_type=F32)
    y += jnp.dot(yg_ref[0], w_ref[768:1024, :], preferred_element_type=F32)
    gate = mod_ref[0, 5:6, :]
    o_ref[0] = _layer_norm(alpha * x_ref[0] + gate * y, g_ref[...], b_ref[...])


def _out_projection(x, mods, mod_row, ya, g3, yw, yg, conv_w, w_out, ln_g, ln_b, alpha):
    bsz, t, d = x.shape
    tm = min(TOK_TILE, t)
    halo = 8
    hb = tm // halo
    n_halo = t // halo
    const = lambda b, i: (0, 0)
    y_spec = pl.BlockSpec((1, tm, 256), lambda b, i: (b, i, 0))
    return pl.pallas_call(
        functools.partial(_outproj_kernel, alpha=alpha),
        grid=(bsz, t // tm),
        in_specs=[
            pl.BlockSpec((1, tm, d), lambda b, i: (b, i, 0)),
            pl.BlockSpec((1, N_MOD, d), lambda b, i: (mod_row(b), 0, 0)),
            y_spec,
            pl.BlockSpec((1, tm, 768), lambda b, i: (b, i, 0)),
            pl.BlockSpec((1, halo, 768), lambda b, i: (b, jnp.maximum(i * hb - 1, 0), 0)),
            pl.BlockSpec((1, halo, 768), lambda b, i: (b, jnp.minimum(i * hb + hb, n_halo - 1), 0)),
            y_spec, y_spec,
            pl.BlockSpec(conv_w.shape, const),
            pl.BlockSpec(w_out.shape, const, pipeline_mode=pl.Buffered(1)),
            pl.BlockSpec((1, d), const),
            pl.BlockSpec((1, d), const),
        ],
        out_specs=pl.BlockSpec((1, tm, d), lambda b, i: (b, i, 0)),
        out_shape=jax.ShapeDtypeStruct(x.shape, F32),
        compiler_params=_params("parallel", "parallel"),
        name="out_projection",
    )(x, mods, ya, g3, g3, g3, yw, yg, conv_w, w_out, ln_g.reshape(1, d), ln_b.reshape(1, d))


def _rope_tables(rows, dim):
    n_freq = dim // 4
    inv_freq = ROPE_BASE ** (-jnp.arange(n_freq, dtype=F32) / n_freq)
    ang_r = jnp.arange(rows, dtype=F32)[:, None] * inv_freq
    ang_c = jnp.arange(GRID_W, dtype=F32)[:, None] * inv_freq

    def on_grid(of_row, of_col):
        r = jnp.broadcast_to(of_row[:, None, :], (rows, GRID_W, n_freq))
        c = jnp.broadcast_to(of_col[None, :, :], (rows, GRID_W, n_freq))
        return jnp.concatenate([r, r, c, c], axis=-1).reshape(rows * GRID_W, dim)

    cos = on_grid(jnp.cos(ang_r), jnp.cos(ang_c))
    sin = on_grid(jnp.sin(ang_r), jnp.sin(ang_c))
    reps = 128 // dim
    cos = jnp.tile(cos, (1, reps))
    sin = jnp.tile(sin, (1, reps))
    first_half = (jnp.arange(128) % (dim // 2)) < (dim // 4)
    sin_next = jnp.where(first_half, -sin, 0.0)
    sin_prev = jnp.where(first_half, 0.0, sin)
    return cos, sin_next, sin_prev


def _identity_tables(t):
    return jnp.ones((t, 128), F32), jnp.zeros((t, 128), F32), jnp.zeros((t, 128), F32)


def kernel(x, c, ctx, c_ctx, w_mod, b_mod, ln_g, ln_b, w_gu1, w_dn1, w_in, w_out, conv_w, lam_q1, lam_k1, lam_q2,
           lam_k2, subln_w, sink, qn_w, kn_w, w_gu2, w_dn2):
    bsz, seq, d = x.shape
    ctx_len = ctx.shape[1]
    depth = w_mod.shape[0]
    alpha = (2.0 * depth) ** 0.25
    assert seq % TOK_TILE == 0 and ctx_len % Q_TILE == 0 and bsz < MOD_ROWS

    tabs_a = _rope_tables(seq // GRID_W, A_QK_DIM)
    tabs_h = _rope_tables(seq // GRID_W, HEAD_DIM)
    tabs_id = _identity_tables(ctx_len)
    gsum = (jnp.arange(256)[:, None] // HEAD_DIM == jnp.arange(256)[None, :] // HEAD_DIM).astype(BF16)

    cvec = jnp.zeros((MOD_ROWS, d), F32).at[:bsz].set(c).at[bsz].set(c_ctx)
    mods_all = _mod_vectors(cvec, w_mod, b_mod).reshape(depth, MOD_ROWS, N_MOD, d)
    lat_row = lambda b: b
    ctx_row = lambda b: bsz

    xc = ctx
    for l in range(depth):
        need_ctx = l < depth - 1
        lambda_init = 0.8 - 0.6 * math.exp(-0.3 * l)
        mods = mods_all[l]
        wgu1, wdn1 = w_gu1[l].astype(BF16), w_dn1[l].astype(BF16)
        wgu2, wdn2 = w_gu2[l].astype(BF16), w_dn2[l].astype(BF16)
        win, wout = w_in[l].astype(BF16), w_out[l].astype(BF16)
        qnw = jnp.tile(qn_w[l], 256 // HEAD_DIM).reshape(1, 256)
        knw = jnp.tile(kn_w[l], 128 // HEAD_DIM).reshape(1, 128)
        lamp = jnp.stack([lam_q1[l], lam_k1[l], lam_q2[l], lam_k2[l]])
        subw = subln_w[l].reshape(HEAD_DIM, 1)

        x = _ffn_step(x, mods, lat_row, 0, wgu1, wdn1, ln_g[l, 0], ln_b[l, 0], alpha)
        xc = _ffn_step(xc, mods, ctx_row, 0, wgu1, wdn1, ln_g[l, 0], ln_b[l, 0], alpha)

        qa, ka, vat, g3, qc, kc, vct, qd, kd, vdt = _in_projection(x, mods, lat_row, win, tabs_a, tabs_h, qnw, knw, gsum)
        (qa_c, ka_c, vat_c, g3_c, qc_c, kc_c, vct_c, qd_c, kd_c, vdt_c) = _in_projection(
            xc, mods, ctx_row, win, tabs_id, tabs_id, qnw, knw, gsum)
        ya = _dense_attention(qa, ka_c, vat_c, ka, vat, n_maps=8, dq=A_QK_DIM, lamp=lamp, subw=subw,
                              diff_lambda_init=lambda_init)
        yw = _window_attention(qc, kc_c, vct_c, kc, vct, sink[l])
        yg = _dense_attention(qd, kd_c, vdt_c, kd, vdt, n_maps=4, dq=HEAD_DIM)
        x = _out_projection(x, mods, lat_row, ya, g3, yw, yg, conv_w[l], wout, ln_g[l, 1], ln_b[l, 1], alpha)

        x = _ffn_step(x, mods, lat_row, 6, wgu2, wdn2, ln_g[l, 2], ln_b[l, 2], alpha)

        if need_ctx:
            ya_c = _dense_attention(qa_c, ka_c, vat_c, None, None, n_maps=8, dq=A_QK_DIM, lamp=lamp, subw=subw,
                                    diff_lambda_init=lambda_init)
            yw_c = _dense_attention(qc_c, kc_c, vct_c, None, None, n_maps=4, dq=HEAD_DIM, sink=sink[l])
            yg_c = _dense_attention(qd_c, kd_c, vdt_c, None, None, n_maps=4, dq=HEAD_DIM)
            xc = _out_projection(xc, mods, ctx_row, ya_c, g3_c, yw_c, yg_c, conv_w[l], wout, ln_g[l, 1], ln_b[l, 1],
                                 alpha)
            xc = _ffn_step(xc, mods, ctx_row, 6, wgu2, wdn2, ln_g[l, 2], ln_b[l, 2], alpha)
    return x
```

```python
import functools
import math

import jax
import jax.numpy as jnp
from jax import lax
from jax.experimental import pallas as pl
from jax.experimental.pallas import tpu as pltpu

F32 = jnp.float32
BF16 = jnp.bfloat16

GRID_W = 64
HEAD_DIM = 64
WINDOW = 128
ROPE_BASE = 10000.0
EPS = 1e-6
NEG = -1e30
A_QK_DIM = 32
N_MOD = 9
MOD_ROWS = 8
VMEM_LIMIT = 56 * 1024 * 1024

OFF_AQ, OFF_AK, OFF_AV = 0, 256, 512
OFF_B = 768
OFF_CQ, OFF_CK, OFF_CV = 1536, 1792, 1920
OFF_DQ, OFF_DK, OFF_DV = 2048, 2304, 2432
IN_W = 2560

TOK_TILE = 512
FFN_TILE = 512
Q_TILE = 256
DENSE_Q_TILE = 256
FF_CHUNK = 256
SCORE_ROWS = 128
PV_ROWS = 256
ATTN_CHUNK = 1024
STAGES = 4
LOOKAHEAD = 2
ROUNDS_PER_ITER = 8
K_TILE = 128
ONES_ROWS = 16
V_ROWS = HEAD_DIM + ONES_ROWS
LOG2E = math.log2(math.e)


def _params(*sem):
    return pltpu.CompilerParams(dimension_semantics=sem, vmem_limit_bytes=VMEM_LIMIT)


def _layer_norm(r, g, b):
    mu = jnp.mean(r, axis=-1, keepdims=True)
    d = r - mu
    var = jnp.mean(d * d, axis=-1, keepdims=True)
    return d * lax.rsqrt(var + EPS) * g + b


def _mod_kernel(c_ref, w_ref, b_ref, o_ref):
    c = c_ref[...]
    a = (c * jax.nn.sigmoid(c)).astype(BF16)
    o_ref[0] = jnp.dot(a, w_ref[0].astype(BF16), preferred_element_type=F32) + b_ref[0]


def _mod_vectors(cvec, w_mod, b_mod):
    depth, d, n = w_mod.shape
    tn = n // 8
    return pl.pallas_call(
        _mod_kernel,
        grid=(depth, n // tn),
        in_specs=[
            pl.BlockSpec((MOD_ROWS, d), lambda l, j: (0, 0)),
            pl.BlockSpec((1, d, tn), lambda l, j: (l, 0, j)),
            pl.BlockSpec((1, 1, tn), lambda l, j: (l, 0, j)),
        ],
        out_specs=pl.BlockSpec((1, MOD_ROWS, tn), lambda l, j: (l, 0, j)),
        out_shape=jax.ShapeDtypeStruct((depth, MOD_ROWS, n), F32),
        compiler_params=_params("parallel", "parallel"),
        name="mod_vectors",
    )(cvec, w_mod, b_mod.reshape(depth, 1, n))


def _ffn_kernel(x_ref, mod_ref, wgu_ref, wdn_ref, g_ref, b_ref, o_ref, acc_ref, *, mod_base, alpha):
    x = x_ref[0]
    shift = mod_ref[0, mod_base:mod_base + 1, :]
    scale = mod_ref[0, mod_base + 1:mod_base + 2, :]
    gate = mod_ref[0, mod_base + 2:mod_base + 3, :]
    h = (x * (1.0 + scale) + shift).astype(BF16)
    d_ff = wdn_ref.shape[0]
    for c in range(d_ff // FF_CHUNK):
        lo = c * FF_CHUNK
        g = jnp.dot(h, wgu_ref[:, lo:lo + FF_CHUNK], preferred_element_type=F32)
        u = jnp.dot(h, wgu_ref[:, d_ff + lo:d_ff + lo + FF_CHUNK], preferred_element_type=F32)
        a = (g * jax.nn.sigmoid(g) * u).astype(BF16)
        y = jnp.dot(a, wdn_ref[lo:lo + FF_CHUNK, :], preferred_element_type=F32)
        if c == 0:
            acc_ref[...] = y
        else:
            acc_ref[...] += y
    r = alpha * x + (0.5 * gate) * acc_ref[...]
    o_ref[0] = _layer_norm(r, g_ref[...], b_ref[...])


def _ffn_step(x, mods, mod_row, mod_base, w_gu, w_dn, ln_g, ln_b, alpha):
    bsz, t, d = x.shape
    d_ff = w_dn.shape[0]
    tm = min(FFN_TILE, t)
    const = lambda b, i: (0, 0)
    return pl.pallas_call(
        functools.partial(_ffn_kernel, mod_base=mod_base, alpha=alpha),
        grid=(bsz, t // tm),
        in_specs=[
            pl.BlockSpec((1, tm, d), lambda b, i: (b, i, 0)),
            pl.BlockSpec((1, N_MOD, d), lambda b, i: (mod_row(b), 0, 0)),
            pl.BlockSpec((d, 2 * d_ff), const, pipeline_mode=pl.Buffered(1)),
            pl.BlockSpec((d_ff, d), const, pipeline_mode=pl.Buffered(1)),
            pl.BlockSpec((1, d), const),
            pl.BlockSpec((1, d), const),
        ],
        out_specs=pl.BlockSpec((1, tm, d), lambda b, i: (b, i, 0)),
        out_shape=jax.ShapeDtypeStruct(x.shape, F32),
        scratch_shapes=[pltpu.VMEM((tm, d), F32)],
        compiler_params=_params("parallel", "parallel"),
        name="ffn_step",
    )(x, mods, w_gu, w_dn, ln_g.reshape(1, d), ln_b.reshape(1, d))


def _rope(x, cos, sin_next, sin_prev, quarter):
    width = x.shape[1]
    reps = width // cos.shape[1]
    if reps > 1:
        cos = jnp.concatenate([cos] * reps, axis=1)
        sin_next = jnp.concatenate([sin_next] * reps, axis=1)
        sin_prev = jnp.concatenate([sin_prev] * reps, axis=1)
    x_next = pltpu.roll(x, width - quarter, axis=1)
    x_prev = pltpu.roll(x, quarter, axis=1)
    return x * cos + x_next * sin_next + x_prev * sin_prev


def _head_rms(x, gsum_ref, w):
    x2 = x * x
    hi = x2.astype(BF16)
    lo = (x2 - hi.astype(F32)).astype(BF16)
    gs = gsum_ref[:x.shape[1], :x.shape[1]]
    ss = jnp.dot(hi, gs, preferred_element_type=F32) + jnp.dot(lo, gs, preferred_element_type=F32)
    return x * lax.rsqrt(ss * (1.0 / HEAD_DIM) + EPS) * w


def _store_k(k_ref, k):
    for kt in range(k.shape[1] // K_TILE):
        k_ref[0, kt] = k[:, kt * K_TILE:(kt + 1) * K_TILE].astype(BF16)


def _store_vt(vt_ref, v):
    vt = v.T
    tm = v.shape[0]
    for hd in range(v.shape[1] // HEAD_DIM):
        vt_ref[0, hd, 0:HEAD_DIM, :] = vt[hd * HEAD_DIM:(hd + 1) * HEAD_DIM, :].astype(BF16)
        vt_ref[0, hd, HEAD_DIM:V_ROWS, :] = jnp.ones((ONES_ROWS, tm), BF16)


def _inproj_kernel(x_ref, mod_ref, w_ref, ca_ref, san_ref, sap_ref, ch_ref, shn_ref, shp_ref,
                   qnw_ref, knw_ref, gsum_ref,
                   qa_ref, ka_ref, vat_ref, g3_ref, qc_ref, kc_ref, vct_ref, qd_ref, kd_ref, vdt_ref):
    x = x_ref[0]
    shift = mod_ref[0, 3:4, :]
    scale = mod_ref[0, 4:5, :]
    h = (x * (1.0 + scale) + shift).astype(BF16)

    def seg(lo, hi):
        return jnp.dot(h, w_ref[:, lo:hi], preferred_element_type=F32)

    rope_a = functools.partial(_rope, cos=ca_ref[...], sin_next=san_ref[...], sin_prev=sap_ref[...],
                               quarter=A_QK_DIM // 4)
    rope_h = functools.partial(_rope, cos=ch_ref[...], sin_next=shn_ref[...], sin_prev=shp_ref[...],
                               quarter=HEAD_DIM // 4)
    qa_ref[0] = (rope_a(seg(OFF_AQ, OFF_AK)) * (LOG2E * A_QK_DIM ** -0.5)).astype(BF16)
    _store_k(ka_ref, rope_a(seg(OFF_AK, OFF_AV)))
    _store_vt(vat_ref, seg(OFF_AV, OFF_B))
    g3_ref[0] = seg(OFF_B, OFF_CQ)
    qc_ref[0] = (rope_h(seg(OFF_CQ, OFF_CK)) * (LOG2E * HEAD_DIM ** -0.5)).astype(BF16)
    kv = seg(OFF_CK, OFF_DQ)
    _store_k(kc_ref, rope_h(kv[:, :OFF_CV - OFF_CK]))
    _store_vt(vct_ref, kv[:, OFF_CV - OFF_CK:])
    qd = _head_rms(seg(OFF_DQ, OFF_DK), gsum_ref, qnw_ref[...])
    qd_ref[0] = (rope_h(qd) * (LOG2E * HEAD_DIM ** -0.5)).astype(BF16)
    kv = seg(OFF_DK, IN_W)
    kd = _head_rms(kv[:, :OFF_DV - OFF_DK], gsum_ref, knw_ref[...])
    _store_k(kd_ref, rope_h(kd))
    _store_vt(vdt_ref, kv[:, OFF_DV - OFF_DK:])


def _in_projection(x, mods, mod_row, w_in, tabs_a, tabs_h, qnw, knw, gsum):
    bsz, t, d = x.shape
    tm = min(TOK_TILE, t)
    nc = t // tm
    const = lambda b, i: (0, 0)
    tab = pl.BlockSpec((tm, 128), lambda b, i: (i, 0))
    q_spec = pl.BlockSpec((1, tm, 256), lambda b, i: (b, i, 0))

    def k_spec(w):
        return pl.BlockSpec((1, w // K_TILE, tm, K_TILE), lambda b, i: (b, 0, i, 0))

    def vt_spec(w):
        return pl.BlockSpec((1, w // HEAD_DIM, V_ROWS, tm), lambda b, i: (b, 0, 0, i))

    def q_shape():
        return jax.ShapeDtypeStruct((bsz, t, 256), BF16)

    def k_shape(w):
        return jax.ShapeDtypeStruct((bsz, w // K_TILE, t, K_TILE), BF16)

    def vt_shape(w):
        return jax.ShapeDtypeStruct((bsz, w // HEAD_DIM, V_ROWS, t), BF16)

    return pl.pallas_call(
        _inproj_kernel,
        grid=(bsz, nc),
        in_specs=[
            pl.BlockSpec((1, tm, d), lambda b, i: (b, i, 0)),
            pl.BlockSpec((1, N_MOD, d), lambda b, i: (mod_row(b), 0, 0)),
            pl.BlockSpec((d, IN_W), const, pipeline_mode=pl.Buffered(1)),
            tab, tab, tab, tab, tab, tab,
            pl.BlockSpec((1, 256), const),
            pl.BlockSpec((1, 128), const),
            pl.BlockSpec((256, 256), const),
        ],
        out_specs=[q_spec, k_spec(256), vt_spec(256),
                   pl.BlockSpec((1, tm, 768), lambda b, i: (b, i, 0)),
                   q_spec, k_spec(128), vt_spec(128),
                   q_spec, k_spec(128), vt_spec(128)],
        out_shape=[q_shape(), k_shape(256), vt_shape(256),
                   jax.ShapeDtypeStruct((bsz, t, 768), F32),
                   q_shape(), k_shape(128), vt_shape(128),
                   q_shape(), k_shape(128), vt_shape(128)],
        compiler_params=_params("parallel", "parallel"),
        name="in_projection",
    )(x, mods, w_in, *tabs_a, *tabs_h, qnw, knw, gsum)


def _whole_per_batch(arr):
    zeros = (0,) * (arr.ndim - 1)
    return pl.BlockSpec((1,) + arr.shape[1:], lambda b, i: (b,) + zeros, pipeline_mode=pl.Buffered(1))


def _attn_scratch(n_maps, tq, stage_rows):
    shapes = ([((n_maps, K_TILE, tq), BF16), ((n_maps, V_ROWS, tq), F32), ((n_maps, 8, tq), F32)]
              + [((stage_rows, tq), F32)] * STAGES + [((8, tq), F32)] * STAGES)
    return [pltpu.VMEM(shape, dtype) for shape, dtype in shapes]


def _key_offset(j, dq, grouped):
    return (j // 2) * dq if grouped else j * dq


def _fill_qpad(q_ref, qpad_ref, n_maps, dq, grouped):
    tq = q_ref.shape[1]
    qt = q_ref[0].astype(F32).T
    for j in range(n_maps):
        row = _key_offset(j, dq, grouped) % K_TILE
        parts = []
        if row:
            parts.append(jnp.zeros((row, tq), F32))
        parts.append(qt[j * dq:(j + 1) * dq, :])
        if K_TILE - row - dq:
            parts.append(jnp.zeros((K_TILE - row - dq, tq), F32))
        qpad_ref[j] = jnp.concatenate(parts, axis=0).astype(BF16)


def _score_unit(keys, j, nk, qpad_ref, s_ref, bm_ref, valid=None):
    tq = s_ref.shape[1]
    w = qpad_ref[j]
    bm = None
    for r in range(nk // SCORE_ROWS):
        rows = slice(r * SCORE_ROWS, (r + 1) * SCORE_ROWS)
        s = jnp.dot(keys(rows), w, preferred_element_type=F32)
        if valid is not None:
            s = jnp.where(valid(rows), s, NEG)
        s_ref[rows, :] = s
        part = jnp.max(s.reshape(SCORE_ROWS // 8, 8, tq), axis=0)
        bm = part if bm is None else jnp.maximum(bm, part)
    bm_ref[...] = bm


def _softmax_unit(values, j, nk, s_ref, bm_ref, m_ref, acc_ref, row0):
    m_old = m_ref[j, 0:1, :]
    m_new = jnp.maximum(m_old, jnp.max(bm_ref[...], axis=0, keepdims=True))
    alpha = jnp.exp2(m_old - m_new)
    pv = None
    for r in range(nk // PV_ROWS):
        rows = pl.ds(pl.multiple_of(row0 + r * PV_ROWS, PV_ROWS), PV_ROWS)
        p = jnp.exp2(s_ref[rows, :] - m_new)
        d = jnp.dot(values(slice(r * PV_ROWS, (r + 1) * PV_ROWS)), p.astype(BF16), preferred_element_type=F32)
        pv = d if pv is None else pv + d
    acc_ref[j] = alpha * acc_ref[j] + pv
    m_ref[j] = jnp.broadcast_to(m_new, m_ref.shape[1:])


def _init_state(m_ref, acc_ref, n_maps, sink_ref):
    tq = m_ref.shape[-1]
    for j in range(n_maps):
        if sink_ref is None:
            m_ref[j] = jnp.full(m_ref.shape[1:], NEG, F32)
            acc_ref[j] = jnp.zeros((V_ROWS, tq), F32)
        else:
            m_ref[j] = jnp.full(m_ref.shape[1:], sink_ref[j] * LOG2E, F32)
            acc_ref[j] = jnp.concatenate([jnp.zeros((HEAD_DIM, tq), F32), jnp.ones((ONES_ROWS, tq), F32)], axis=0)


def _normalised(acc_ref, j):
    acc = acc_ref[j]
    return acc[0:HEAD_DIM, :] / acc[HEAD_DIM:HEAD_DIM + 1, :]


def _finish_plain(o_ref, acc_ref, n_maps):
    ot = jnp.concatenate([_normalised(acc_ref, j) for j in range(n_maps)], axis=0)
    o_ref[0] = ot.T.astype(BF16)


def _finish_diff(o_ref, acc_ref, lamp_ref, subw_ref, n_maps, lambda_init):
    lp = lamp_ref[...]
    lam = (jnp.exp(jnp.sum(lp[0:1] * lp[1:2], axis=1, keepdims=True))
           - jnp.exp(jnp.sum(lp[2:3] * lp[3:4], axis=1, keepdims=True)) + lambda_init)
    outs = []
    for hd in range(n_maps // 2):
        o = _normalised(acc_ref, 2 * hd) - lam * _normalised(acc_ref, 2 * hd + 1)
        ms = jnp.mean(o * o, axis=0, keepdims=True)
        outs.append(o * lax.rsqrt(ms + EPS) * subw_ref[...] * (1.0 - lambda_init))
    o_ref[0] = jnp.concatenate(outs, axis=0).T.astype(BF16)


def _dense_attn_kernel(*refs, n_maps, dq, grouped, has_latent, has_sink, diff_lambda_init):
    refs = list(refs)
    q_ref, kc_ref, vct_ref, zero_ref = refs[:4]
    pos = 4
    k_ref = vt_ref = sink_ref = lamp_ref = subw_ref = None
    if has_latent:
        k_ref, vt_ref = refs[pos:pos + 2]
        pos += 2
    if has_sink:
        sink_ref = refs[pos]
        pos += 1
    if diff_lambda_init is not None:
        lamp_ref, subw_ref = refs[pos:pos + 2]
        pos += 2
    o_ref, qpad_ref, acc_ref, m_ref = refs[pos:pos + 4]
    s_refs = refs[pos + 4:pos + 4 + STAGES]
    bm_refs = refs[pos + 4 + STAGES:pos + 4 + 2 * STAGES]

    _fill_qpad(q_ref, qpad_ref, n_maps, dq, grouped)
    _init_state(m_ref, acc_ref, n_maps, sink_ref)
    n_ctx = kc_ref.shape[2]
    tk = k_ref.shape[3] if has_latent else 0
    n_lat = n_maps * k_ref.shape[2] if has_latent else 0
    row0 = zero_ref[0]

    def score(v, slot):
        if isinstance(v, int) and v < n_maps:
            kt = _key_offset(v, dq, grouped) // K_TILE
            _score_unit(lambda rows: kc_ref[0, kt, rows, :], v, n_ctx, qpad_ref, s_refs[slot], bm_refs[slot])
        else:
            u = v - n_maps
            j, c = u % n_maps, u // n_maps
            kt = _key_offset(j, dq, grouped) // K_TILE
            _score_unit(lambda rows: k_ref[0, kt, c, rows, :], j, tk, qpad_ref, s_refs[slot], bm_refs[slot])

    def softmax(v, slot):
        if isinstance(v, int) and v < n_maps:
            _softmax_unit(lambda cols: vct_ref[0, v // 2, :, cols], v, n_ctx, s_refs[slot], bm_refs[slot],
                          m_ref, acc_ref, row0)
        else:
            u = v - n_maps
            j, c = u % n_maps, u // n_maps
            _softmax_unit(lambda cols: vt_ref[0, c, j // 2, :, cols], j, tk, s_refs[slot], bm_refs[slot],
                          m_ref, acc_ref, row0)

    n_units = n_maps + n_lat
    for v in range(LOOKAHEAD):
        score(v, v % STAGES)
    for v in range(n_maps):
        if v + LOOKAHEAD < n_units:
            score(v + LOOKAHEAD, (v + LOOKAHEAD) % STAGES)
        softmax(v, v % STAGES)

    if has_latent:
        per_iter = STAGES * math.gcd(n_lat // STAGES, ROUNDS_PER_ITER)

        def body(i, carry):
            v0 = n_maps + per_iter * i
            for k in range(per_iter):
                score(jnp.minimum(v0 + k + LOOKAHEAD, n_units - 1), (k + LOOKAHEAD) % STAGES)
                softmax(v0 + k, k % STAGES)
            return carry
        lax.fori_loop(0, n_lat // per_iter, body, 0)
    if diff_lambda_init is None:
        _finish_plain(o_ref, acc_ref, n_maps)
    else:
        _finish_diff(o_ref, acc_ref, lamp_ref, subw_ref, n_maps, diff_lambda_init)


def _dense_attention(q, k_ctx, vt_ctx, k_lat, vt_lat, *, n_maps, dq, sink=None, lamp=None, subw=None,
                     diff_lambda_init=None):
    bsz, t, qw = q.shape
    key_tiles, n_ctx = k_ctx.shape[1:3]
    heads = vt_ctx.shape[1]
    grouped = key_tiles * K_TILE != n_maps * dq
    tq = min(DENSE_Q_TILE, t)
    has_latent = k_lat is not None
    whole = _whole_per_batch
    assert n_maps % STAGES == 0 and n_ctx % PV_ROWS == 0

    in_specs = [pl.BlockSpec((1, tq, qw), lambda b, i: (b, i, 0)), whole(k_ctx), whole(vt_ctx),
                pl.BlockSpec(memory_space=pltpu.SMEM)]
    args = [q, k_ctx, vt_ctx, jnp.zeros((1,), jnp.int32)]
    stage_rows = n_ctx
    if has_latent:
        n_keys = k_lat.shape[2]
        tk = math.gcd(ATTN_CHUNK, n_keys)
        n_chunks = n_keys // tk
        assert tk % PV_ROWS == 0 and tk % SCORE_ROWS == 0
        k_lat = k_lat.reshape(bsz, key_tiles, n_chunks, tk, K_TILE)
        vt_lat = vt_lat.reshape(bsz, heads, V_ROWS, n_chunks, tk).transpose(0, 3, 1, 2, 4)
        in_specs += [whole(k_lat), whole(vt_lat)]
        args += [k_lat, vt_lat]
        stage_rows = max(n_ctx, tk)
    if sink is not None:
        in_specs.append(pl.BlockSpec(memory_space=pltpu.SMEM))
        args.append(sink)
    if diff_lambda_init is not None:
        in_specs += [pl.BlockSpec(lamp.shape, lambda b, i: (0, 0)), pl.BlockSpec(subw.shape, lambda b, i: (0, 0))]
        args += [lamp, subw]
    scratch = _attn_scratch(n_maps, tq, stage_rows)
    return pl.pallas_call(
        functools.partial(_dense_attn_kernel, n_maps=n_maps, dq=dq, grouped=grouped, has_latent=has_latent,
                          has_sink=sink is not None, diff_lambda_init=diff_lambda_init),
        grid=(bsz, t // tq),
        in_specs=in_specs,
        out_specs=pl.BlockSpec((1, tq, 256), lambda b, i: (b, i, 0)),
        out_shape=jax.ShapeDtypeStruct((bsz, t, 256), BF16),
        scratch_shapes=scratch,
        compiler_params=_params("parallel", "parallel"),
        name="dense_attention",
    )(*args)


def _window_attn_kernel(q_ref, kp_ref, kc_ref, kn_ref, vtp_ref, vtc_ref, vtn_ref, kctx_ref, vtctx_ref, zero_ref,
                        sink_ref, o_ref, qpad_ref, acc_ref, m_ref, *stage_refs, n_maps, seq):
    s_refs, bm_refs = stage_refs[:STAGES], stage_refs[STAGES:]
    tq = q_ref.shape[1]
    q0 = pl.program_id(1) * tq
    _fill_qpad(q_ref, qpad_ref, n_maps, HEAD_DIM, True)
    _init_state(m_ref, acc_ref, n_maps, sink_ref)
    k_win = jnp.concatenate([kp_ref[0], kc_ref[0], kn_ref[0]], axis=0)
    vt_win = jnp.concatenate([vtp_ref[0], vtc_ref[0], vtn_ref[0]], axis=2)
    n_ctx, n_win = kctx_ref.shape[2], k_win.shape[0]
    row0 = zero_ref[0]

    def in_window(rows):
        kpos = q0 - WINDOW + rows.start + lax.broadcasted_iota(jnp.int32, (rows.stop - rows.start, tq), 0)
        qpos = q0 + lax.broadcasted_iota(jnp.int32, (rows.stop - rows.start, tq), 1)
        return (jnp.abs(kpos - qpos) <= WINDOW) & (kpos >= 0) & (kpos < seq)

    def score(v, slot):
        if v < n_maps:
            _score_unit(lambda rows: kctx_ref[0, 0, rows, :], v, n_ctx, qpad_ref, s_refs[slot], bm_refs[slot])
        else:
            _score_unit(lambda rows: k_win[rows, :], v - n_maps, n_win, qpad_ref, s_refs[slot], bm_refs[slot],
                        valid=in_window)

    def softmax(v, slot):
        if v < n_maps:
            _softmax_unit(lambda cols: vtctx_ref[0, v // 2, :, cols], v, n_ctx, s_refs[slot], bm_refs[slot],
                          m_ref, acc_ref, row0)
        else:
            j = v - n_maps
            _softmax_unit(lambda cols: vt_win[j // 2][:, cols], j, n_win, s_refs[slot], bm_refs[slot],
                          m_ref, acc_ref, row0)

    for v in range(LOOKAHEAD):
        score(v, v % STAGES)
    for v in range(2 * n_maps):
        if v + LOOKAHEAD < 2 * n_maps:
            score(v + LOOKAHEAD, (v + LOOKAHEAD) % STAGES)
        softmax(v, v % STAGES)
    _finish_plain(o_ref, acc_ref, n_maps)


def _window_attention(q, k_ctx, vt_ctx, k_lat, vt_lat, sink):
    bsz, t, qw = q.shape
    key_tiles, _, kw = k_lat.shape[1:]
    assert key_tiles == 1
    tq = Q_TILE
    n_maps = 4
    k_flat = k_lat.reshape(bsz, t, kw)
    wb = tq // WINDOW
    nwb = t // WINDOW
    heads = vt_lat.shape[1]
    whole = _whole_per_batch

    def vt_blk(width, col):
        return pl.BlockSpec((1, heads, V_ROWS, width), lambda b, i: (b, 0, 0, col(i)))

    def prev_blk(i):
        return jnp.maximum(i * wb - 1, 0)

    def next_blk(i):
        return jnp.minimum(i * wb + wb, nwb - 1)

    in_specs = [
        pl.BlockSpec((1, tq, qw), lambda b, i: (b, i, 0)),
        pl.BlockSpec((1, WINDOW, kw), lambda b, i: (b, prev_blk(i), 0)),
        pl.BlockSpec((1, tq, kw), lambda b, i: (b, i, 0)),
        pl.BlockSpec((1, WINDOW, kw), lambda b, i: (b, next_blk(i), 0)),
        vt_blk(WINDOW, prev_blk),
        vt_blk(tq, lambda i: i),
        vt_blk(WINDOW, next_blk),
        whole(k_ctx), whole(vt_ctx),
        pl.BlockSpec(memory_space=pltpu.SMEM),
        pl.BlockSpec(memory_space=pltpu.SMEM),
    ]
    stage_rows = max(k_ctx.shape[2], tq + 2 * WINDOW)
    return pl.pallas_call(
        functools.partial(_window_attn_kernel, n_maps=n_maps, seq=t),
        grid=(bsz, t // tq),
        in_specs=in_specs,
        out_specs=pl.BlockSpec((1, tq, 256), lambda b, i: (b, i, 0)),
        out_shape=jax.ShapeDtypeStruct((bsz, t, 256), BF16),
        scratch_shapes=_attn_scratch(n_maps, tq, stage_rows),
        compiler_params=_params("parallel", "parallel"),
        name="window_attention",
    )(q, k_flat, k_flat, k_flat, vt_lat, vt_lat, vt_lat, k_ctx, vt_ctx, jnp.zeros((1,), jnp.int32), sink)


def _outproj_kernel(x_ref, mod_ref, ya_ref, g3_ref, g3p_ref, g3n_ref, yw_ref, yg_ref, cw_ref, w_ref, g_ref, b_ref,
                    o_ref, *, alpha):
    i = pl.program_id(1)
    last = pl.num_programs(1) - 1
    tm = x_ref.shape[1]
    gb = g3_ref[0, :, 0:256]
    hid = g3_ref[0, :, 256:512] * g3_ref[0, :, 512:768]
    halo = g3p_ref.shape[1]
    h_prev = g3p_ref[0, halo - 1:halo, 256:512] * g3p_ref[0, halo - 1:halo, 512:768]
    h_next = g3n_ref[0, 0:1, 256:512] * g3n_ref[0, 0:1, 512:768]
    h_prev = jnp.where(i == 0, 0.0, h_prev)
    h_next = jnp.where(i == last, 0.0, h_next)
    row = lax.broadcasted_iota(jnp.int32, hid.shape, 0)
    below = jnp.where(row == 0, h_prev, pltpu.roll(hid, 1, axis=0))
    above = jnp.where(row == tm - 1, h_next, pltpu.roll(hid, tm - 1, axis=0))
    yb = gb * (cw_ref[0:1, :] * below + cw_ref[1:2, :] * hid + cw_ref[2:3, :] * above)
    y = jnp.dot(ya_ref[0], w_ref[0:256, :], preferred_element_type=F32)
    y += jnp.dot(yb.astype(BF16), w_ref[256:512, :], preferred_element_type=F32)
    y += jnp.dot(yw_ref[0], w_ref[512:768, :], preferred_element_type=F32)
    y += jnp.dot(yg_ref[0], w_ref[768:1024, :], preferred_element_type=F32)
    gate = mod_ref[0, 5:6, :]
    o_ref[0] = _layer_norm(alpha * x_ref[0] + gate * y, g_ref[...], b_ref[...])


def _out_projection(x, mods, mod_row, ya, g3, yw, yg, conv_w, w_out, ln_g, ln_b, alpha):
    bsz, t, d = x.shape
    tm = min(TOK_TILE, t)
    halo = 8
    hb = tm // halo
    n_halo = t // halo
    const = lambda b, i: (0, 0)
    y_spec = pl.BlockSpec((1, tm, 256), lambda b, i: (b, i, 0))
    return pl.pallas_call(
        functools.partial(_outproj_kernel, alpha=alpha),
        grid=(bsz, t // tm),
        in_specs=[
            pl.BlockSpec((1, tm, d), lambda b, i: (b, i, 0)),
            pl.BlockSpec((1, N_MOD, d), lambda b, i: (mod_row(b), 0, 0)),
            y_spec,
            pl.BlockSpec((1, tm, 768), lambda b, i: (b, i, 0)),
            pl.BlockSpec((1, halo, 768), lambda b, i: (b, jnp.maximum(i * hb - 1, 0), 0)),
            pl.BlockSpec((1, halo, 768), lambda b, i: (b, jnp.minimum(i * hb + hb, n_halo - 1), 0)),
            y_spec, y_spec,
            pl.BlockSpec(conv_w.shape, const),
            pl.BlockSpec(w_out.shape, const, pipeline_mode=pl.Buffered(1)),
            pl.BlockSpec((1, d), const),
            pl.BlockSpec((1, d), const),
        ],
        out_specs=pl.BlockSpec((1, tm, d), lambda b, i: (b, i, 0)),
        out_shape=jax.ShapeDtypeStruct(x.shape, F32),
        compiler_params=_params("parallel", "parallel"),
        name="out_projection",
    )(x, mods, ya, g3, g3, g3, yw, yg, conv_w, w_out, ln_g.reshape(1, d), ln_b.reshape(1, d))


def _rope_tables(rows, dim):
    n_freq = dim // 4
    inv_freq = ROPE_BASE ** (-jnp.arange(n_freq, dtype=F32) / n_freq)
    ang_r = jnp.arange(rows, dtype=F32)[:, None] * inv_freq
    ang_c = jnp.arange(GRID_W, dtype=F32)[:, None] * inv_freq

    def on_grid(of_row, of_col):
        r = jnp.broadcast_to(of_row[:, None, :], (rows, GRID_W, n_freq))
        c = jnp.broadcast_to(of_col[None, :, :], (rows, GRID_W, n_freq))
        return jnp.concatenate([r, r, c, c], axis=-1).reshape(rows * GRID_W, dim)

    cos = on_grid(jnp.cos(ang_r), jnp.cos(ang_c))
    sin = on_grid(jnp.sin(ang_r), jnp.sin(ang_c))
    reps = 128 // dim
    cos = jnp.tile(cos, (1, reps))
    sin = jnp.tile(sin, (1, reps))
    first_half = (jnp.arange(128) % (dim // 2)) < (dim // 4)
    sin_next = jnp.where(first_half, -sin, 0.0)
    sin_prev = jnp.where(first_half, 0.0, sin)
    return cos, sin_next, sin_prev


def _identity_tables(t):
    return jnp.ones((t, 128), F32), jnp.zeros((t, 128), F32), jnp.zeros((t, 128), F32)


def kernel(x, c, ctx, c_ctx, w_mod, b_mod, ln_g, ln_b, w_gu1, w_dn1, w_in, w_out, conv_w, lam_q1, lam_k1, lam_q2,
           lam_k2, subln_w, sink, qn_w, kn_w, w_gu2, w_dn2):
    bsz, seq, d = x.shape
    ctx_len = ctx.shape[1]
    depth = w_mod.shape[0]
    alpha = (2.0 * depth) ** 0.25
    assert seq % TOK_TILE == 0 and ctx_len % Q_TILE == 0 and bsz < MOD_ROWS

    tabs_a = _rope_tables(seq // GRID_W, A_QK_DIM)
    tabs_h = _rope_tables(seq // GRID_W, HEAD_DIM)
    tabs_id = _identity_tables(ctx_len)
    gsum = (jnp.arange(256)[:, None] // HEAD_DIM == jnp.arange(256)[None, :] // HEAD_DIM).astype(BF16)

    cvec = jnp.zeros((MOD_ROWS, d), F32).at[:bsz].set(c).at[bsz].set(c_ctx)
    mods_all = _mod_vectors(cvec, w_mod, b_mod).reshape(depth, MOD_ROWS, N_MOD, d)
    lat_row = lambda b: b
    ctx_row = lambda b: bsz

    xc = ctx
    for l in range(depth):
        need_ctx = l < depth - 1
        lambda_init = 0.8 - 0.6 * math.exp(-0.3 * l)
        mods = mods_all[l]
        wgu1, wdn1 = w_gu1[l].astype(BF16), w_dn1[l].astype(BF16)
        wgu2, wdn2 = w_gu2[l].astype(BF16), w_dn2[l].astype(BF16)
        win, wout = w_in[l].astype(BF16), w_out[l].astype(BF16)
        qnw = jnp.tile(qn_w[l], 256 // HEAD_DIM).reshape(1, 256)
        knw = jnp.tile(kn_w[l], 128 // HEAD_DIM).reshape(1, 128)
        lamp = jnp.stack([lam_q1[l], lam_k1[l], lam_q2[l], lam_k2[l]])
        subw = subln_w[l].reshape(HEAD_DIM, 1)

        x = _ffn_step(x, mods, lat_row, 0, wgu1, wdn1, ln_g[l, 0], ln_b[l, 0], alpha)
        xc = _ffn_step(xc, mods, ctx_row, 0, wgu1, wdn1, ln_g[l, 0], ln_b[l, 0], alpha)

        qa, ka, vat, g3, qc, kc, vct, qd, kd, vdt = _in_projection(x, mods, lat_row, win, tabs_a, tabs_h, qnw, knw, gsum)
        (qa_c, ka_c, vat_c, g3_c, qc_c, kc_c, vct_c, qd_c, kd_c, vdt_c) = _in_projection(
            xc, mods, ctx_row, win, tabs_id, tabs_id, qnw, knw, gsum)
        ya = _dense_attention(qa, ka_c, vat_c, ka, vat, n_maps=8, dq=A_QK_DIM, lamp=lamp, subw=subw,
                              diff_lambda_init=lambda_init)
        yw = _window_attention(qc, kc_c, vct_c, kc, vct, sink[l])
        yg = _dense_attention(qd, kd_c, vdt_c, kd, vdt, n_maps=4, dq=HEAD_DIM)
        x = _out_projection(x, mods, lat_row, ya, g3, yw, yg, conv_w[l], wout, ln_g[l, 1], ln_b[l, 1], alpha)

        x = _ffn_step(x, mods, lat_row, 6, wgu2, wdn2, ln_g[l, 2], ln_b[l, 2], alpha)

        if need_ctx:
            ya_c = _dense_attention(qa_c, ka_c, vat_c, None, None, n_maps=8, dq=A_QK_DIM, lamp=lamp, subw=subw,
                                    diff_lambda_init=lambda_init)
            yw_c = _dense_attention(qc_c, kc_c, vct_c, None, None, n_maps=4, dq=HEAD_DIM, sink=sink[l])
            yg_c = _dense_attention(qd_c, kd_c, vdt_c, None, None, n_maps=4, dq=HEAD_DIM)
            xc = _out_projection(xc, mods, ctx_row, ya_c, g3_c, yw_c, yg_c, conv_w[l], wout, ln_g[l, 1], ln_b[l, 1],
                                 alpha)
            xc = _ffn_step(xc, mods, ctx_row, 6, wgu2, wdn2, ln_g[l, 2], ln_b[l, 2], alpha)
    return x
```

```python
import functools
import math

import jax
import jax.numpy as jnp
from jax import lax
from jax.experimental import pallas as pl
from jax.experimental.pallas import tpu as pltpu

F32 = jnp.float32
BF16 = jnp.bfloat16

GRID_W = 64
HEAD_DIM = 64
WINDOW = 128
ROPE_BASE = 10000.0
EPS = 1e-6
NEG = -1e30
A_QK_DIM = 32
N_MOD = 9
MOD_ROWS = 8
VMEM_LIMIT = 56 * 1024 * 1024

OFF_AQ, OFF_AK, OFF_AV = 0, 256, 512
OFF_B = 768
OFF_CQ, OFF_CK, OFF_CV = 1536, 1792, 1920
OFF_DQ, OFF_DK, OFF_DV = 2048, 2304, 2432
IN_W = 2560

TOK_TILE = 512
FFN_TILE = 512
Q_TILE = 256
DENSE_Q_TILE = 256
FF_CHUNK = 256
SCORE_ROWS = 128
PV_ROWS = 256
ATTN_CHUNK = 1024
STAGES = 4
LOOKAHEAD = 2
ROUNDS_PER_ITER = 8
K_TILE = 128
ONES_ROWS = 16
V_ROWS = HEAD_DIM + ONES_ROWS
LOG2E = math.log2(math.e)


def _params(*sem):
    return pltpu.CompilerParams(dimension_semantics=sem, vmem_limit_bytes=VMEM_LIMIT)


def _layer_norm(r, g, b):
    mu = jnp.mean(r, axis=-1, keepdims=True)
    d = r - mu
    var = jnp.mean(d * d, axis=-1, keepdims=True)
    return d * lax.rsqrt(var + EPS) * g + b


def _mod_kernel(c_ref, w_ref, b_ref, o_ref):
    c = c_ref[...]
    a = (c * jax.nn.sigmoid(c)).astype(BF16)
    o_ref[0] = jnp.dot(a, w_ref[0].astype(BF16), preferred_element_type=F32) + b_ref[0]


def _mod_vectors(cvec, w_mod, b_mod):
    depth, d, n = w_mod.shape
    tn = n // 8
    return pl.pallas_call(
        _mod_kernel,
        grid=(depth, n // tn),
        in_specs=[
            pl.BlockSpec((MOD_ROWS, d), lambda l, j: (0, 0)),
            pl.BlockSpec((1, d, tn), lambda l, j: (l, 0, j)),
            pl.BlockSpec((1, 1, tn), lambda l, j: (l, 0, j)),
        ],
        out_specs=pl.BlockSpec((1, MOD_ROWS, tn), lambda l, j: (l, 0, j)),
        out_shape=jax.ShapeDtypeStruct((depth, MOD_ROWS, n), F32),
        compiler_params=_params("parallel", "parallel"),
        name="mod_vectors",
    )(cvec, w_mod, b_mod.reshape(depth, 1, n))


def _ffn_kernel(x_ref, mod_ref, wgu_ref, wdn_ref, g_ref, b_ref, o_ref, acc_ref, *, mod_base, alpha):
    x = x_ref[0]
    shift = mod_ref[0, mod_base:mod_base + 1, :]
    scale = mod_ref[0, mod_base + 1:mod_base + 2, :]
    gate = mod_ref[0, mod_base + 2:mod_base + 3, :]
    h = (x * (1.0 + scale) + shift).astype(BF16)
    d_ff = wdn_ref.shape[0]
    for c in range(d_ff // FF_CHUNK):
        lo = c * FF_CHUNK
        g = jnp.dot(h, wgu_ref[:, lo:lo + FF_CHUNK], preferred_element_type=F32)
        u = jnp.dot(h, wgu_ref[:, d_ff + lo:d_ff + lo + FF_CHUNK], preferred_element_type=F32)
        a = (g * jax.nn.sigmoid(g) * u).astype(BF16)
        y = jnp.dot(a, wdn_ref[lo:lo + FF_CHUNK, :], preferred_element_type=F32)
        if c == 0:
            acc_ref[...] = y
        else:
            acc_ref[...] += y
    r = alpha * x + (0.5 * gate) * acc_ref[...]
    o_ref[0] = _layer_norm(r, g_ref[...], b_ref[...])


def _ffn_step(x, mods, mod_row, mod_base, w_gu, w_dn, layer, ln_g, ln_b, alpha):
    bsz, t, d = x.shape
    d_ff = w_dn.shape[1]
    tm = min(FFN_TILE, t)
    const = lambda b, i: (0, 0)
    return pl.pallas_call(
        functools.partial(_ffn_kernel, mod_base=mod_base, alpha=alpha),
        grid=(bsz, t // tm),
        in_specs=[
            pl.BlockSpec((1, tm, d), lambda b, i: (b, i, 0)),
            pl.BlockSpec((None, 1, N_MOD, d), lambda b, i: (layer, mod_row(b), 0, 0)),
            pl.BlockSpec((None, d, 2 * d_ff), lambda b, i: (layer, 0, 0), pipeline_mode=pl.Buffered(1)),
            pl.BlockSpec((None, d_ff, d), lambda b, i: (layer, 0, 0), pipeline_mode=pl.Buffered(1)),
            pl.BlockSpec((1, d), const),
            pl.BlockSpec((1, d), const),
        ],
        out_specs=pl.BlockSpec((1, tm, d), lambda b, i: (b, i, 0)),
        out_shape=jax.ShapeDtypeStruct(x.shape, F32),
        scratch_shapes=[pltpu.VMEM((tm, d), F32)],
        compiler_params=_params("parallel", "parallel"),
        name="ffn_step",
    )(x, mods, w_gu, w_dn, ln_g.reshape(1, d), ln_b.reshape(1, d))


def _rope(x, cos, sin_next, sin_prev, quarter):
    width = x.shape[1]
    reps = width // cos.shape[1]
    if reps > 1:
        cos = jnp.concatenate([cos] * reps, axis=1)
        sin_next = jnp.concatenate([sin_next] * reps, axis=1)
        sin_prev = jnp.concatenate([sin_prev] * reps, axis=1)
    x_next = pltpu.roll(x, width - quarter, axis=1)
    x_prev = pltpu.roll(x, quarter, axis=1)
    return x * cos + x_next * sin_next + x_prev * sin_prev


def _head_rms(x, gsum_ref, w):
    x2 = x * x
    hi = x2.astype(BF16)
    lo = (x2 - hi.astype(F32)).astype(BF16)
    gs = gsum_ref[:x.shape[1], :x.shape[1]]
    ss = jnp.dot(hi, gs, preferred_element_type=F32) + jnp.dot(lo, gs, preferred_element_type=F32)
    return x * lax.rsqrt(ss * (1.0 / HEAD_DIM) + EPS) * w


def _store_k(k_ref, k):
    for kt in range(k.shape[1] // K_TILE):
        k_ref[0, kt] = k[:, kt * K_TILE:(kt + 1) * K_TILE].astype(BF16)


def _store_vt(vt_ref, v):
    vt = v.T
    tm = v.shape[0]
    for hd in range(v.shape[1] // HEAD_DIM):
        vt_ref[0, hd, 0:HEAD_DIM, :] = vt[hd * HEAD_DIM:(hd + 1) * HEAD_DIM, :].astype(BF16)
        vt_ref[0, hd, HEAD_DIM:V_ROWS, :] = jnp.ones((ONES_ROWS, tm), BF16)


def _inproj_kernel(x_ref, mod_ref, w_ref, ca_ref, san_ref, sap_ref, ch_ref, shn_ref, shp_ref,
                   qnw_ref, knw_ref, gsum_ref,
                   qa_ref, ka_ref, vat_ref, g3_ref, qc_ref, kc_ref, vct_ref, qd_ref, kd_ref, vdt_ref):
    x = x_ref[0]
    shift = mod_ref[0, 3:4, :]
    scale = mod_ref[0, 4:5, :]
    h = (x * (1.0 + scale) + shift).astype(BF16)

    def seg(lo, hi):
        return jnp.dot(h, w_ref[:, lo:hi], preferred_element_type=F32)

    rope_a = functools.partial(_rope, cos=ca_ref[...], sin_next=san_ref[...], sin_prev=sap_ref[...],
                               quarter=A_QK_DIM // 4)
    rope_h = functools.partial(_rope, cos=ch_ref[...], sin_next=shn_ref[...], sin_prev=shp_ref[...],
                               quarter=HEAD_DIM // 4)
    qa_ref[0] = (rope_a(seg(OFF_AQ, OFF_AK)) * (LOG2E * A_QK_DIM ** -0.5)).astype(BF16)
    _store_k(ka_ref, rope_a(seg(OFF_AK, OFF_AV)))
    _store_vt(vat_ref, seg(OFF_AV, OFF_B))
    g3_ref[0] = seg(OFF_B, OFF_CQ)
    qc_ref[0] = (rope_h(seg(OFF_CQ, OFF_CK)) * (LOG2E * HEAD_DIM ** -0.5)).astype(BF16)
    kv = seg(OFF_CK, OFF_DQ)
    _store_k(kc_ref, rope_h(kv[:, :OFF_CV - OFF_CK]))
    _store_vt(vct_ref, kv[:, OFF_CV - OFF_CK:])
    qd = _head_rms(seg(OFF_DQ, OFF_DK), gsum_ref, qnw_ref[...])
    qd_ref[0] = (rope_h(qd) * (LOG2E * HEAD_DIM ** -0.5)).astype(BF16)
    kv = seg(OFF_DK, IN_W)
    kd = _head_rms(kv[:, :OFF_DV - OFF_DK], gsum_ref, knw_ref[...])
    _store_k(kd_ref, rope_h(kd))
    _store_vt(vdt_ref, kv[:, OFF_DV - OFF_DK:])


def _in_projection(x, mods, mod_row, w_in, layer, tabs_a, tabs_h, qnw, knw, gsum):
    bsz, t, d = x.shape
    tm = min(TOK_TILE, t)
    nc = t // tm
    const = lambda b, i: (0, 0)
    tab = pl.BlockSpec((tm, 128), lambda b, i: (i, 0))
    q_spec = pl.BlockSpec((1, tm, 256), lambda b, i: (b, i, 0))

    def k_spec(w):
        return pl.BlockSpec((1, w // K_TILE, tm, K_TILE), lambda b, i: (b, 0, i, 0))

    def vt_spec(w):
        return pl.BlockSpec((1, w // HEAD_DIM, V_ROWS, tm), lambda b, i: (b, 0, 0, i))

    def q_shape():
        return jax.ShapeDtypeStruct((bsz, t, 256), BF16)

    def k_shape(w):
        return jax.ShapeDtypeStruct((bsz, w // K_TILE, t, K_TILE), BF16)

    def vt_shape(w):
        return jax.ShapeDtypeStruct((bsz, w // HEAD_DIM, V_ROWS, t), BF16)

    return pl.pallas_call(
        _inproj_kernel,
        grid=(bsz, nc),
        in_specs=[
            pl.BlockSpec((1, tm, d), lambda b, i: (b, i, 0)),
            pl.BlockSpec((None, 1, N_MOD, d), lambda b, i: (layer, mod_row(b), 0, 0)),
            pl.BlockSpec((None, d, IN_W), lambda b, i: (layer, 0, 0), pipeline_mode=pl.Buffered(1)),
            tab, tab, tab, tab, tab, tab,
            pl.BlockSpec((1, 256), const),
            pl.BlockSpec((1, 128), const),
            pl.BlockSpec((256, 256), const),
        ],
        out_specs=[q_spec, k_spec(256), vt_spec(256),
                   pl.BlockSpec((1, tm, 768), lambda b, i: (b, i, 0)),
                   q_spec, k_spec(128), vt_spec(128),
                   q_spec, k_spec(128), vt_spec(128)],
        out_shape=[q_shape(), k_shape(256), vt_shape(256),
                   jax.ShapeDtypeStruct((bsz, t, 768), F32),
                   q_shape(), k_shape(128), vt_shape(128),
                   q_shape(), k_shape(128), vt_shape(128)],
        compiler_params=_params("parallel", "parallel"),
        name="in_projection",
    )(x, mods, w_in, *tabs_a, *tabs_h, qnw, knw, gsum)


def _whole_per_batch(arr):
    zeros = (0,) * (arr.ndim - 1)
    return pl.BlockSpec((1,) + arr.shape[1:], lambda b, i: (b,) + zeros, pipeline_mode=pl.Buffered(1))


def _attn_scratch(n_maps, tq, stage_rows):
    shapes = ([((n_maps, K_TILE, tq), BF16), ((n_maps, V_ROWS, tq), F32), ((n_maps, 8, tq), F32)]
              + [((stage_rows, tq), F32)] * STAGES + [((8, tq), F32)] * STAGES)
    return [pltpu.VMEM(shape, dtype) for shape, dtype in shapes]


def _key_offset(j, dq, grouped):
    return (j // 2) * dq if grouped else j * dq


def _fill_qpad(q_ref, qpad_ref, n_maps, dq, grouped):
    tq = q_ref.shape[1]
    qt = q_ref[0].astype(F32).T
    for j in range(n_maps):
        row = _key_offset(j, dq, grouped) % K_TILE
        parts = []
        if row:
            parts.append(jnp.zeros((row, tq), F32))
        parts.append(qt[j * dq:(j + 1) * dq, :])
        if K_TILE - row - dq:
            parts.append(jnp.zeros((K_TILE - row - dq, tq), F32))
        qpad_ref[j] = jnp.concatenate(parts, axis=0).astype(BF16)


def _score_unit(keys, j, nk, qpad_ref, s_ref, bm_ref, valid=None):
    tq = s_ref.shape[1]
    w = qpad_ref[j]
    bm = None
    for r in range(nk // SCORE_ROWS):
        rows = slice(r * SCORE_ROWS, (r + 1) * SCORE_ROWS)
        s = jnp.dot(keys(rows), w, preferred_element_type=F32)
        if valid is not None:
            s = jnp.where(valid(rows), s, NEG)
        s_ref[rows, :] = s
        part = jnp.max(s.reshape(SCORE_ROWS // 8, 8, tq), axis=0)
        bm = part if bm is None else jnp.maximum(bm, part)
    bm_ref[...] = bm


def _softmax_unit(values, j, nk, s_ref, bm_ref, m_ref, acc_ref, row0):
    m_old = m_ref[j, 0:1, :]
    m_new = jnp.maximum(m_old, jnp.max(bm_ref[...], axis=0, keepdims=True))
    alpha = jnp.exp2(m_old - m_new)
    pv = None
    for r in range(nk // PV_ROWS):
        rows = pl.ds(pl.multiple_of(row0 + r * PV_ROWS, PV_ROWS), PV_ROWS)
        p = jnp.exp2(s_ref[rows, :] - m_new)
        d = jnp.dot(values(slice(r * PV_ROWS, (r + 1) * PV_ROWS)), p.astype(BF16), preferred_element_type=F32)
        pv = d if pv is None else pv + d
    acc_ref[j] = alpha * acc_ref[j] + pv
    m_ref[j] = jnp.broadcast_to(m_new, m_ref.shape[1:])


def _init_state(m_ref, acc_ref, n_maps, sink_ref):
    tq = m_ref.shape[-1]
    for j in range(n_maps):
        if sink_ref is None:
            m_ref[j] = jnp.full(m_ref.shape[1:], NEG, F32)
            acc_ref[j] = jnp.zeros((V_ROWS, tq), F32)
        else:
            m_ref[j] = jnp.full(m_ref.shape[1:], sink_ref[j] * LOG2E, F32)
            acc_ref[j] = jnp.concatenate([jnp.zeros((HEAD_DIM, tq), F32), jnp.ones((ONES_ROWS, tq), F32)], axis=0)


def _normalised(acc_ref, j):
    acc = acc_ref[j]
    return acc[0:HEAD_DIM, :] / acc[HEAD_DIM:HEAD_DIM + 1, :]


def _finish_plain(o_ref, acc_ref, n_maps):
    ot = jnp.concatenate([_normalised(acc_ref, j) for j in range(n_maps)], axis=0)
    o_ref[0] = ot.T.astype(BF16)


def _finish_diff(o_ref, acc_ref, lamp_ref, subw_ref, n_maps, lambda_init):
    lp = lamp_ref[...]
    lam = (jnp.exp(jnp.sum(lp[0:1] * lp[1:2], axis=1, keepdims=True))
           - jnp.exp(jnp.sum(lp[2:3] * lp[3:4], axis=1, keepdims=True)) + lambda_init)
    outs = []
    for hd in range(n_maps // 2):
        o = _normalised(acc_ref, 2 * hd) - lam * _normalised(acc_ref, 2 * hd + 1)
        ms = jnp.mean(o * o, axis=0, keepdims=True)
        outs.append(o * lax.rsqrt(ms + EPS) * subw_ref[...] * (1.0 - lambda_init))
    o_ref[0] = jnp.concatenate(outs, axis=0).T.astype(BF16)


def _dense_attn_kernel(*refs, n_maps, dq, grouped, has_latent, has_sink, diff_lambda_init):
    refs = list(refs)
    q_ref, kc_ref, vct_ref, zero_ref = refs[:4]
    pos = 4
    k_ref = vt_ref = sink_ref = lamp_ref = subw_ref = None
    if has_latent:
        k_ref, vt_ref = refs[pos:pos + 2]
        pos += 2
    if has_sink:
        sink_ref = refs[pos]
        pos += 1
    if diff_lambda_init is not None:
        lamp_ref, subw_ref = refs[pos:pos + 2]
        pos += 2
    o_ref, qpad_ref, acc_ref, m_ref = refs[pos:pos + 4]
    s_refs = refs[pos + 4:pos + 4 + STAGES]
    bm_refs = refs[pos + 4 + STAGES:pos + 4 + 2 * STAGES]

    _fill_qpad(q_ref, qpad_ref, n_maps, dq, grouped)
    _init_state(m_ref, acc_ref, n_maps, sink_ref)
    n_ctx = kc_ref.shape[2]
    tk = k_ref.shape[3] if has_latent else 0
    n_lat = n_maps * k_ref.shape[2] if has_latent else 0
    row0 = zero_ref[0]

    def score(v, slot):
        if isinstance(v, int) and v < n_maps:
            kt = _key_offset(v, dq, grouped) // K_TILE
            _score_unit(lambda rows: kc_ref[0, kt, rows, :], v, n_ctx, qpad_ref, s_refs[slot], bm_refs[slot])
        else:
            u = v - n_maps
            j, c = u % n_maps, u // n_maps
            kt = _key_offset(j, dq, grouped) // K_TILE
            _score_unit(lambda rows: k_ref[0, kt, c, rows, :], j, tk, qpad_ref, s_refs[slot], bm_refs[slot])

    def softmax(v, slot):
        if isinstance(v, int) and v < n_maps:
            _softmax_unit(lambda cols: vct_ref[0, v // 2, :, cols], v, n_ctx, s_refs[slot], bm_refs[slot],
                          m_ref, acc_ref, row0)
        else:
            u = v - n_maps
            j, c = u % n_maps, u // n_maps
            _softmax_unit(lambda cols: vt_ref[0, c, j // 2, :, cols], j, tk, s_refs[slot], bm_refs[slot],
                          m_ref, acc_ref, row0)

    n_units = n_maps + n_lat
    for v in range(LOOKAHEAD):
        score(v, v % STAGES)
    for v in range(n_maps):
        if v + LOOKAHEAD < n_units:
            score(v + LOOKAHEAD, (v + LOOKAHEAD) % STAGES)
        softmax(v, v % STAGES)

    if has_latent:
        per_iter = STAGES * math.gcd(n_lat // STAGES, ROUNDS_PER_ITER)

        def body(i, carry):
            v0 = n_maps + per_iter * i
            for k in range(per_iter):
                score(jnp.minimum(v0 + k + LOOKAHEAD, n_units - 1), (k + LOOKAHEAD) % STAGES)
                softmax(v0 + k, k % STAGES)
            return carry
        lax.fori_loop(0, n_lat // per_iter, body, 0)
    if diff_lambda_init is None:
        _finish_plain(o_ref, acc_ref, n_maps)
    else:
        _finish_diff(o_ref, acc_ref, lamp_ref, subw_ref, n_maps, diff_lambda_init)


def _dense_attention(q, k_ctx, vt_ctx, k_lat, vt_lat, *, n_maps, dq, sink=None, lamp=None, subw=None,
                     diff_lambda_init=None):
    bsz, t, qw = q.shape
    key_tiles, n_ctx = k_ctx.shape[1:3]
    heads = vt_ctx.shape[1]
    grouped = key_tiles * K_TILE != n_maps * dq
    tq = min(DENSE_Q_TILE, t)
    has_latent = k_lat is not None
    whole = _whole_per_batch
    assert n_maps % STAGES == 0 and n_ctx % PV_ROWS == 0

    in_specs = [pl.BlockSpec((1, tq, qw), lambda b, i: (b, i, 0)), whole(k_ctx), whole(vt_ctx),
                pl.BlockSpec(memory_space=pltpu.SMEM)]
    args = [q, k_ctx, vt_ctx, jnp.zeros((1,), jnp.int32)]
    stage_rows = n_ctx
    if has_latent:
        n_keys = k_lat.shape[2]
        tk = math.gcd(ATTN_CHUNK, n_keys)
        n_chunks = n_keys // tk
        assert tk % PV_ROWS == 0 and tk % SCORE_ROWS == 0
        k_lat = k_lat.reshape(bsz, key_tiles, n_chunks, tk, K_TILE)
        vt_lat = vt_lat.reshape(bsz, heads, V_ROWS, n_chunks, tk).transpose(0, 3, 1, 2, 4)
        in_specs += [whole(k_lat), whole(vt_lat)]
        args += [k_lat, vt_lat]
        stage_rows = max(n_ctx, tk)
    if sink is not None:
        in_specs.append(pl.BlockSpec(memory_space=pltpu.SMEM))
        args.append(sink)
    if diff_lambda_init is not None:
        in_specs += [pl.BlockSpec(lamp.shape, lambda b, i: (0, 0)), pl.BlockSpec(subw.shape, lambda b, i: (0, 0))]
        args += [lamp, subw]
    scratch = _attn_scratch(n_maps, tq, stage_rows)
    return pl.pallas_call(
        functools.partial(_dense_attn_kernel, n_maps=n_maps, dq=dq, grouped=grouped, has_latent=has_latent,
                          has_sink=sink is not None, diff_lambda_init=diff_lambda_init),
        grid=(bsz, t // tq),
        in_specs=in_specs,
        out_specs=pl.BlockSpec((1, tq, 256), lambda b, i: (b, i, 0)),
        out_shape=jax.ShapeDtypeStruct((bsz, t, 256), BF16),
        scratch_shapes=scratch,
        compiler_params=_params("parallel", "parallel"),
        name="dense_attention",
    )(*args)


def _window_attn_kernel(q_ref, kp_ref, kc_ref, kn_ref, vtp_ref, vtc_ref, vtn_ref, kctx_ref, vtctx_ref, zero_ref,
                        sink_ref, o_ref, qpad_ref, acc_ref, m_ref, *stage_refs, n_maps, seq):
    s_refs, bm_refs = stage_refs[:STAGES], stage_refs[STAGES:]
    tq = q_ref.shape[1]
    q0 = pl.program_id(1) * tq
    _fill_qpad(q_ref, qpad_ref, n_maps, HEAD_DIM, True)
    _init_state(m_ref, acc_ref, n_maps, sink_ref)
    k_win = jnp.concatenate([kp_ref[0], kc_ref[0], kn_ref[0]], axis=0)
    vt_win = jnp.concatenate([vtp_ref[0], vtc_ref[0], vtn_ref[0]], axis=2)
    n_ctx, n_win = kctx_ref.shape[2], k_win.shape[0]
    row0 = zero_ref[0]

    def in_window(rows):
        kpos = q0 - WINDOW + rows.start + lax.broadcasted_iota(jnp.int32, (rows.stop - rows.start, tq), 0)
        qpos = q0 + lax.broadcasted_iota(jnp.int32, (rows.stop - rows.start, tq), 1)
        return (jnp.abs(kpos - qpos) <= WINDOW) & (kpos >= 0) & (kpos < seq)

    def score(v, slot):
        if v < n_maps:
            _score_unit(lambda rows: kctx_ref[0, 0, rows, :], v, n_ctx, qpad_ref, s_refs[slot], bm_refs[slot])
        else:
            _score_unit(lambda rows: k_win[rows, :], v - n_maps, n_win, qpad_ref, s_refs[slot], bm_refs[slot],
                        valid=in_window)

    def softmax(v, slot):
        if v < n_maps:
            _softmax_unit(lambda cols: vtctx_ref[0, v // 2, :, cols], v, n_ctx, s_refs[slot], bm_refs[slot],
                          m_ref, acc_ref, row0)
        else:
            j = v - n_maps
            _softmax_unit(lambda cols: vt_win[j // 2][:, cols], j, n_win, s_refs[slot], bm_refs[slot],
                          m_ref, acc_ref, row0)

    for v in range(LOOKAHEAD):
        score(v, v % STAGES)
    for v in range(2 * n_maps):
        if v + LOOKAHEAD < 2 * n_maps:
            score(v + LOOKAHEAD, (v + LOOKAHEAD) % STAGES)
        softmax(v, v % STAGES)
    _finish_plain(o_ref, acc_ref, n_maps)


def _window_attention(q, k_ctx, vt_ctx, k_lat, vt_lat, sink):
    bsz, t, qw = q.shape
    key_tiles, _, kw = k_lat.shape[1:]
    assert key_tiles == 1
    tq = Q_TILE
    n_maps = 4
    k_flat = k_lat.reshape(bsz, t, kw)
    wb = tq // WINDOW
    nwb = t // WINDOW
    heads = vt_lat.shape[1]
    whole = _whole_per_batch

    def vt_blk(width, col):
        return pl.BlockSpec((1, heads, V_ROWS, width), lambda b, i: (b, 0, 0, col(i)))

    def prev_blk(i):
        return jnp.maximum(i * wb - 1, 0)

    def next_blk(i):
        return jnp.minimum(i * wb + wb, nwb - 1)

    in_specs = [
        pl.BlockSpec((1, tq, qw), lambda b, i: (b, i, 0)),
        pl.BlockSpec((1, WINDOW, kw), lambda b, i: (b, prev_blk(i), 0)),
        pl.BlockSpec((1, tq, kw), lambda b, i: (b, i, 0)),
        pl.BlockSpec((1, WINDOW, kw), lambda b, i: (b, next_blk(i), 0)),
        vt_blk(WINDOW, prev_blk),
        vt_blk(tq, lambda i: i),
        vt_blk(WINDOW, next_blk),
        whole(k_ctx), whole(vt_ctx),
        pl.BlockSpec(memory_space=pltpu.SMEM),
        pl.BlockSpec(memory_space=pltpu.SMEM),
    ]
    stage_rows = max(k_ctx.shape[2], tq + 2 * WINDOW)
    return pl.pallas_call(
        functools.partial(_window_attn_kernel, n_maps=n_maps, seq=t),
        grid=(bsz, t // tq),
        in_specs=in_specs,
        out_specs=pl.BlockSpec((1, tq, 256), lambda b, i: (b, i, 0)),
        out_shape=jax.ShapeDtypeStruct((bsz, t, 256), BF16),
        scratch_shapes=_attn_scratch(n_maps, tq, stage_rows),
        compiler_params=_params("parallel", "parallel"),
        name="window_attention",
    )(q, k_flat, k_flat, k_flat, vt_lat, vt_lat, vt_lat, k_ctx, vt_ctx, jnp.zeros((1,), jnp.int32), sink)


def _outproj_kernel(x_ref, mod_ref, ya_ref, g3_ref, g3p_ref, g3n_ref, yw_ref, yg_ref, cw_ref, w_ref, g_ref, b_ref,
                    o_ref, *, alpha):
    i = pl.program_id(1)
    last = pl.num_programs(1) - 1
    tm = x_ref.shape[1]
    gb = g3_ref[0, :, 0:256]
    hid = g3_ref[0, :, 256:512] * g3_ref[0, :, 512:768]
    halo = g3p_ref.shape[1]
    h_prev = g3p_ref[0, halo - 1:halo, 256:512] * g3p_ref[0, halo - 1:halo, 512:768]
    h_next = g3n_ref[0, 0:1, 256:512] * g3n_ref[0, 0:1, 512:768]
    h_prev = jnp.where(i == 0, 0.0, h_prev)
    h_next = jnp.where(i == last, 0.0, h_next)
    row = lax.broadcasted_iota(jnp.int32, hid.shape, 0)
    below = jnp.where(row == 0, h_prev, pltpu.roll(hid, 1, axis=0))
    above = jnp.where(row == tm - 1, h_next, pltpu.roll(hid, tm - 1, axis=0))
    yb = gb * (cw_ref[0:1, :] * below + cw_ref[1:2, :] * hid + cw_ref[2:3, :] * above)
    y = jnp.dot(ya_ref[0], w_ref[0:256, :], preferred_element_type=F32)
    y += jnp.dot(yb.astype(BF16), w_ref[256:512, :], preferred_element_type=F32)
    y += jnp.dot(yw_ref[0], w_ref[512:768, :], preferred_element_type=F32)
    y += jnp.dot(yg_ref[0], w_ref[768:1024, :], preferred_element_type=F32)
    gate = mod_ref[0, 5:6, :]
    o_ref[0] = _layer_norm(alpha * x_ref[0] + gate * y, g_ref[...], b_ref[...])


def _out_projection(x, mods, mod_row, ya, g3, yw, yg, conv_w, w_out, layer, ln_g, ln_b, alpha):
    bsz, t, d = x.shape
    tm = min(TOK_TILE, t)
    halo = 8
    hb = tm // halo
    n_halo = t // halo
    const = lambda b, i: (0, 0)
    y_spec = pl.BlockSpec((1, tm, 256), lambda b, i: (b, i, 0))
    return pl.pallas_call(
        functools.partial(_outproj_kernel, alpha=alpha),
        grid=(bsz, t // tm),
        in_specs=[
            pl.BlockSpec((1, tm, d), lambda b, i: (b, i, 0)),
            pl.BlockSpec((None, 1, N_MOD, d), lambda b, i: (layer, mod_row(b), 0, 0)),
            y_spec,
            pl.BlockSpec((1, tm, 768), lambda b, i: (b, i, 0)),
            pl.BlockSpec((1, halo, 768), lambda b, i: (b, jnp.maximum(i * hb - 1, 0), 0)),
            pl.BlockSpec((1, halo, 768), lambda b, i: (b, jnp.minimum(i * hb + hb, n_halo - 1), 0)),
            y_spec, y_spec,
            pl.BlockSpec(conv_w.shape, const),
            pl.BlockSpec((None,) + w_out.shape[1:], lambda b, i: (layer, 0, 0), pipeline_mode=pl.Buffered(1)),
            pl.BlockSpec((1, d), const),
            pl.BlockSpec((1, d), const),
        ],
        out_specs=pl.BlockSpec((1, tm, d), lambda b, i: (b, i, 0)),
        out_shape=jax.ShapeDtypeStruct(x.shape, F32),
        compiler_params=_params("parallel", "parallel"),
        name="out_projection",
    )(x, mods, ya, g3, g3, g3, yw, yg, conv_w, w_out, ln_g.reshape(1, d), ln_b.reshape(1, d))


def _rope_tables(rows, dim):
    n_freq = dim // 4
    inv_freq = ROPE_BASE ** (-jnp.arange(n_freq, dtype=F32) / n_freq)
    ang_r = jnp.arange(rows, dtype=F32)[:, None] * inv_freq
    ang_c = jnp.arange(GRID_W, dtype=F32)[:, None] * inv_freq

    def on_grid(of_row, of_col):
        r = jnp.broadcast_to(of_row[:, None, :], (rows, GRID_W, n_freq))
        c = jnp.broadcast_to(of_col[None, :, :], (rows, GRID_W, n_freq))
        return jnp.concatenate([r, r, c, c], axis=-1).reshape(rows * GRID_W, dim)

    cos = on_grid(jnp.cos(ang_r), jnp.cos(ang_c))
    sin = on_grid(jnp.sin(ang_r), jnp.sin(ang_c))
    reps = 128 // dim
    cos = jnp.tile(cos, (1, reps))
    sin = jnp.tile(sin, (1, reps))
    first_half = (jnp.arange(128) % (dim // 2)) < (dim // 4)
    sin_next = jnp.where(first_half, -sin, 0.0)
    sin_prev = jnp.where(first_half, 0.0, sin)
    return cos, sin_next, sin_prev


def _identity_tables(t):
    return jnp.ones((t, 128), F32), jnp.zeros((t, 128), F32), jnp.zeros((t, 128), F32)


def kernel(x, c, ctx, c_ctx, w_mod, b_mod, ln_g, ln_b, w_gu1, w_dn1, w_in, w_out, conv_w, lam_q1, lam_k1, lam_q2,
           lam_k2, subln_w, sink, qn_w, kn_w, w_gu2, w_dn2):
    bsz, seq, d = x.shape
    ctx_len = ctx.shape[1]
    depth = w_mod.shape[0]
    alpha = (2.0 * depth) ** 0.25
    assert seq % TOK_TILE == 0 and ctx_len % Q_TILE == 0 and bsz < MOD_ROWS

    tabs_a = _rope_tables(seq // GRID_W, A_QK_DIM)
    tabs_h = _rope_tables(seq // GRID_W, HEAD_DIM)
    tabs_id = _identity_tables(ctx_len)
    gsum = (jnp.arange(256)[:, None] // HEAD_DIM == jnp.arange(256)[None, :] // HEAD_DIM).astype(BF16)

    cvec = jnp.zeros((MOD_ROWS, d), F32).at[:bsz].set(c).at[bsz].set(c_ctx)
    mods_all = _mod_vectors(cvec, w_mod, b_mod).reshape(depth, MOD_ROWS, N_MOD, d)
    lat_row = lambda b: b
    ctx_row = lambda b: bsz

    wgu1, wdn1, wgu2, wdn2 = (w.astype(BF16) for w in (w_gu1, w_dn1, w_gu2, w_dn2))
    win, wout = w_in.astype(BF16), w_out.astype(BF16)

    xc = ctx
    for l in range(depth):
        need_ctx = l < depth - 1
        lambda_init = 0.8 - 0.6 * math.exp(-0.3 * l)
        mods = mods_all
        qnw = jnp.tile(qn_w[l], 256 // HEAD_DIM).reshape(1, 256)
        knw = jnp.tile(kn_w[l], 128 // HEAD_DIM).reshape(1, 128)
        lamp = jnp.stack([lam_q1[l], lam_k1[l], lam_q2[l], lam_k2[l]])
        subw = subln_w[l].reshape(HEAD_DIM, 1)

        x = _ffn_step(x, mods, lat_row, 0, wgu1, wdn1, l, ln_g[l, 0], ln_b[l, 0], alpha)
        xc = _ffn_step(xc, mods, ctx_row, 0, wgu1, wdn1, l, ln_g[l, 0], ln_b[l, 0], alpha)

        qa, ka, vat, g3, qc, kc, vct, qd, kd, vdt = _in_projection(x, mods, lat_row, win, l, tabs_a, tabs_h, qnw, knw, gsum)
        (qa_c, ka_c, vat_c, g3_c, qc_c, kc_c, vct_c, qd_c, kd_c, vdt_c) = _in_projection(
            xc, mods, ctx_row, win, l, tabs_id, tabs_id, qnw, knw, gsum)
        ya = _dense_attention(qa, ka_c, vat_c, ka, vat, n_maps=8, dq=A_QK_DIM, lamp=lamp, subw=subw,
                              diff_lambda_init=lambda_init)
        yw = _window_attention(qc, kc_c, vct_c, kc, vct, sink[l])
        yg = _dense_attention(qd, kd_c, vdt_c, kd, vdt, n_maps=4, dq=HEAD_DIM)
        x = _out_projection(x, mods, lat_row, ya, g3, yw, yg, conv_w[l], wout, l, ln_g[l, 1], ln_b[l, 1], alpha)

        x = _ffn_step(x, mods, lat_row, 6, wgu2, wdn2, l, ln_g[l, 2], ln_b[l, 2], alpha)

        if need_ctx:
            ya_c = _dense_attention(qa_c, ka_c, vat_c, None, None, n_maps=8, dq=A_QK_DIM, lamp=lamp, subw=subw,
                                    diff_lambda_init=lambda_init)
            yw_c = _dense_attention(qc_c, kc_c, vct_c, None, None, n_maps=4, dq=HEAD_DIM, sink=sink[l])
            yg_c = _dense_attention(qd_c, kd_c, vdt_c, None, None, n_maps=4, dq=HEAD_DIM)
            xc = _out_projection(xc, mods, ctx_row, ya_c, g3_c, yw_c, yg_c, conv_w[l], wout, l, ln_g[l, 1],
                                 ln_b[l, 1], alpha)
            xc = _ffn_step(xc, mods, ctx_row, 6, wgu2, wdn2, l, ln_g[l, 2], ln_b[l, 2], alpha)
    return x
```

```python
import functools
import math

import jax
import jax.numpy as jnp
from jax import lax
from jax.experimental import pallas as pl
from jax.experimental.pallas import tpu as pltpu

F32 = jnp.float32
BF16 = jnp.bfloat16

GRID_W = 64
HEAD_DIM = 64
WINDOW = 128
ROPE_BASE = 10000.0
EPS = 1e-6
NEG = -1e30
A_QK_DIM = 32
N_MOD = 9
MOD_ROWS = 8
VMEM_LIMIT = 56 * 1024 * 1024

OFF_AQ, OFF_AK, OFF_AV = 0, 256, 512
OFF_B = 768
OFF_CQ, OFF_CK, OFF_CV = 1536, 1792, 1920
OFF_DQ, OFF_DK, OFF_DV = 2048, 2304, 2432
IN_W = 2560

TOK_TILE = 512
Q_TILE = 256
DENSE_Q_TILE = 256
FF_CHUNK = 256
SCORE_ROWS = 128
PV_ROWS = 256
ATTN_CHUNK = 1024
STAGES = 4
LOOKAHEAD = 2
ROUNDS_PER_ITER = 8
K_TILE = 128
ONES_ROWS = 16
V_ROWS = HEAD_DIM + ONES_ROWS
LOG2E = math.log2(math.e)


def _params(*sem):
    return pltpu.CompilerParams(dimension_semantics=sem, vmem_limit_bytes=VMEM_LIMIT)


def _layer_norm(r, g, b):
    mu = jnp.mean(r, axis=-1, keepdims=True)
    d = r - mu
    var = jnp.mean(d * d, axis=-1, keepdims=True)
    return d * lax.rsqrt(var + EPS) * g + b


def _mod_kernel(c_ref, w_ref, b_ref, o_ref):
    c = c_ref[...]
    a = (c * jax.nn.sigmoid(c)).astype(BF16)
    o_ref[0] = jnp.dot(a, w_ref[0].astype(BF16), preferred_element_type=F32) + b_ref[0]


def _mod_vectors(cvec, w_mod, b_mod):
    depth, d, n = w_mod.shape
    tn = n // 8
    return pl.pallas_call(
        _mod_kernel,
        grid=(depth, n // tn),
        in_specs=[
            pl.BlockSpec((MOD_ROWS, d), lambda l, j: (0, 0)),
            pl.BlockSpec((1, d, tn), lambda l, j: (l, 0, j)),
            pl.BlockSpec((1, 1, tn), lambda l, j: (l, 0, j)),
        ],
        out_specs=pl.BlockSpec((1, MOD_ROWS, tn), lambda l, j: (l, 0, j)),
        out_shape=jax.ShapeDtypeStruct((depth, MOD_ROWS, n), F32),
        compiler_params=_params("parallel", "parallel"),
        name="mod_vectors",
    )(cvec, w_mod, b_mod.reshape(depth, 1, n))


def _ffn_body(x, mod_ref, wgu_ref, wdn_ref, g_ref, b_ref, acc_ref, mod_base, alpha):
    shift = mod_ref[0, mod_base:mod_base + 1, :]
    scale = mod_ref[0, mod_base + 1:mod_base + 2, :]
    gate = mod_ref[0, mod_base + 2:mod_base + 3, :]
    h = (x * (1.0 + scale) + shift).astype(BF16)
    d_ff = wdn_ref.shape[0]
    for c in range(d_ff // FF_CHUNK):
        lo = c * FF_CHUNK
        g = jnp.dot(h, wgu_ref[:, lo:lo + FF_CHUNK], preferred_element_type=F32)
        u = jnp.dot(h, wgu_ref[:, d_ff + lo:d_ff + lo + FF_CHUNK], preferred_element_type=F32)
        a = (g * jax.nn.sigmoid(g) * u).astype(BF16)
        y = jnp.dot(a, wdn_ref[lo:lo + FF_CHUNK, :], preferred_element_type=F32)
        if c == 0:
            acc_ref[...] = y
        else:
            acc_ref[...] += y
    r = alpha * x + (0.5 * gate) * acc_ref[...]
    return _layer_norm(r, g_ref[...], b_ref[...])


def _rope(x, cos, sin_next, sin_prev, quarter):
    width = x.shape[1]
    reps = width // cos.shape[1]
    if reps > 1:
        cos = jnp.concatenate([cos] * reps, axis=1)
        sin_next = jnp.concatenate([sin_next] * reps, axis=1)
        sin_prev = jnp.concatenate([sin_prev] * reps, axis=1)
    x_next = pltpu.roll(x, width - quarter, axis=1)
    x_prev = pltpu.roll(x, quarter, axis=1)
    return x * cos + x_next * sin_next + x_prev * sin_prev


def _head_rms(x, gsum_ref, w):
    x2 = x * x
    hi = x2.astype(BF16)
    lo = (x2 - hi.astype(F32)).astype(BF16)
    gs = gsum_ref[:x.shape[1], :x.shape[1]]
    ss = jnp.dot(hi, gs, preferred_element_type=F32) + jnp.dot(lo, gs, preferred_element_type=F32)
    return x * lax.rsqrt(ss * (1.0 / HEAD_DIM) + EPS) * w


def _store_k(k_ref, k):
    for kt in range(k.shape[1] // K_TILE):
        k_ref[0, kt] = k[:, kt * K_TILE:(kt + 1) * K_TILE].astype(BF16)


def _store_vt(vt_ref, v):
    vt = v.T
    tm = v.shape[0]
    for hd in range(v.shape[1] // HEAD_DIM):
        vt_ref[0, hd, 0:HEAD_DIM, :] = vt[hd * HEAD_DIM:(hd + 1) * HEAD_DIM, :].astype(BF16)
        vt_ref[0, hd, HEAD_DIM:V_ROWS, :] = jnp.ones((ONES_ROWS, tm), BF16)


def _ffn_inproj_kernel(x_ref, mod_ref, wgu_ref, wdn_ref, g_ref, b_ref, w_ref,
                       ca_ref, san_ref, sap_ref, ch_ref, shn_ref, shp_ref, qnw_ref, knw_ref, gsum_ref,
                       x1_ref, qa_ref, ka_ref, vat_ref, g3_ref, qc_ref, kc_ref, vct_ref, qd_ref, kd_ref, vdt_ref,
                       acc_ref, *, alpha):
    x = _ffn_body(x_ref[0], mod_ref, wgu_ref, wdn_ref, g_ref, b_ref, acc_ref, 0, alpha)
    x1_ref[0] = x
    shift = mod_ref[0, 3:4, :]
    scale = mod_ref[0, 4:5, :]
    h = (x * (1.0 + scale) + shift).astype(BF16)

    def seg(lo, hi):
        return jnp.dot(h, w_ref[:, lo:hi], preferred_element_type=F32)

    rope_a = functools.partial(_rope, cos=ca_ref[...], sin_next=san_ref[...], sin_prev=sap_ref[...],
                               quarter=A_QK_DIM // 4)
    rope_h = functools.partial(_rope, cos=ch_ref[...], sin_next=shn_ref[...], sin_prev=shp_ref[...],
                               quarter=HEAD_DIM // 4)
    qa_ref[0] = (rope_a(seg(OFF_AQ, OFF_AK)) * (LOG2E * A_QK_DIM ** -0.5)).astype(BF16)
    _store_k(ka_ref, rope_a(seg(OFF_AK, OFF_AV)))
    _store_vt(vat_ref, seg(OFF_AV, OFF_B))
    g3_ref[0] = seg(OFF_B, OFF_CQ)
    qc_ref[0] = (rope_h(seg(OFF_CQ, OFF_CK)) * (LOG2E * HEAD_DIM ** -0.5)).astype(BF16)
    kv = seg(OFF_CK, OFF_DQ)
    _store_k(kc_ref, rope_h(kv[:, :OFF_CV - OFF_CK]))
    _store_vt(vct_ref, kv[:, OFF_CV - OFF_CK:])
    qd = _head_rms(seg(OFF_DQ, OFF_DK), gsum_ref, qnw_ref[...])
    qd_ref[0] = (rope_h(qd) * (LOG2E * HEAD_DIM ** -0.5)).astype(BF16)
    kv = seg(OFF_DK, IN_W)
    kd = _head_rms(kv[:, :OFF_DV - OFF_DK], gsum_ref, knw_ref[...])
    _store_k(kd_ref, rope_h(kd))
    _store_vt(vdt_ref, kv[:, OFF_DV - OFF_DK:])


def _ffn_in_projection(x, mods, mod_row, w_gu, w_dn, ln_g, ln_b, alpha, w_in, layer, tabs_a, tabs_h, qnw, knw, gsum):
    bsz, t, d = x.shape
    d_ff = w_dn.shape[1]
    tm = min(TOK_TILE, t)
    nc = t // tm
    const = lambda b, i: (0, 0)
    tab = pl.BlockSpec((tm, 128), lambda b, i: (i, 0))
    q_spec = pl.BlockSpec((1, tm, 256), lambda b, i: (b, i, 0))

    def k_spec(w):
        return pl.BlockSpec((1, w // K_TILE, tm, K_TILE), lambda b, i: (b, 0, i, 0))

    def vt_spec(w):
        return pl.BlockSpec((1, w // HEAD_DIM, V_ROWS, tm), lambda b, i: (b, 0, 0, i))

    def q_shape():
        return jax.ShapeDtypeStruct((bsz, t, 256), BF16)

    def k_shape(w):
        return jax.ShapeDtypeStruct((bsz, w // K_TILE, t, K_TILE), BF16)

    def vt_shape(w):
        return jax.ShapeDtypeStruct((bsz, w // HEAD_DIM, V_ROWS, t), BF16)

    x_spec = pl.BlockSpec((1, tm, d), lambda b, i: (b, i, 0))
    return pl.pallas_call(
        functools.partial(_ffn_inproj_kernel, alpha=alpha),
        grid=(bsz, nc),
        in_specs=[
            x_spec,
            pl.BlockSpec((None, 1, N_MOD, d), lambda b, i: (layer, mod_row(b), 0, 0)),
            pl.BlockSpec((None, d, 2 * d_ff), lambda b, i: (layer, 0, 0), pipeline_mode=pl.Buffered(1)),
            pl.BlockSpec((None, d_ff, d), lambda b, i: (layer, 0, 0), pipeline_mode=pl.Buffered(1)),
            pl.BlockSpec((1, d), const),
            pl.BlockSpec((1, d), const),
            pl.BlockSpec((None, d, IN_W), lambda b, i: (layer, 0, 0), pipeline_mode=pl.Buffered(1)),
            tab, tab, tab, tab, tab, tab,
            pl.BlockSpec((1, 256), const),
            pl.BlockSpec((1, 128), const),
            pl.BlockSpec((256, 256), const),
        ],
        out_specs=[x_spec, q_spec, k_spec(256), vt_spec(256),
                   pl.BlockSpec((1, tm, 768), lambda b, i: (b, i, 0)),
                   q_spec, k_spec(128), vt_spec(128),
                   q_spec, k_spec(128), vt_spec(128)],
        out_shape=[jax.ShapeDtypeStruct(x.shape, F32), q_shape(), k_shape(256), vt_shape(256),
                   jax.ShapeDtypeStruct((bsz, t, 768), F32),
                   q_shape(), k_shape(128), vt_shape(128),
                   q_shape(), k_shape(128), vt_shape(128)],
        scratch_shapes=[pltpu.VMEM((tm, d), F32)],
        compiler_params=_params("parallel", "parallel"),
        name="ffn_in_projection",
    )(x, mods, w_gu, w_dn, ln_g.reshape(1, d), ln_b.reshape(1, d), w_in, *tabs_a, *tabs_h, qnw, knw, gsum)


def _whole_per_batch(arr):
    zeros = (0,) * (arr.ndim - 1)
    return pl.BlockSpec((1,) + arr.shape[1:], lambda b, i: (b,) + zeros, pipeline_mode=pl.Buffered(1))


def _attn_scratch(n_maps, tq, stage_rows):
    shapes = ([((n_maps, K_TILE, tq), BF16), ((n_maps, V_ROWS, tq), F32), ((n_maps, 8, tq), F32)]
              + [((stage_rows, tq), F32)] * STAGES + [((8, tq), F32)] * STAGES)
    return [pltpu.VMEM(shape, dtype) for shape, dtype in shapes]


def _key_offset(j, dq, grouped):
    return (j // 2) * dq if grouped else j * dq


def _fill_qpad(q_ref, qpad_ref, n_maps, dq, grouped):
    tq = q_ref.shape[1]
    qt = q_ref[0].astype(F32).T
    for j in range(n_maps):
        row = _key_offset(j, dq, grouped) % K_TILE
        parts = []
        if row:
            parts.append(jnp.zeros((row, tq), F32))
        parts.append(qt[j * dq:(j + 1) * dq, :])
        if K_TILE - row - dq:
            parts.append(jnp.zeros((K_TILE - row - dq, tq), F32))
        qpad_ref[j] = jnp.concatenate(parts, axis=0).astype(BF16)


def _score_unit(keys, j, nk, qpad_ref, s_ref, bm_ref, valid=None):
    tq = s_ref.shape[1]
    w = qpad_ref[j]
    bm = None
    for r in range(nk // SCORE_ROWS):
        rows = slice(r * SCORE_ROWS, (r + 1) * SCORE_ROWS)
        s = jnp.dot(keys(rows), w, preferred_element_type=F32)
        if valid is not None:
            s = jnp.where(valid(rows), s, NEG)
        s_ref[rows, :] = s
        part = jnp.max(s.reshape(SCORE_ROWS // 8, 8, tq), axis=0)
        bm = part if bm is None else jnp.maximum(bm, part)
    bm_ref[...] = bm


def _softmax_unit(values, j, nk, s_ref, bm_ref, m_ref, acc_ref, row0):
    m_old = m_ref[j, 0:1, :]
    m_new = jnp.maximum(m_old, jnp.max(bm_ref[...], axis=0, keepdims=True))
    alpha = jnp.exp2(m_old - m_new)
    pv = None
    for r in range(nk // PV_ROWS):
        rows = pl.ds(pl.multiple_of(row0 + r * PV_ROWS, PV_ROWS), PV_ROWS)
        p = jnp.exp2(s_ref[rows, :] - m_new)
        d = jnp.dot(values(slice(r * PV_ROWS, (r + 1) * PV_ROWS)), p.astype(BF16), preferred_element_type=F32)
        pv = d if pv is None else pv + d
    acc_ref[j] = alpha * acc_ref[j] + pv
    m_ref[j] = jnp.broadcast_to(m_new, m_ref.shape[1:])


def _init_state(m_ref, acc_ref, n_maps, sink_ref):
    tq = m_ref.shape[-1]
    for j in range(n_maps):
        if sink_ref is None:
            m_ref[j] = jnp.full(m_ref.shape[1:], NEG, F32)
            acc_ref[j] = jnp.zeros((V_ROWS, tq), F32)
        else:
            m_ref[j] = jnp.full(m_ref.shape[1:], sink_ref[j] * LOG2E, F32)
            acc_ref[j] = jnp.concatenate([jnp.zeros((HEAD_DIM, tq), F32), jnp.ones((ONES_ROWS, tq), F32)], axis=0)


def _normalised(acc_ref, j):
    acc = acc_ref[j]
    return acc[0:HEAD_DIM, :] / acc[HEAD_DIM:HEAD_DIM + 1, :]


def _finish_plain(o_ref, acc_ref, n_maps):
    ot = jnp.concatenate([_normalised(acc_ref, j) for j in range(n_maps)], axis=0)
    o_ref[0] = ot.T.astype(BF16)


def _finish_diff(o_ref, acc_ref, lamp_ref, subw_ref, n_maps, lambda_init):
    lp = lamp_ref[...]
    lam = (jnp.exp(jnp.sum(lp[0:1] * lp[1:2], axis=1, keepdims=True))
           - jnp.exp(jnp.sum(lp[2:3] * lp[3:4], axis=1, keepdims=True)) + lambda_init)
    outs = []
    for hd in range(n_maps // 2):
        o = _normalised(acc_ref, 2 * hd) - lam * _normalised(acc_ref, 2 * hd + 1)
        ms = jnp.mean(o * o, axis=0, keepdims=True)
        outs.append(o * lax.rsqrt(ms + EPS) * subw_ref[...] * (1.0 - lambda_init))
    o_ref[0] = jnp.concatenate(outs, axis=0).T.astype(BF16)


def _dense_attn_kernel(*refs, n_maps, dq, grouped, has_latent, has_sink, diff_lambda_init):
    refs = list(refs)
    q_ref, kc_ref, vct_ref, zero_ref = refs[:4]
    pos = 4
    k_ref = vt_ref = sink_ref = lamp_ref = subw_ref = None
    if has_latent:
        k_ref, vt_ref = refs[pos:pos + 2]
        pos += 2
    if has_sink:
        sink_ref = refs[pos]
        pos += 1
    if diff_lambda_init is not None:
        lamp_ref, subw_ref = refs[pos:pos + 2]
        pos += 2
    o_ref, qpad_ref, acc_ref, m_ref = refs[pos:pos + 4]
    s_refs = refs[pos + 4:pos + 4 + STAGES]
    bm_refs = refs[pos + 4 + STAGES:pos + 4 + 2 * STAGES]

    _fill_qpad(q_ref, qpad_ref, n_maps, dq, grouped)
    _init_state(m_ref, acc_ref, n_maps, sink_ref)
    n_ctx = kc_ref.shape[2]
    tk = k_ref.shape[3] if has_latent else 0
    n_lat = n_maps * k_ref.shape[2] if has_latent else 0
    row0 = zero_ref[0]

    def score(v, slot):
        if isinstance(v, int) and v < n_maps:
            kt = _key_offset(v, dq, grouped) // K_TILE
            _score_unit(lambda rows: kc_ref[0, kt, rows, :], v, n_ctx, qpad_ref, s_refs[slot], bm_refs[slot])
        else:
            u = v - n_maps
            j, c = u % n_maps, u // n_maps
            kt = _key_offset(j, dq, grouped) // K_TILE
            _score_unit(lambda rows: k_ref[0, kt, c, rows, :], j, tk, qpad_ref, s_refs[slot], bm_refs[slot])

    def softmax(v, slot):
        if isinstance(v, int) and v < n_maps:
            _softmax_unit(lambda cols: vct_ref[0, v // 2, :, cols], v, n_ctx, s_refs[slot], bm_refs[slot],
                          m_ref, acc_ref, row0)
        else:
            u = v - n_maps
            j, c = u % n_maps, u // n_maps
            _softmax_unit(lambda cols: vt_ref[0, c, j // 2, :, cols], j, tk, s_refs[slot], bm_refs[slot],
                          m_ref, acc_ref, row0)

    n_units = n_maps + n_lat
    for v in range(LOOKAHEAD):
        score(v, v % STAGES)
    for v in range(n_maps):
        if v + LOOKAHEAD < n_units:
            score(v + LOOKAHEAD, (v + LOOKAHEAD) % STAGES)
        softmax(v, v % STAGES)

    if has_latent:
        per_iter = STAGES * math.gcd(n_lat // STAGES, ROUNDS_PER_ITER)

        def body(i, carry):
            v0 = n_maps + per_iter * i
            for k in range(per_iter):
                score(jnp.minimum(v0 + k + LOOKAHEAD, n_units - 1), (k + LOOKAHEAD) % STAGES)
                softmax(v0 + k, k % STAGES)
            return carry
        lax.fori_loop(0, n_lat // per_iter, body, 0)
    if diff_lambda_init is None:
        _finish_plain(o_ref, acc_ref, n_maps)
    else:
        _finish_diff(o_ref, acc_ref, lamp_ref, subw_ref, n_maps, diff_lambda_init)


def _dense_attention(q, k_ctx, vt_ctx, k_lat, vt_lat, *, n_maps, dq, sink=None, lamp=None, subw=None,
                     diff_lambda_init=None):
    bsz, t, qw = q.shape
    key_tiles, n_ctx = k_ctx.shape[1:3]
    heads = vt_ctx.shape[1]
    grouped = key_tiles * K_TILE != n_maps * dq
    tq = min(DENSE_Q_TILE, t)
    has_latent = k_lat is not None
    whole = _whole_per_batch
    assert n_maps % STAGES == 0 and n_ctx % PV_ROWS == 0

    in_specs = [pl.BlockSpec((1, tq, qw), lambda b, i: (b, i, 0)), whole(k_ctx), whole(vt_ctx),
                pl.BlockSpec(memory_space=pltpu.SMEM)]
    args = [q, k_ctx, vt_ctx, jnp.zeros((1,), jnp.int32)]
    stage_rows = n_ctx
    if has_latent:
        n_keys = k_lat.shape[2]
        tk = math.gcd(ATTN_CHUNK, n_keys)
        n_chunks = n_keys // tk
        assert tk % PV_ROWS == 0 and tk % SCORE_ROWS == 0
        k_lat = k_lat.reshape(bsz, key_tiles, n_chunks, tk, K_TILE)
        vt_lat = vt_lat.reshape(bsz, heads, V_ROWS, n_chunks, tk).transpose(0, 3, 1, 2, 4)
        in_specs += [whole(k_lat), whole(vt_lat)]
        args += [k_lat, vt_lat]
        stage_rows = max(n_ctx, tk)
    if sink is not None:
        in_specs.append(pl.BlockSpec(memory_space=pltpu.SMEM))
        args.append(sink)
    if diff_lambda_init is not None:
        in_specs += [pl.BlockSpec(lamp.shape, lambda b, i: (0, 0)), pl.BlockSpec(subw.shape, lambda b, i: (0, 0))]
        args += [lamp, subw]
    scratch = _attn_scratch(n_maps, tq, stage_rows)
    return pl.pallas_call(
        functools.partial(_dense_attn_kernel, n_maps=n_maps, dq=dq, grouped=grouped, has_latent=has_latent,
                          has_sink=sink is not None, diff_lambda_init=diff_lambda_init),
        grid=(bsz, t // tq),
        in_specs=in_specs,
        out_specs=pl.BlockSpec((1, tq, 256), lambda b, i: (b, i, 0)),
        out_shape=jax.ShapeDtypeStruct((bsz, t, 256), BF16),
        scratch_shapes=scratch,
        compiler_params=_params("parallel", "parallel"),
        name="dense_attention",
    )(*args)


def _window_attn_kernel(q_ref, kp_ref, kc_ref, kn_ref, vtp_ref, vtc_ref, vtn_ref, kctx_ref, vtctx_ref, zero_ref,
                        sink_ref, o_ref, qpad_ref, acc_ref, m_ref, *stage_refs, n_maps, seq):
    s_refs, bm_refs = stage_refs[:STAGES], stage_refs[STAGES:]
    tq = q_ref.shape[1]
    q0 = pl.program_id(1) * tq
    _fill_qpad(q_ref, qpad_ref, n_maps, HEAD_DIM, True)
    _init_state(m_ref, acc_ref, n_maps, sink_ref)
    k_win = jnp.concatenate([kp_ref[0], kc_ref[0], kn_ref[0]], axis=0)
    vt_win = jnp.concatenate([vtp_ref[0], vtc_ref[0], vtn_ref[0]], axis=2)
    n_ctx, n_win = kctx_ref.shape[2], k_win.shape[0]
    row0 = zero_ref[0]

    def in_window(rows):
        kpos = q0 - WINDOW + rows.start + lax.broadcasted_iota(jnp.int32, (rows.stop - rows.start, tq), 0)
        qpos = q0 + lax.broadcasted_iota(jnp.int32, (rows.stop - rows.start, tq), 1)
        return (jnp.abs(kpos - qpos) <= WINDOW) & (kpos >= 0) & (kpos < seq)

    def score(v, slot):
        if v < n_maps:
            _score_unit(lambda rows: kctx_ref[0, 0, rows, :], v, n_ctx, qpad_ref, s_refs[slot], bm_refs[slot])
        else:
            _score_unit(lambda rows: k_win[rows, :], v - n_maps, n_win, qpad_ref, s_refs[slot], bm_refs[slot],
                        valid=in_window)

    def softmax(v, slot):
        if v < n_maps:
            _softmax_unit(lambda cols: vtctx_ref[0, v // 2, :, cols], v, n_ctx, s_refs[slot], bm_refs[slot],
                          m_ref, acc_ref, row0)
        else:
            j = v - n_maps
            _softmax_unit(lambda cols: vt_win[j // 2][:, cols], j, n_win, s_refs[slot], bm_refs[slot],
                          m_ref, acc_ref, row0)

    for v in range(LOOKAHEAD):
        score(v, v % STAGES)
    for v in range(2 * n_maps):
        if v + LOOKAHEAD < 2 * n_maps:
            score(v + LOOKAHEAD, (v + LOOKAHEAD) % STAGES)
        softmax(v, v % STAGES)
    _finish_plain(o_ref, acc_ref, n_maps)


def _window_attention(q, k_ctx, vt_ctx, k_lat, vt_lat, sink):
    bsz, t, qw = q.shape
    key_tiles, _, kw = k_lat.shape[1:]
    assert key_tiles == 1
    tq = Q_TILE
    n_maps = 4
    k_flat = k_lat.reshape(bsz, t, kw)
    wb = tq // WINDOW
    nwb = t // WINDOW
    heads = vt_lat.shape[1]
    whole = _whole_per_batch

    def vt_blk(width, col):
        return pl.BlockSpec((1, heads, V_ROWS, width), lambda b, i: (b, 0, 0, col(i)))

    def prev_blk(i):
        return jnp.maximum(i * wb - 1, 0)

    def next_blk(i):
        return jnp.minimum(i * wb + wb, nwb - 1)

    in_specs = [
        pl.BlockSpec((1, tq, qw), lambda b, i: (b, i, 0)),
        pl.BlockSpec((1, WINDOW, kw), lambda b, i: (b, prev_blk(i), 0)),
        pl.BlockSpec((1, tq, kw), lambda b, i: (b, i, 0)),
        pl.BlockSpec((1, WINDOW, kw), lambda b, i: (b, next_blk(i), 0)),
        vt_blk(WINDOW, prev_blk),
        vt_blk(tq, lambda i: i),
        vt_blk(WINDOW, next_blk),
        whole(k_ctx), whole(vt_ctx),
        pl.BlockSpec(memory_space=pltpu.SMEM),
        pl.BlockSpec(memory_space=pltpu.SMEM),
    ]
    stage_rows = max(k_ctx.shape[2], tq + 2 * WINDOW)
    return pl.pallas_call(
        functools.partial(_window_attn_kernel, n_maps=n_maps, seq=t),
        grid=(bsz, t // tq),
        in_specs=in_specs,
        out_specs=pl.BlockSpec((1, tq, 256), lambda b, i: (b, i, 0)),
        out_shape=jax.ShapeDtypeStruct((bsz, t, 256), BF16),
        scratch_shapes=_attn_scratch(n_maps, tq, stage_rows),
        compiler_params=_params("parallel", "parallel"),
        name="window_attention",
    )(q, k_flat, k_flat, k_flat, vt_lat, vt_lat, vt_lat, k_ctx, vt_ctx, jnp.zeros((1,), jnp.int32), sink)


def _outproj_ffn_kernel(x_ref, mod_ref, ya_ref, g3_ref, g3p_ref, g3n_ref, yw_ref, yg_ref, cw_ref, w_ref, g_ref, b_ref,
                        wgu_ref, wdn_ref, g2_ref, b2_ref, o_ref, acc_ref, *, alpha):
    i = pl.program_id(1)
    last = pl.num_programs(1) - 1
    tm = x_ref.shape[1]
    gb = g3_ref[0, :, 0:256]
    hid = g3_ref[0, :, 256:512] * g3_ref[0, :, 512:768]
    halo = g3p_ref.shape[1]
    h_prev = g3p_ref[0, halo - 1:halo, 256:512] * g3p_ref[0, halo - 1:halo, 512:768]
    h_next = g3n_ref[0, 0:1, 256:512] * g3n_ref[0, 0:1, 512:768]
    h_prev = jnp.where(i == 0, 0.0, h_prev)
    h_next = jnp.where(i == last, 0.0, h_next)
    row = lax.broadcasted_iota(jnp.int32, hid.shape, 0)
    below = jnp.where(row == 0, h_prev, pltpu.roll(hid, 1, axis=0))
    above = jnp.where(row == tm - 1, h_next, pltpu.roll(hid, tm - 1, axis=0))
    yb = gb * (cw_ref[0:1, :] * below + cw_ref[1:2, :] * hid + cw_ref[2:3, :] * above)
    y = jnp.dot(ya_ref[0], w_ref[0:256, :], preferred_element_type=F32)
    y += jnp.dot(yb.astype(BF16), w_ref[256:512, :], preferred_element_type=F32)
    y += jnp.dot(yw_ref[0], w_ref[512:768, :], preferred_element_type=F32)
    y += jnp.dot(yg_ref[0], w_ref[768:1024, :], preferred_element_type=F32)
    gate = mod_ref[0, 5:6, :]
    x_mix = _layer_norm(alpha * x_ref[0] + gate * y, g_ref[...], b_ref[...])
    o_ref[0] = _ffn_body(x_mix, mod_ref, wgu_ref, wdn_ref, g2_ref, b2_ref, acc_ref, 6, alpha)


def _out_projection_ffn(x, mods, mod_row, ya, g3, yw, yg, conv_w, w_out, w_gu, w_dn, layer, ln_g, ln_b, ln_g2, ln_b2,
                        alpha):
    bsz, t, d = x.shape
    d_ff = w_dn.shape[1]
    tm = min(TOK_TILE, t)
    halo = 8
    hb = tm // halo
    n_halo = t // halo
    const = lambda b, i: (0, 0)
    y_spec = pl.BlockSpec((1, tm, 256), lambda b, i: (b, i, 0))
    return pl.pallas_call(
        functools.partial(_outproj_ffn_kernel, alpha=alpha),
        grid=(bsz, t // tm),
        in_specs=[
            pl.BlockSpec((1, tm, d), lambda b, i: (b, i, 0)),
            pl.BlockSpec((None, 1, N_MOD, d), lambda b, i: (layer, mod_row(b), 0, 0)),
            y_spec,
            pl.BlockSpec((1, tm, 768), lambda b, i: (b, i, 0)),
            pl.BlockSpec((1, halo, 768), lambda b, i: (b, jnp.maximum(i * hb - 1, 0), 0)),
            pl.BlockSpec((1, halo, 768), lambda b, i: (b, jnp.minimum(i * hb + hb, n_halo - 1), 0)),
            y_spec, y_spec,
            pl.BlockSpec(conv_w.shape, const),
            pl.BlockSpec((None,) + w_out.shape[1:], lambda b, i: (layer, 0, 0), pipeline_mode=pl.Buffered(1)),
            pl.BlockSpec((1, d), const),
            pl.BlockSpec((1, d), const),
            pl.BlockSpec((None, d, 2 * d_ff), lambda b, i: (layer, 0, 0), pipeline_mode=pl.Buffered(1)),
            pl.BlockSpec((None, d_ff, d), lambda b, i: (layer, 0, 0), pipeline_mode=pl.Buffered(1)),
            pl.BlockSpec((1, d), const),
            pl.BlockSpec((1, d), const),
        ],
        out_specs=pl.BlockSpec((1, tm, d), lambda b, i: (b, i, 0)),
        out_shape=jax.ShapeDtypeStruct(x.shape, F32),
        scratch_shapes=[pltpu.VMEM((tm, d), F32)],
        compiler_params=_params("parallel", "parallel"),
        name="out_projection_ffn",
    )(x, mods, ya, g3, g3, g3, yw, yg, conv_w, w_out, ln_g.reshape(1, d), ln_b.reshape(1, d), w_gu, w_dn,
      ln_g2.reshape(1, d), ln_b2.reshape(1, d))


def _rope_tables(rows, dim):
    n_freq = dim // 4
    inv_freq = ROPE_BASE ** (-jnp.arange(n_freq, dtype=F32) / n_freq)
    ang_r = jnp.arange(rows, dtype=F32)[:, None] * inv_freq
    ang_c = jnp.arange(GRID_W, dtype=F32)[:, None] * inv_freq

    def on_grid(of_row, of_col):
        r = jnp.broadcast_to(of_row[:, None, :], (rows, GRID_W, n_freq))
        c = jnp.broadcast_to(of_col[None, :, :], (rows, GRID_W, n_freq))
        return jnp.concatenate([r, r, c, c], axis=-1).reshape(rows * GRID_W, dim)

    cos = on_grid(jnp.cos(ang_r), jnp.cos(ang_c))
    sin = on_grid(jnp.sin(ang_r), jnp.sin(ang_c))
    reps = 128 // dim
    cos = jnp.tile(cos, (1, reps))
    sin = jnp.tile(sin, (1, reps))
    first_half = (jnp.arange(128) % (dim // 2)) < (dim // 4)
    sin_next = jnp.where(first_half, -sin, 0.0)
    sin_prev = jnp.where(first_half, 0.0, sin)
    return cos, sin_next, sin_prev


def _identity_tables(t):
    return jnp.ones((t, 128), F32), jnp.zeros((t, 128), F32), jnp.zeros((t, 128), F32)


def kernel(x, c, ctx, c_ctx, w_mod, b_mod, ln_g, ln_b, w_gu1, w_dn1, w_in, w_out, conv_w, lam_q1, lam_k1, lam_q2,
           lam_k2, subln_w, sink, qn_w, kn_w, w_gu2, w_dn2):
    bsz, seq, d = x.shape
    ctx_len = ctx.shape[1]
    depth = w_mod.shape[0]
    alpha = (2.0 * depth) ** 0.25
    assert seq % TOK_TILE == 0 and ctx_len % Q_TILE == 0 and bsz < MOD_ROWS

    tabs_a = _rope_tables(seq // GRID_W, A_QK_DIM)
    tabs_h = _rope_tables(seq // GRID_W, HEAD_DIM)
    tabs_id = _identity_tables(ctx_len)
    gsum = (jnp.arange(256)[:, None] // HEAD_DIM == jnp.arange(256)[None, :] // HEAD_DIM).astype(BF16)

    cvec = jnp.zeros((MOD_ROWS, d), F32).at[:bsz].set(c).at[bsz].set(c_ctx)
    mods_all = _mod_vectors(cvec, w_mod, b_mod).reshape(depth, MOD_ROWS, N_MOD, d)
    lat_row = lambda b: b
    ctx_row = lambda b: bsz

    wgu1, wdn1, wgu2, wdn2 = (w.astype(BF16) for w in (w_gu1, w_dn1, w_gu2, w_dn2))
    win, wout = w_in.astype(BF16), w_out.astype(BF16)

    xc = ctx
    for l in range(depth):
        need_ctx = l < depth - 1
        lambda_init = 0.8 - 0.6 * math.exp(-0.3 * l)
        mods = mods_all
        qnw = jnp.tile(qn_w[l], 256 // HEAD_DIM).reshape(1, 256)
        knw = jnp.tile(kn_w[l], 128 // HEAD_DIM).reshape(1, 128)
        lamp = jnp.stack([lam_q1[l], lam_k1[l], lam_q2[l], lam_k2[l]])
        subw = subln_w[l].reshape(HEAD_DIM, 1)

        x, qa, ka, vat, g3, qc, kc, vct, qd, kd, vdt = _ffn_in_projection(
            x, mods, lat_row, wgu1, wdn1, ln_g[l, 0], ln_b[l, 0], alpha, win, l, tabs_a, tabs_h, qnw, knw, gsum)
        xc, qa_c, ka_c, vat_c, g3_c, qc_c, kc_c, vct_c, qd_c, kd_c, vdt_c = _ffn_in_projection(
            xc, mods, ctx_row, wgu1, wdn1, ln_g[l, 0], ln_b[l, 0], alpha, win, l, tabs_id, tabs_id, qnw, knw, gsum)

        ya = _dense_attention(qa, ka_c, vat_c, ka, vat, n_maps=8, dq=A_QK_DIM, lamp=lamp, subw=subw,
                              diff_lambda_init=lambda_init)
        yw = _window_attention(qc, kc_c, vct_c, kc, vct, sink[l])
        yg = _dense_attention(qd, kd_c, vdt_c, kd, vdt, n_maps=4, dq=HEAD_DIM)
        x = _out_projection_ffn(x, mods, lat_row, ya, g3, yw, yg, conv_w[l], wout, wgu2, wdn2, l,
                                ln_g[l, 1], ln_b[l, 1], ln_g[l, 2], ln_b[l, 2], alpha)

        if need_ctx:
            ya_c = _dense_attention(qa_c, ka_c, vat_c, None, None, n_maps=8, dq=A_QK_DIM, lamp=lamp, subw=subw,
                                    diff_lambda_init=lambda_init)
            yw_c = _dense_attention(qc_c, kc_c, vct_c, None, None, n_maps=4, dq=HEAD_DIM, sink=sink[l])
            yg_c = _dense_attention(qd_c, kd_c, vdt_c, None, None, n_maps=4, dq=HEAD_DIM)
            xc = _out_projection_ffn(xc, mods, ctx_row, ya_c, g3_c, yw_c, yg_c, conv_w[l], wout, wgu2, wdn2, l,
                                     ln_g[l, 1], ln_b[l, 1], ln_g[l, 2], ln_b[l, 2], alpha)
    return x
```

```python
import functools
import math

import jax
import jax.numpy as jnp
from jax import lax
from jax.experimental import pallas as pl
from jax.experimental.pallas import tpu as pltpu

F32 = jnp.float32
BF16 = jnp.bfloat16

GRID_W = 64
HEAD_DIM = 64
WINDOW = 128
ROPE_BASE = 10000.0
EPS = 1e-6
NEG = -1e30
A_QK_DIM = 32
N_MOD = 9
MOD_ROWS = 8
VMEM_LIMIT = 56 * 1024 * 1024

OFF_AQ, OFF_AK, OFF_AV = 0, 256, 512
OFF_B = 768
OFF_CQ, OFF_CK, OFF_CV = 1536, 1792, 1920
OFF_DQ, OFF_DK, OFF_DV = 2048, 2304, 2432
IN_W = 2560

TOK_TILE = 512
Q_TILE = 256
DENSE_Q_TILE = 256
FF_CHUNK = 256
SCORE_ROWS = 128
PV_ROWS = 256
ATTN_CHUNK = 1024
STAGES = 4
LOOKAHEAD = 2
ROUNDS_PER_ITER = 8
K_TILE = 128
ONES_ROWS = 16
V_ROWS = HEAD_DIM + ONES_ROWS
LOG2E = math.log2(math.e)


def _params(*sem):
    return pltpu.CompilerParams(dimension_semantics=sem, vmem_limit_bytes=VMEM_LIMIT)


def _layer_norm(r, g, b):
    mu = jnp.mean(r, axis=-1, keepdims=True)
    d = r - mu
    var = jnp.mean(d * d, axis=-1, keepdims=True)
    return d * lax.rsqrt(var + EPS) * g + b


def _mod_kernel(c_ref, w_ref, b_ref, o_ref):
    c = c_ref[...]
    a = (c * jax.nn.sigmoid(c)).astype(BF16)
    o_ref[0] = jnp.dot(a, w_ref[0].astype(BF16), preferred_element_type=F32) + b_ref[0]


def _mod_vectors(cvec, w_mod, b_mod):
    depth, d, n = w_mod.shape
    tn = n // 8
    return pl.pallas_call(
        _mod_kernel,
        grid=(depth, n // tn),
        in_specs=[
            pl.BlockSpec((MOD_ROWS, d), lambda l, j: (0, 0)),
            pl.BlockSpec((1, d, tn), lambda l, j: (l, 0, j)),
            pl.BlockSpec((1, 1, tn), lambda l, j: (l, 0, j)),
        ],
        out_specs=pl.BlockSpec((1, MOD_ROWS, tn), lambda l, j: (l, 0, j)),
        out_shape=jax.ShapeDtypeStruct((depth, MOD_ROWS, n), F32),
        compiler_params=_params("parallel", "parallel"),
        name="mod_vectors",
    )(cvec, w_mod, b_mod.reshape(depth, 1, n))


def _ffn_body(x, mod_ref, wgu_ref, wdn_ref, g_ref, b_ref, acc_ref, mod_base, alpha):
    shift = mod_ref[0, mod_base:mod_base + 1, :]
    scale = mod_ref[0, mod_base + 1:mod_base + 2, :]
    gate = mod_ref[0, mod_base + 2:mod_base + 3, :]
    h = (x * (1.0 + scale) + shift).astype(BF16)
    d_ff = wdn_ref.shape[0]
    for c in range(d_ff // FF_CHUNK):
        lo = c * FF_CHUNK
        g = jnp.dot(h, wgu_ref[:, lo:lo + FF_CHUNK], preferred_element_type=F32)
        u = jnp.dot(h, wgu_ref[:, d_ff + lo:d_ff + lo + FF_CHUNK], preferred_element_type=F32)
        a = (g * jax.nn.sigmoid(g) * u).astype(BF16)
        y = jnp.dot(a, wdn_ref[lo:lo + FF_CHUNK, :], preferred_element_type=F32)
        if c == 0:
            acc_ref[...] = y
        else:
            acc_ref[...] += y
    r = alpha * x + (0.5 * gate) * acc_ref[...]
    return _layer_norm(r, g_ref[...], b_ref[...])


def _rope(x, cos, sin_next, sin_prev, quarter):
    width = x.shape[1]
    reps = width // cos.shape[1]
    if reps > 1:
        cos = jnp.concatenate([cos] * reps, axis=1)
        sin_next = jnp.concatenate([sin_next] * reps, axis=1)
        sin_prev = jnp.concatenate([sin_prev] * reps, axis=1)
    x_next = pltpu.roll(x, width - quarter, axis=1)
    x_prev = pltpu.roll(x, quarter, axis=1)
    return x * cos + x_next * sin_next + x_prev * sin_prev


def _head_rms(x, gsum_ref, w):
    x2 = x * x
    hi = x2.astype(BF16)
    lo = (x2 - hi.astype(F32)).astype(BF16)
    gs = gsum_ref[:x.shape[1], :x.shape[1]]
    ss = jnp.dot(hi, gs, preferred_element_type=F32) + jnp.dot(lo, gs, preferred_element_type=F32)
    return x * lax.rsqrt(ss * (1.0 / HEAD_DIM) + EPS) * w


def _store_k(k_ref, k):
    for kt in range(k.shape[1] // K_TILE):
        k_ref[0, kt] = k[:, kt * K_TILE:(kt + 1) * K_TILE].astype(BF16)


def _store_vt(vt_ref, v):
    vt = v.T
    tm = v.shape[0]
    for hd in range(v.shape[1] // HEAD_DIM):
        vt_ref[0, 0, hd, 0:HEAD_DIM, :] = vt[hd * HEAD_DIM:(hd + 1) * HEAD_DIM, :].astype(BF16)
        vt_ref[0, 0, hd, HEAD_DIM:V_ROWS, :] = jnp.ones((ONES_ROWS, tm), BF16)


def _ffn_inproj_kernel(x_ref, mod_ref, wgu_ref, wdn_ref, g_ref, b_ref, w_ref,
                       ca_ref, san_ref, sap_ref, ch_ref, shn_ref, shp_ref, qnw_ref, knw_ref, gsum_ref,
                       x1_ref, qa_ref, ka_ref, vat_ref, g3_ref, qc_ref, kc_ref, vct_ref, qd_ref, kd_ref, vdt_ref,
                       acc_ref, *, alpha):
    x = _ffn_body(x_ref[0], mod_ref, wgu_ref, wdn_ref, g_ref, b_ref, acc_ref, 0, alpha)
    x1_ref[0] = x
    shift = mod_ref[0, 3:4, :]
    scale = mod_ref[0, 4:5, :]
    h = (x * (1.0 + scale) + shift).astype(BF16)

    def seg(lo, hi):
        return jnp.dot(h, w_ref[:, lo:hi], preferred_element_type=F32)

    rope_a = functools.partial(_rope, cos=ca_ref[...], sin_next=san_ref[...], sin_prev=sap_ref[...],
                               quarter=A_QK_DIM // 4)
    rope_h = functools.partial(_rope, cos=ch_ref[...], sin_next=shn_ref[...], sin_prev=shp_ref[...],
                               quarter=HEAD_DIM // 4)
    qa_ref[0] = (rope_a(seg(OFF_AQ, OFF_AK)) * (LOG2E * A_QK_DIM ** -0.5)).astype(BF16)
    _store_k(ka_ref, rope_a(seg(OFF_AK, OFF_AV)))
    _store_vt(vat_ref, seg(OFF_AV, OFF_B))
    g3_ref[0] = seg(OFF_B, OFF_CQ)
    qc_ref[0] = (rope_h(seg(OFF_CQ, OFF_CK)) * (LOG2E * HEAD_DIM ** -0.5)).astype(BF16)
    kv = seg(OFF_CK, OFF_DQ)
    _store_k(kc_ref, rope_h(kv[:, :OFF_CV - OFF_CK]))
    _store_vt(vct_ref, kv[:, OFF_CV - OFF_CK:])
    qd = _head_rms(seg(OFF_DQ, OFF_DK), gsum_ref, qnw_ref[...])
    qd_ref[0] = (rope_h(qd) * (LOG2E * HEAD_DIM ** -0.5)).astype(BF16)
    kv = seg(OFF_DK, IN_W)
    kd = _head_rms(kv[:, :OFF_DV - OFF_DK], gsum_ref, knw_ref[...])
    _store_k(kd_ref, rope_h(kd))
    _store_vt(vdt_ref, kv[:, OFF_DV - OFF_DK:])


def _ffn_in_projection(x, mods, mod_row, w_gu, w_dn, ln_g, ln_b, alpha, w_in, layer, tabs_a, tabs_h, qnw, knw, gsum):
    bsz, t, d = x.shape
    d_ff = w_dn.shape[1]
    tm = min(TOK_TILE, t)
    nc = t // tm
    const = lambda b, i: (0, 0)
    tab = pl.BlockSpec((tm, 128), lambda b, i: (i, 0))
    q_spec = pl.BlockSpec((1, tm, 256), lambda b, i: (b, i, 0))

    def k_spec(w):
        return pl.BlockSpec((1, w // K_TILE, tm, K_TILE), lambda b, i: (b, 0, i, 0))

    cw = min(ATTN_CHUNK, t)
    per_chunk = cw // tm

    def vt_spec(w):
        return pl.BlockSpec((1, 1, w // HEAD_DIM, V_ROWS, tm),
                            lambda b, i: (b, i // per_chunk, 0, 0, i % per_chunk))

    def q_shape():
        return jax.ShapeDtypeStruct((bsz, t, 256), BF16)

    def k_shape(w):
        return jax.ShapeDtypeStruct((bsz, w // K_TILE, t, K_TILE), BF16)

    def vt_shape(w):
        return jax.ShapeDtypeStruct((bsz, t // cw, w // HEAD_DIM, V_ROWS, cw), BF16)

    x_spec = pl.BlockSpec((1, tm, d), lambda b, i: (b, i, 0))
    return pl.pallas_call(
        functools.partial(_ffn_inproj_kernel, alpha=alpha),
        grid=(bsz, nc),
        in_specs=[
            x_spec,
            pl.BlockSpec((None, 1, N_MOD, d), lambda b, i: (layer, mod_row(b), 0, 0)),
            pl.BlockSpec((None, d, 2 * d_ff), lambda b, i: (layer, 0, 0), pipeline_mode=pl.Buffered(1)),
            pl.BlockSpec((None, d_ff, d), lambda b, i: (layer, 0, 0), pipeline_mode=pl.Buffered(1)),
            pl.BlockSpec((1, d), const),
            pl.BlockSpec((1, d), const),
            pl.BlockSpec((None, d, IN_W), lambda b, i: (layer, 0, 0), pipeline_mode=pl.Buffered(1)),
            tab, tab, tab, tab, tab, tab,
            pl.BlockSpec((1, 256), const),
            pl.BlockSpec((1, 128), const),
            pl.BlockSpec((256, 256), const),
        ],
        out_specs=[x_spec, q_spec, k_spec(256), vt_spec(256),
                   pl.BlockSpec((1, tm, 768), lambda b, i: (b, i, 0)),
                   q_spec, k_spec(128), vt_spec(128),
                   q_spec, k_spec(128), vt_spec(128)],
        out_shape=[jax.ShapeDtypeStruct(x.shape, F32), q_shape(), k_shape(256), vt_shape(256),
                   jax.ShapeDtypeStruct((bsz, t, 768), F32),
                   q_shape(), k_shape(128), vt_shape(128),
                   q_shape(), k_shape(128), vt_shape(128)],
        scratch_shapes=[pltpu.VMEM((tm, d), F32)],
        compiler_params=_params("parallel", "parallel"),
        name="ffn_in_projection",
    )(x, mods, w_gu, w_dn, ln_g.reshape(1, d), ln_b.reshape(1, d), w_in, *tabs_a, *tabs_h, qnw, knw, gsum)


def _whole_per_batch(arr):
    zeros = (0,) * (arr.ndim - 1)
    return pl.BlockSpec((1,) + arr.shape[1:], lambda b, i: (b,) + zeros, pipeline_mode=pl.Buffered(1))


def _attn_scratch(n_maps, tq, ctx_rows, stage_rows):
    shapes = ([((n_maps, K_TILE, tq), BF16), ((n_maps, V_ROWS, tq), F32), ((n_maps, 8, tq), F32),
               ((n_maps, ctx_rows, tq), F32), ((n_maps, 8, tq), F32)]
              + [((stage_rows, tq), F32)] * STAGES + [((8, tq), F32)] * STAGES)
    return [pltpu.VMEM(shape, dtype) for shape, dtype in shapes]


def _key_offset(j, dq, grouped):
    return (j // 2) * dq if grouped else j * dq


def _fill_qpad(q_ref, qpad_ref, n_maps, dq, grouped):
    tq = q_ref.shape[1]
    qt = q_ref[0].astype(F32).T
    for j in range(n_maps):
        row = _key_offset(j, dq, grouped) % K_TILE
        parts = []
        if row:
            parts.append(jnp.zeros((row, tq), F32))
        parts.append(qt[j * dq:(j + 1) * dq, :])
        if K_TILE - row - dq:
            parts.append(jnp.zeros((K_TILE - row - dq, tq), F32))
        qpad_ref[j] = jnp.concatenate(parts, axis=0).astype(BF16)


def _score_unit(keys, j, nk, qpad_ref, s_ref, bm_ref, valid=None):
    tq = s_ref.shape[1]
    w = qpad_ref[j]
    bm = None
    for r in range(nk // SCORE_ROWS):
        rows = slice(r * SCORE_ROWS, (r + 1) * SCORE_ROWS)
        s = jnp.dot(keys(rows), w, preferred_element_type=F32)
        if valid is not None:
            s = jnp.where(valid(rows), s, NEG)
        s_ref[rows, :] = s
        part = jnp.max(s.reshape(SCORE_ROWS // 8, 8, tq), axis=0)
        bm = part if bm is None else jnp.maximum(bm, part)
    bm_ref[...] = bm


def _softmax_unit(values, j, nk, s_ref, bm_ref, m_ref, acc_ref, row0):
    m_old = m_ref[j, 0:1, :]
    m_new = jnp.maximum(m_old, jnp.max(bm_ref[...], axis=0, keepdims=True))
    alpha = jnp.exp2(m_old - m_new)
    pv = None
    for r in range(nk // PV_ROWS):
        rows = pl.ds(pl.multiple_of(row0 + r * PV_ROWS, PV_ROWS), PV_ROWS)
        p = jnp.exp2(s_ref[rows, :] - m_new)
        d = jnp.dot(values(slice(r * PV_ROWS, (r + 1) * PV_ROWS)), p.astype(BF16), preferred_element_type=F32)
        pv = d if pv is None else pv + d
    acc_ref[j] = alpha * acc_ref[j] + pv
    m_ref[j] = jnp.broadcast_to(m_new, m_ref.shape[1:])


def _init_state(m_ref, acc_ref, n_maps, sink_ref):
    tq = m_ref.shape[-1]
    for j in range(n_maps):
        if sink_ref is None:
            m_ref[j] = jnp.full(m_ref.shape[1:], NEG, F32)
            acc_ref[j] = jnp.zeros((V_ROWS, tq), F32)
        else:
            m_ref[j] = jnp.full(m_ref.shape[1:], sink_ref[j] * LOG2E, F32)
            acc_ref[j] = jnp.concatenate([jnp.zeros((HEAD_DIM, tq), F32), jnp.ones((ONES_ROWS, tq), F32)], axis=0)


def _normalised(acc_ref, j):
    acc = acc_ref[j]
    return acc[0:HEAD_DIM, :] / acc[HEAD_DIM:HEAD_DIM + 1, :]


def _finish_plain(o_ref, acc_ref, n_maps):
    ot = jnp.concatenate([_normalised(acc_ref, j) for j in range(n_maps)], axis=0)
    o_ref[0] = ot.T.astype(BF16)


def _finish_diff(o_ref, acc_ref, lamp_ref, subw_ref, n_maps, lambda_init):
    lp = lamp_ref[...]
    lam = (jnp.exp(jnp.sum(lp[0:1] * lp[1:2], axis=1, keepdims=True))
           - jnp.exp(jnp.sum(lp[2:3] * lp[3:4], axis=1, keepdims=True)) + lambda_init)
    outs = []
    for hd in range(n_maps // 2):
        o = _normalised(acc_ref, 2 * hd) - lam * _normalised(acc_ref, 2 * hd + 1)
        ms = jnp.mean(o * o, axis=0, keepdims=True)
        outs.append(o * lax.rsqrt(ms + EPS) * subw_ref[...] * (1.0 - lambda_init))
    o_ref[0] = jnp.concatenate(outs, axis=0).T.astype(BF16)


def _dense_attn_kernel(*refs, n_maps, dq, grouped, has_latent, has_sink, diff_lambda_init):
    refs = list(refs)
    q_ref, kc_ref, vct_ref, zero_ref = refs[:4]
    pos = 4
    k_ref = vt_ref = sink_ref = lamp_ref = subw_ref = None
    if has_latent:
        k_ref, vt_ref = refs[pos:pos + 2]
        pos += 2
    if has_sink:
        sink_ref = refs[pos]
        pos += 1
    if diff_lambda_init is not None:
        lamp_ref, subw_ref = refs[pos:pos + 2]
        pos += 2
    o_ref, qpad_ref, acc_ref, m_ref, cs_ref, cbm_ref = refs[pos:pos + 6]
    s_refs = refs[pos + 6:pos + 6 + STAGES]
    bm_refs = refs[pos + 6 + STAGES:pos + 6 + 2 * STAGES]

    _fill_qpad(q_ref, qpad_ref, n_maps, dq, grouped)
    _init_state(m_ref, acc_ref, n_maps, sink_ref)
    n_ctx = kc_ref.shape[2]
    row0 = zero_ref[0]

    for j in range(n_maps):
        kt = _key_offset(j, dq, grouped) // K_TILE
        _score_unit(lambda rows: kc_ref[0, kt, rows, :], j, n_ctx, qpad_ref, cs_ref.at[j], cbm_ref.at[j])

    if has_latent:
        tk = k_ref.shape[3]
        n_units = n_maps * k_ref.shape[2]

        def score(u, slot):
            j, c = u % n_maps, u // n_maps
            kt = _key_offset(j, dq, grouped) // K_TILE
            _score_unit(lambda rows: k_ref[0, kt, c, rows, :], j, tk, qpad_ref, s_refs[slot], bm_refs[slot])

        def softmax(u, slot):
            j, c = u % n_maps, u // n_maps
            _softmax_unit(lambda cols: vt_ref[0, c, j // 2, :, cols], j, tk, s_refs[slot], bm_refs[slot],
                          m_ref, acc_ref, row0)

        for u in range(LOOKAHEAD):
            score(u, u % STAGES)

    for j in range(n_maps):
        _softmax_unit(lambda cols: vct_ref[0, 0, j // 2, :, cols], j, n_ctx, cs_ref.at[j], cbm_ref.at[j],
                      m_ref, acc_ref, row0)

    if has_latent:
        per_iter = STAGES * math.gcd(n_units // STAGES, ROUNDS_PER_ITER)

        def body(i, carry):
            u0 = per_iter * i
            for k in range(per_iter):
                score(jnp.minimum(u0 + k + LOOKAHEAD, n_units - 1), (k + LOOKAHEAD) % STAGES)
                softmax(u0 + k, k % STAGES)
            return carry
        lax.fori_loop(0, n_units // per_iter, body, 0)
    if diff_lambda_init is None:
        _finish_plain(o_ref, acc_ref, n_maps)
    else:
        _finish_diff(o_ref, acc_ref, lamp_ref, subw_ref, n_maps, diff_lambda_init)


def _dense_attention(q, k_ctx, vt_ctx, k_lat, vt_lat, *, n_maps, dq, sink=None, lamp=None, subw=None,
                     diff_lambda_init=None):
    bsz, t, qw = q.shape
    key_tiles, n_ctx = k_ctx.shape[1:3]
    assert vt_ctx.shape[1] == 1 and vt_ctx.shape[-1] == n_ctx
    grouped = key_tiles * K_TILE != n_maps * dq
    tq = min(DENSE_Q_TILE, t)
    has_latent = k_lat is not None
    whole = _whole_per_batch
    assert n_maps % STAGES == 0 and n_ctx % PV_ROWS == 0

    in_specs = [pl.BlockSpec((1, tq, qw), lambda b, i: (b, i, 0)), whole(k_ctx), whole(vt_ctx),
                pl.BlockSpec(memory_space=pltpu.SMEM)]
    args = [q, k_ctx, vt_ctx, jnp.zeros((1,), jnp.int32)]
    stage_rows = 8
    if has_latent:
        n_chunks, tk = vt_lat.shape[1], vt_lat.shape[-1]
        assert tk % PV_ROWS == 0 and tk % SCORE_ROWS == 0 and n_chunks * tk == k_lat.shape[2]
        k_lat = k_lat.reshape(bsz, key_tiles, n_chunks, tk, K_TILE)
        in_specs += [whole(k_lat), whole(vt_lat)]
        args += [k_lat, vt_lat]
        stage_rows = tk
    if sink is not None:
        in_specs.append(pl.BlockSpec(memory_space=pltpu.SMEM))
        args.append(sink)
    if diff_lambda_init is not None:
        in_specs += [pl.BlockSpec(lamp.shape, lambda b, i: (0, 0)), pl.BlockSpec(subw.shape, lambda b, i: (0, 0))]
        args += [lamp, subw]
    scratch = _attn_scratch(n_maps, tq, n_ctx, stage_rows)
    return pl.pallas_call(
        functools.partial(_dense_attn_kernel, n_maps=n_maps, dq=dq, grouped=grouped, has_latent=has_latent,
                          has_sink=sink is not None, diff_lambda_init=diff_lambda_init),
        grid=(bsz, t // tq),
        in_specs=in_specs,
        out_specs=pl.BlockSpec((1, tq, 256), lambda b, i: (b, i, 0)),
        out_shape=jax.ShapeDtypeStruct((bsz, t, 256), BF16),
        scratch_shapes=scratch,
        compiler_params=_params("parallel", "parallel"),
        name="dense_attention",
    )(*args)


def _window_attn_kernel(q_ref, kp_ref, kc_ref, kn_ref, vtp_ref, vtc_ref, vtn_ref, kctx_ref, vtctx_ref, zero_ref,
                        sink_ref, o_ref, qpad_ref, acc_ref, m_ref, cs_ref, cbm_ref, *stage_refs, n_maps, seq):
    s_refs, bm_refs = stage_refs[:STAGES], stage_refs[STAGES:]
    tq = q_ref.shape[1]
    q0 = pl.program_id(1) * tq
    _fill_qpad(q_ref, qpad_ref, n_maps, HEAD_DIM, True)
    _init_state(m_ref, acc_ref, n_maps, sink_ref)
    k_win = jnp.concatenate([kp_ref[0], kc_ref[0], kn_ref[0]], axis=0)
    vt_win = jnp.concatenate([vtp_ref[0, 0], vtc_ref[0, 0], vtn_ref[0, 0]], axis=2)
    n_ctx, n_win = kctx_ref.shape[2], k_win.shape[0]
    row0 = zero_ref[0]

    def in_window(rows):
        kpos = q0 - WINDOW + rows.start + lax.broadcasted_iota(jnp.int32, (rows.stop - rows.start, tq), 0)
        qpos = q0 + lax.broadcasted_iota(jnp.int32, (rows.stop - rows.start, tq), 1)
        return (jnp.abs(kpos - qpos) <= WINDOW) & (kpos >= 0) & (kpos < seq)

    for j in range(n_maps):
        _score_unit(lambda rows: kctx_ref[0, 0, rows, :], j, n_ctx, qpad_ref, cs_ref.at[j], cbm_ref.at[j])
    for j in range(n_maps):
        _score_unit(lambda rows: k_win[rows, :], j, n_win, qpad_ref, s_refs[j], bm_refs[j], valid=in_window)
    for j in range(n_maps):
        _softmax_unit(lambda cols: vtctx_ref[0, 0, j // 2, :, cols], j, n_ctx, cs_ref.at[j], cbm_ref.at[j],
                      m_ref, acc_ref, row0)
    for j in range(n_maps):
        _softmax_unit(lambda cols: vt_win[j // 2][:, cols], j, n_win, s_refs[j], bm_refs[j], m_ref, acc_ref, row0)
    _finish_plain(o_ref, acc_ref, n_maps)


def _window_attention(q, k_ctx, vt_ctx, k_lat, vt_lat, sink):
    bsz, t, qw = q.shape
    key_tiles, _, kw = k_lat.shape[1:]
    assert key_tiles == 1
    tq = Q_TILE
    n_maps = 4
    k_flat = k_lat.reshape(bsz, t, kw)
    wb = tq // WINDOW
    nwb = t // WINDOW
    heads, cw = vt_lat.shape[2], vt_lat.shape[-1]
    whole = _whole_per_batch

    def vt_blk(width, blk):
        per_chunk = cw // width
        return pl.BlockSpec((1, 1, heads, V_ROWS, width),
                            lambda b, i: (b, blk(i) // per_chunk, 0, 0, blk(i) % per_chunk))

    def prev_blk(i):
        return jnp.maximum(i * wb - 1, 0)

    def next_blk(i):
        return jnp.minimum(i * wb + wb, nwb - 1)

    in_specs = [
        pl.BlockSpec((1, tq, qw), lambda b, i: (b, i, 0)),
        pl.BlockSpec((1, WINDOW, kw), lambda b, i: (b, prev_blk(i), 0)),
        pl.BlockSpec((1, tq, kw), lambda b, i: (b, i, 0)),
        pl.BlockSpec((1, WINDOW, kw), lambda b, i: (b, next_blk(i), 0)),
        vt_blk(WINDOW, prev_blk),
        vt_blk(tq, lambda i: i),
        vt_blk(WINDOW, next_blk),
        whole(k_ctx), whole(vt_ctx),
        pl.BlockSpec(memory_space=pltpu.SMEM),
        pl.BlockSpec(memory_space=pltpu.SMEM),
    ]
    assert n_maps == STAGES
    return pl.pallas_call(
        functools.partial(_window_attn_kernel, n_maps=n_maps, seq=t),
        grid=(bsz, t // tq),
        in_specs=in_specs,
        out_specs=pl.BlockSpec((1, tq, 256), lambda b, i: (b, i, 0)),
        out_shape=jax.ShapeDtypeStruct((bsz, t, 256), BF16),
        scratch_shapes=_attn_scratch(n_maps, tq, k_ctx.shape[2], tq + 2 * WINDOW),
        compiler_params=_params("parallel", "parallel"),
        name="window_attention",
    )(q, k_flat, k_flat, k_flat, vt_lat, vt_lat, vt_lat, k_ctx, vt_ctx, jnp.zeros((1,), jnp.int32), sink)


def _outproj_ffn_kernel(x_ref, mod_ref, ya_ref, g3_ref, g3p_ref, g3n_ref, yw_ref, yg_ref, cw_ref, w_ref, g_ref, b_ref,
                        wgu_ref, wdn_ref, g2_ref, b2_ref, o_ref, acc_ref, *, alpha):
    i = pl.program_id(1)
    last = pl.num_programs(1) - 1
    tm = x_ref.shape[1]
    gb = g3_ref[0, :, 0:256]
    hid = g3_ref[0, :, 256:512] * g3_ref[0, :, 512:768]
    halo = g3p_ref.shape[1]
    h_prev = g3p_ref[0, halo - 1:halo, 256:512] * g3p_ref[0, halo - 1:halo, 512:768]
    h_next = g3n_ref[0, 0:1, 256:512] * g3n_ref[0, 0:1, 512:768]
    h_prev = jnp.where(i == 0, 0.0, h_prev)
    h_next = jnp.where(i == last, 0.0, h_next)
    row = lax.broadcasted_iota(jnp.int32, hid.shape, 0)
    below = jnp.where(row == 0, h_prev, pltpu.roll(hid, 1, axis=0))
    above = jnp.where(row == tm - 1, h_next, pltpu.roll(hid, tm - 1, axis=0))
    yb = gb * (cw_ref[0:1, :] * below + cw_ref[1:2, :] * hid + cw_ref[2:3, :] * above)
    y = jnp.dot(ya_ref[0], w_ref[0:256, :], preferred_element_type=F32)
    y += jnp.dot(yb.astype(BF16), w_ref[256:512, :], preferred_element_type=F32)
    y += jnp.dot(yw_ref[0], w_ref[512:768, :], preferred_element_type=F32)
    y += jnp.dot(yg_ref[0], w_ref[768:1024, :], preferred_element_type=F32)
    gate = mod_ref[0, 5:6, :]
    x_mix = _layer_norm(alpha * x_ref[0] + gate * y, g_ref[...], b_ref[...])
    o_ref[0] = _ffn_body(x_mix, mod_ref, wgu_ref, wdn_ref, g2_ref, b2_ref, acc_ref, 6, alpha)


def _out_projection_ffn(x, mods, mod_row, ya, g3, yw, yg, conv_w, w_out, w_gu, w_dn, layer, ln_g, ln_b, ln_g2, ln_b2,
                        alpha):
    bsz, t, d = x.shape
    d_ff = w_dn.shape[1]
    tm = min(TOK_TILE, t)
    halo = 8
    hb = tm // halo
    n_halo = t // halo
    const = lambda b, i: (0, 0)
    y_spec = pl.BlockSpec((1, tm, 256), lambda b, i: (b, i, 0))
    return pl.pallas_call(
        functools.partial(_outproj_ffn_kernel, alpha=alpha),
        grid=(bsz, t // tm),
        in_specs=[
            pl.BlockSpec((1, tm, d), lambda b, i: (b, i, 0)),
            pl.BlockSpec((None, 1, N_MOD, d), lambda b, i: (layer, mod_row(b), 0, 0)),
            y_spec,
            pl.BlockSpec((1, tm, 768), lambda b, i: (b, i, 0)),
            pl.BlockSpec((1, halo, 768), lambda b, i: (b, jnp.maximum(i * hb - 1, 0), 0)),
            pl.BlockSpec((1, halo, 768), lambda b, i: (b, jnp.minimum(i * hb + hb, n_halo - 1), 0)),
            y_spec, y_spec,
            pl.BlockSpec(conv_w.shape, const),
            pl.BlockSpec((None,) + w_out.shape[1:], lambda b, i: (layer, 0, 0), pipeline_mode=pl.Buffered(1)),
            pl.BlockSpec((1, d), const),
            pl.BlockSpec((1, d), const),
            pl.BlockSpec((None, d, 2 * d_ff), lambda b, i: (layer, 0, 0), pipeline_mode=pl.Buffered(1)),
            pl.BlockSpec((None, d_ff, d), lambda b, i: (layer, 0, 0), pipeline_mode=pl.Buffered(1)),
            pl.BlockSpec((1, d), const),
            pl.BlockSpec((1, d), const),
        ],
        out_specs=pl.BlockSpec((1, tm, d), lambda b, i: (b, i, 0)),
        out_shape=jax.ShapeDtypeStruct(x.shape, F32),
        scratch_shapes=[pltpu.VMEM((tm, d), F32)],
        compiler_params=_params("parallel", "parallel"),
        name="out_projection_ffn",
    )(x, mods, ya, g3, g3, g3, yw, yg, conv_w, w_out, ln_g.reshape(1, d), ln_b.reshape(1, d), w_gu, w_dn,
      ln_g2.reshape(1, d), ln_b2.reshape(1, d))


def _rope_tables(rows, dim):
    n_freq = dim // 4
    inv_freq = ROPE_BASE ** (-jnp.arange(n_freq, dtype=F32) / n_freq)
    ang_r = jnp.arange(rows, dtype=F32)[:, None] * inv_freq
    ang_c = jnp.arange(GRID_W, dtype=F32)[:, None] * inv_freq

    def on_grid(of_row, of_col):
        r = jnp.broadcast_to(of_row[:, None, :], (rows, GRID_W, n_freq))
        c = jnp.broadcast_to(of_col[None, :, :], (rows, GRID_W, n_freq))
        return jnp.concatenate([r, r, c, c], axis=-1).reshape(rows * GRID_W, dim)

    cos = on_grid(jnp.cos(ang_r), jnp.cos(ang_c))
    sin = on_grid(jnp.sin(ang_r), jnp.sin(ang_c))
    reps = 128 // dim
    cos = jnp.tile(cos, (1, reps))
    sin = jnp.tile(sin, (1, reps))
    first_half = (jnp.arange(128) % (dim // 2)) < (dim // 4)
    sin_next = jnp.where(first_half, -sin, 0.0)
    sin_prev = jnp.where(first_half, 0.0, sin)
    return cos, sin_next, sin_prev


def _identity_tables(t):
    return jnp.ones((t, 128), F32), jnp.zeros((t, 128), F32), jnp.zeros((t, 128), F32)


def kernel(x, c, ctx, c_ctx, w_mod, b_mod, ln_g, ln_b, w_gu1, w_dn1, w_in, w_out, conv_w, lam_q1, lam_k1, lam_q2,
           lam_k2, subln_w, sink, qn_w, kn_w, w_gu2, w_dn2):
    bsz, seq, d = x.shape
    ctx_len = ctx.shape[1]
    depth = w_mod.shape[0]
    alpha = (2.0 * depth) ** 0.25
    assert seq % TOK_TILE == 0 and ctx_len % Q_TILE == 0 and bsz < MOD_ROWS

    tabs_a = _rope_tables(seq // GRID_W, A_QK_DIM)
    tabs_h = _rope_tables(seq // GRID_W, HEAD_DIM)
    tabs_id = _identity_tables(ctx_len)
    gsum = (jnp.arange(256)[:, None] // HEAD_DIM == jnp.arange(256)[None, :] // HEAD_DIM).astype(BF16)

    cvec = jnp.zeros((MOD_ROWS, d), F32).at[:bsz].set(c).at[bsz].set(c_ctx)
    mods_all = _mod_vectors(cvec, w_mod, b_mod).reshape(depth, MOD_ROWS, N_MOD, d)
    lat_row = lambda b: b
    ctx_row = lambda b: bsz

    wgu1, wdn1, wgu2, wdn2 = (w.astype(BF16) for w in (w_gu1, w_dn1, w_gu2, w_dn2))
    win, wout = w_in.astype(BF16), w_out.astype(BF16)

    xc = ctx
    for l in range(depth):
        need_ctx = l < depth - 1
        lambda_init = 0.8 - 0.6 * math.exp(-0.3 * l)
        mods = mods_all
        qnw = jnp.tile(qn_w[l], 256 // HEAD_DIM).reshape(1, 256)
        knw = jnp.tile(kn_w[l], 128 // HEAD_DIM).reshape(1, 128)
        lamp = jnp.stack([lam_q1[l], lam_k1[l], lam_q2[l], lam_k2[l]])
        subw = subln_w[l].reshape(HEAD_DIM, 1)

        x, qa, ka, vat, g3, qc, kc, vct, qd, kd, vdt = _ffn_in_projection(
            x, mods, lat_row, wgu1, wdn1, ln_g[l, 0], ln_b[l, 0], alpha, win, l, tabs_a, tabs_h, qnw, knw, gsum)
        xc, qa_c, ka_c, vat_c, g3_c, qc_c, kc_c, vct_c, qd_c, kd_c, vdt_c = _ffn_in_projection(
            xc, mods, ctx_row, wgu1, wdn1, ln_g[l, 0], ln_b[l, 0], alpha, win, l, tabs_id, tabs_id, qnw, knw, gsum)

        ya = _dense_attention(qa, ka_c, vat_c, ka, vat, n_maps=8, dq=A_QK_DIM, lamp=lamp, subw=subw,
                              diff_lambda_init=lambda_init)
        yw = _window_attention(qc, kc_c, vct_c, kc, vct, sink[l])
        yg = _dense_attention(qd, kd_c, vdt_c, kd, vdt, n_maps=4, dq=HEAD_DIM)
        x = _out_projection_ffn(x, mods, lat_row, ya, g3, yw, yg, conv_w[l], wout, wgu2, wdn2, l,
                                ln_g[l, 1], ln_b[l, 1], ln_g[l, 2], ln_b[l, 2], alpha)

        if need_ctx:
            ya_c = _dense_attention(qa_c, ka_c, vat_c, None, None, n_maps=8, dq=A_QK_DIM, lamp=lamp, subw=subw,
                                    diff_lambda_init=lambda_init)
            yw_c = _dense_attention(qc_c, kc_c, vct_c, None, None, n_maps=4, dq=HEAD_DIM, sink=sink[l])
            yg_c = _dense_attention(qd_c, kd_c, vdt_c, None, None, n_maps=4, dq=HEAD_DIM)
            xc = _out_projection_ffn(xc, mods, ctx_row, ya_c, g3_c, yw_c, yg_c, conv_w[l], wout, wgu2, wdn2, l,
                                     ln_g[l, 1], ln_b[l, 1], ln_g[l, 2], ln_b[l, 2], alpha)
    return x
```

```python
import functools
import math

import jax
import jax.numpy as jnp
from jax import lax
from jax.experimental import pallas as pl
from jax.experimental.pallas import tpu as pltpu

F32 = jnp.float32
BF16 = jnp.bfloat16

GRID_W = 64
HEAD_DIM = 64
WINDOW = 128
ROPE_BASE = 10000.0
EPS = 1e-6
NEG = -1e30
A_QK_DIM = 32
N_MOD = 9
MOD_ROWS = 8
VMEM_LIMIT = 56 * 1024 * 1024

OFF_AQ, OFF_AK, OFF_AV = 0, 256, 512
OFF_B = 768
OFF_CQ, OFF_CK, OFF_CV = 1536, 1792, 1920
OFF_DQ, OFF_DK, OFF_DV = 2048, 2304, 2432
IN_W = 2560

TOK_TILE = 512
Q_TILE = 256
DENSE_Q_TILE = 256
FF_CHUNK = 256
SCORE_ROWS = 128
PV_ROWS = 256
ATTN_CHUNK = 1024
STAGES = 4
LOOKAHEAD = 2
ROUNDS_PER_ITER = 8
K_TILE = 128
ONES_ROWS = 16
V_ROWS = HEAD_DIM + ONES_ROWS
LOG2E = math.log2(math.e)


def _params(*sem):
    return pltpu.CompilerParams(dimension_semantics=sem, vmem_limit_bytes=VMEM_LIMIT)


def _layer_norm(r, g, b):
    mu = jnp.mean(r, axis=-1, keepdims=True)
    d = r - mu
    var = jnp.mean(d * d, axis=-1, keepdims=True)
    return d * lax.rsqrt(var + EPS) * g + b


def _mod_kernel(c_ref, w_ref, b_ref, o_ref):
    c = c_ref[...]
    a = (c * jax.nn.sigmoid(c)).astype(BF16)
    o_ref[0] = jnp.dot(a, w_ref[0].astype(BF16), preferred_element_type=F32) + b_ref[0]


def _mod_vectors(cvec, w_mod, b_mod):
    depth, d, n = w_mod.shape
    tn = n // 8
    return pl.pallas_call(
        _mod_kernel,
        grid=(depth, n // tn),
        in_specs=[
            pl.BlockSpec((MOD_ROWS, d), lambda l, j: (0, 0)),
            pl.BlockSpec((1, d, tn), lambda l, j: (l, 0, j)),
            pl.BlockSpec((1, 1, tn), lambda l, j: (l, 0, j)),
        ],
        out_specs=pl.BlockSpec((1, MOD_ROWS, tn), lambda l, j: (l, 0, j)),
        out_shape=jax.ShapeDtypeStruct((depth, MOD_ROWS, n), F32),
        compiler_params=_params("parallel", "parallel"),
        name="mod_vectors",
    )(cvec, w_mod, b_mod.reshape(depth, 1, n))


def _ffn_body(x, mod_ref, wgu_ref, wdn_ref, g_ref, b_ref, acc_ref, mod_base, alpha):
    shift = mod_ref[0, mod_base:mod_base + 1, :]
    scale = mod_ref[0, mod_base + 1:mod_base + 2, :]
    gate = mod_ref[0, mod_base + 2:mod_base + 3, :]
    h = (x * (1.0 + scale) + shift).astype(BF16)
    d_ff = wdn_ref.shape[0]
    for c in range(d_ff // FF_CHUNK):
        lo = c * FF_CHUNK
        g = jnp.dot(h, wgu_ref[:, lo:lo + FF_CHUNK], preferred_element_type=F32)
        u = jnp.dot(h, wgu_ref[:, d_ff + lo:d_ff + lo + FF_CHUNK], preferred_element_type=F32)
        a = (g * jax.nn.sigmoid(g) * u).astype(BF16)
        y = jnp.dot(a, wdn_ref[lo:lo + FF_CHUNK, :], preferred_element_type=F32)
        if c == 0:
            acc_ref[...] = y
        else:
            acc_ref[...] += y
    r = alpha * x + (0.5 * gate) * acc_ref[...]
    return _layer_norm(r, g_ref[...], b_ref[...])


def _rope(x, cos, sin_next, sin_prev, quarter):
    width = x.shape[1]
    reps = width // cos.shape[1]
    if reps > 1:
        cos = jnp.concatenate([cos] * reps, axis=1)
        sin_next = jnp.concatenate([sin_next] * reps, axis=1)
        sin_prev = jnp.concatenate([sin_prev] * reps, axis=1)
    x_next = pltpu.roll(x, width - quarter, axis=1)
    x_prev = pltpu.roll(x, quarter, axis=1)
    return x * cos + x_next * sin_next + x_prev * sin_prev


def _head_rms(x, gsum_ref, w):
    x2 = x * x
    hi = x2.astype(BF16)
    lo = (x2 - hi.astype(F32)).astype(BF16)
    gs = gsum_ref[:x.shape[1], :x.shape[1]]
    ss = jnp.dot(hi, gs, preferred_element_type=F32) + jnp.dot(lo, gs, preferred_element_type=F32)
    return x * lax.rsqrt(ss * (1.0 / HEAD_DIM) + EPS) * w


def _store_k(k_ref, k):
    for kt in range(k.shape[1] // K_TILE):
        k_ref[0, kt] = k[:, kt * K_TILE:(kt + 1) * K_TILE].astype(BF16)


def _store_vt(vt_ref, v):
    vt = v.T
    tm = v.shape[0]
    for hd in range(v.shape[1] // HEAD_DIM):
        vt_ref[0, 0, hd, 0:HEAD_DIM, :] = vt[hd * HEAD_DIM:(hd + 1) * HEAD_DIM, :].astype(BF16)
        vt_ref[0, 0, hd, HEAD_DIM:V_ROWS, :] = jnp.ones((ONES_ROWS, tm), BF16)


def _ffn_inproj_kernel(x_ref, mod_ref, wgu_ref, wdn_ref, g_ref, b_ref, w_ref,
                       ca_ref, san_ref, sap_ref, ch_ref, shn_ref, shp_ref, qnw_ref, knw_ref, gsum_ref,
                       x1_ref, qa_ref, ka_ref, vat_ref, g3_ref, qc_ref, kc_ref, vct_ref, qd_ref, kd_ref, vdt_ref,
                       acc_ref, *, alpha):
    x = _ffn_body(x_ref[0], mod_ref, wgu_ref, wdn_ref, g_ref, b_ref, acc_ref, 0, alpha)
    x1_ref[0] = x
    shift = mod_ref[0, 3:4, :]
    scale = mod_ref[0, 4:5, :]
    h = (x * (1.0 + scale) + shift).astype(BF16)

    def seg(lo, hi):
        return jnp.dot(h, w_ref[:, lo:hi], preferred_element_type=F32)

    rope_a = functools.partial(_rope, cos=ca_ref[...], sin_next=san_ref[...], sin_prev=sap_ref[...],
                               quarter=A_QK_DIM // 4)
    rope_h = functools.partial(_rope, cos=ch_ref[...], sin_next=shn_ref[...], sin_prev=shp_ref[...],
                               quarter=HEAD_DIM // 4)
    qa_ref[0] = (rope_a(seg(OFF_AQ, OFF_AK)) * (LOG2E * A_QK_DIM ** -0.5)).astype(BF16)
    _store_k(ka_ref, rope_a(seg(OFF_AK, OFF_AV)))
    _store_vt(vat_ref, seg(OFF_AV, OFF_B))
    qc_ref[0] = (rope_h(seg(OFF_CQ, OFF_CK)) * (LOG2E * HEAD_DIM ** -0.5)).astype(BF16)
    kv = seg(OFF_CK, OFF_DQ)
    _store_k(kc_ref, rope_h(kv[:, :OFF_CV - OFF_CK]))
    _store_vt(vct_ref, kv[:, OFF_CV - OFF_CK:])
    qd = _head_rms(seg(OFF_DQ, OFF_DK), gsum_ref, qnw_ref[...])
    qd_ref[0] = (rope_h(qd) * (LOG2E * HEAD_DIM ** -0.5)).astype(BF16)
    kv = seg(OFF_DK, IN_W)
    kd = _head_rms(kv[:, :OFF_DV - OFF_DK], gsum_ref, knw_ref[...])
    _store_k(kd_ref, rope_h(kd))
    _store_vt(vdt_ref, kv[:, OFF_DV - OFF_DK:])
    g3_ref[0] = seg(OFF_B, OFF_CQ)


def _ffn_in_projection(x, mods, mod_row, w_gu, w_dn, ln_g, ln_b, alpha, w_in, layer, tabs_a, tabs_h, qnw, knw, gsum):
    bsz, t, d = x.shape
    d_ff = w_dn.shape[1]
    tm = min(TOK_TILE, t)
    nc = t // tm
    const = lambda b, i: (0, 0)
    tab = pl.BlockSpec((tm, 128), lambda b, i: (i, 0))
    q_spec = pl.BlockSpec((1, tm, 256), lambda b, i: (b, i, 0))

    def k_spec(w):
        return pl.BlockSpec((1, w // K_TILE, tm, K_TILE), lambda b, i: (b, 0, i, 0))

    cw = min(ATTN_CHUNK, t)
    per_chunk = cw // tm

    def vt_spec(w):
        return pl.BlockSpec((1, 1, w // HEAD_DIM, V_ROWS, tm),
                            lambda b, i: (b, i // per_chunk, 0, 0, i % per_chunk))

    def q_shape():
        return jax.ShapeDtypeStruct((bsz, t, 256), BF16)

    def k_shape(w):
        return jax.ShapeDtypeStruct((bsz, w // K_TILE, t, K_TILE), BF16)

    def vt_shape(w):
        return jax.ShapeDtypeStruct((bsz, t // cw, w // HEAD_DIM, V_ROWS, cw), BF16)

    x_spec = pl.BlockSpec((1, tm, d), lambda b, i: (b, i, 0))
    return pl.pallas_call(
        functools.partial(_ffn_inproj_kernel, alpha=alpha),
        grid=(bsz, nc),
        in_specs=[
            x_spec,
            pl.BlockSpec((None, 1, N_MOD, d), lambda b, i: (layer, mod_row(b), 0, 0)),
            pl.BlockSpec((None, d, 2 * d_ff), lambda b, i: (layer, 0, 0), pipeline_mode=pl.Buffered(1)),
            pl.BlockSpec((None, d_ff, d), lambda b, i: (layer, 0, 0), pipeline_mode=pl.Buffered(1)),
            pl.BlockSpec((1, d), const),
            pl.BlockSpec((1, d), const),
            pl.BlockSpec((None, d, IN_W), lambda b, i: (layer, 0, 0), pipeline_mode=pl.Buffered(1)),
            tab, tab, tab, tab, tab, tab,
            pl.BlockSpec((1, 256), const),
            pl.BlockSpec((1, 128), const),
            pl.BlockSpec((256, 256), const),
        ],
        out_specs=[x_spec, q_spec, k_spec(256), vt_spec(256),
                   pl.BlockSpec((1, tm, 768), lambda b, i: (b, i, 0)),
                   q_spec, k_spec(128), vt_spec(128),
                   q_spec, k_spec(128), vt_spec(128)],
        out_shape=[jax.ShapeDtypeStruct(x.shape, F32), q_shape(), k_shape(256), vt_shape(256),
                   jax.ShapeDtypeStruct((bsz, t, 768), F32),
                   q_shape(), k_shape(128), vt_shape(128),
                   q_shape(), k_shape(128), vt_shape(128)],
        scratch_shapes=[pltpu.VMEM((tm, d), F32)],
        compiler_params=_params("parallel", "parallel"),
        name="ffn_in_projection",
    )(x, mods, w_gu, w_dn, ln_g.reshape(1, d), ln_b.reshape(1, d), w_in, *tabs_a, *tabs_h, qnw, knw, gsum)


def _whole_per_batch(arr):
    zeros = (0,) * (arr.ndim - 1)
    return pl.BlockSpec((1,) + arr.shape[1:], lambda b, i: (b,) + zeros, pipeline_mode=pl.Buffered(1))


def _attn_scratch(n_maps, tq, ctx_rows, stage_rows):
    shapes = ([((n_maps, K_TILE, tq), BF16), ((n_maps, V_ROWS, tq), F32), ((n_maps, 8, tq), F32),
               ((n_maps, ctx_rows, tq), F32), ((n_maps, 8, tq), F32)]
              + [((stage_rows, tq), F32)] * STAGES + [((8, tq), F32)] * STAGES)
    return [pltpu.VMEM(shape, dtype) for shape, dtype in shapes]


def _key_offset(j, dq, grouped):
    return (j // 2) * dq if grouped else j * dq


def _fill_qpad(q_ref, qpad_ref, n_maps, dq, grouped):
    tq = q_ref.shape[1]
    qt = q_ref[0].astype(F32).T
    for j in range(n_maps):
        row = _key_offset(j, dq, grouped) % K_TILE
        parts = []
        if row:
            parts.append(jnp.zeros((row, tq), F32))
        parts.append(qt[j * dq:(j + 1) * dq, :])
        if K_TILE - row - dq:
            parts.append(jnp.zeros((K_TILE - row - dq, tq), F32))
        qpad_ref[j] = jnp.concatenate(parts, axis=0).astype(BF16)


def _score_unit(keys, j, nk, qpad_ref, s_ref, bm_ref, valid=None):
    tq = s_ref.shape[1]
    w = qpad_ref[j]
    bm = None
    for r in range(nk // SCORE_ROWS):
        rows = slice(r * SCORE_ROWS, (r + 1) * SCORE_ROWS)
        s = jnp.dot(keys(rows), w, preferred_element_type=F32)
        if valid is not None:
            s = jnp.where(valid(rows), s, NEG)
        s_ref[rows, :] = s
        part = jnp.max(s.reshape(SCORE_ROWS // 8, 8, tq), axis=0)
        bm = part if bm is None else jnp.maximum(bm, part)
    bm_ref[...] = bm


def _softmax_unit(values, j, nk, s_ref, bm_ref, m_ref, acc_ref, row0):
    m_old = m_ref[j, 0:1, :]
    m_new = jnp.maximum(m_old, jnp.max(bm_ref[...], axis=0, keepdims=True))
    alpha = jnp.exp2(m_old - m_new)
    pv = None
    for r in range(nk // PV_ROWS):
        rows = pl.ds(pl.multiple_of(row0 + r * PV_ROWS, PV_ROWS), PV_ROWS)
        p = jnp.exp2(s_ref[rows, :] - m_new)
        d = jnp.dot(values(slice(r * PV_ROWS, (r + 1) * PV_ROWS)), p.astype(BF16), preferred_element_type=F32)
        pv = d if pv is None else pv + d
    acc_ref[j] = alpha * acc_ref[j] + pv
    m_ref[j] = jnp.broadcast_to(m_new, m_ref.shape[1:])


def _init_state(m_ref, acc_ref, n_maps, sink_ref):
    tq = m_ref.shape[-1]
    for j in range(n_maps):
        if sink_ref is None:
            m_ref[j] = jnp.full(m_ref.shape[1:], NEG, F32)
            acc_ref[j] = jnp.zeros((V_ROWS, tq), F32)
        else:
            m_ref[j] = jnp.full(m_ref.shape[1:], sink_ref[j] * LOG2E, F32)
            acc_ref[j] = jnp.concatenate([jnp.zeros((HEAD_DIM, tq), F32), jnp.ones((ONES_ROWS, tq), F32)], axis=0)


def _normalised(acc_ref, j):
    acc = acc_ref[j]
    return acc[0:HEAD_DIM, :] / acc[HEAD_DIM:HEAD_DIM + 1, :]


def _finish_plain(o_ref, acc_ref, n_maps):
    ot = jnp.concatenate([_normalised(acc_ref, j) for j in range(n_maps)], axis=0)
    o_ref[0] = ot.T.astype(BF16)


def _finish_diff(o_ref, acc_ref, lamp_ref, subw_ref, n_maps, lambda_init):
    lp = lamp_ref[...]
    lam = (jnp.exp(jnp.sum(lp[0:1] * lp[1:2], axis=1, keepdims=True))
           - jnp.exp(jnp.sum(lp[2:3] * lp[3:4], axis=1, keepdims=True)) + lambda_init)
    outs = []
    for hd in range(n_maps // 2):
        o = _normalised(acc_ref, 2 * hd) - lam * _normalised(acc_ref, 2 * hd + 1)
        ms = jnp.mean(o * o, axis=0, keepdims=True)
        outs.append(o * lax.rsqrt(ms + EPS) * subw_ref[...] * (1.0 - lambda_init))
    o_ref[0] = jnp.concatenate(outs, axis=0).T.astype(BF16)


def _dense_attn_kernel(*refs, n_maps, dq, grouped, has_latent, has_sink, diff_lambda_init):
    refs = list(refs)
    q_ref, kc_ref, vct_ref, zero_ref = refs[:4]
    pos = 4
    k_ref = vt_ref = sink_ref = lamp_ref = subw_ref = None
    if has_latent:
        k_ref, vt_ref = refs[pos:pos + 2]
        pos += 2
    if has_sink:
        sink_ref = refs[pos]
        pos += 1
    if diff_lambda_init is not None:
        lamp_ref, subw_ref = refs[pos:pos + 2]
        pos += 2
    o_ref, qpad_ref, acc_ref, m_ref, cs_ref, cbm_ref = refs[pos:pos + 6]
    s_refs = refs[pos + 6:pos + 6 + STAGES]
    bm_refs = refs[pos + 6 + STAGES:pos + 6 + 2 * STAGES]

    _fill_qpad(q_ref, qpad_ref, n_maps, dq, grouped)
    _init_state(m_ref, acc_ref, n_maps, sink_ref)
    n_ctx = kc_ref.shape[2]
    row0 = zero_ref[0]

    for j in range(n_maps):
        kt = _key_offset(j, dq, grouped) // K_TILE
        _score_unit(lambda rows: kc_ref[0, kt, rows, :], j, n_ctx, qpad_ref, cs_ref.at[j], cbm_ref.at[j])

    if has_latent:
        tk = k_ref.shape[3]
        n_units = n_maps * k_ref.shape[2]

        def score(u, slot):
            j, c = u % n_maps, u // n_maps
            kt = _key_offset(j, dq, grouped) // K_TILE
            _score_unit(lambda rows: k_ref[0, kt, c, rows, :], j, tk, qpad_ref, s_refs[slot], bm_refs[slot])

        def softmax(u, slot):
            j, c = u % n_maps, u // n_maps
            _softmax_unit(lambda cols: vt_ref[0, c, j // 2, :, cols], j, tk, s_refs[slot], bm_refs[slot],
                          m_ref, acc_ref, row0)

        for u in range(LOOKAHEAD):
            score(u, u % STAGES)

    for j in range(n_maps):
        _softmax_unit(lambda cols: vct_ref[0, 0, j // 2, :, cols], j, n_ctx, cs_ref.at[j], cbm_ref.at[j],
                      m_ref, acc_ref, row0)

    if has_latent:
        per_iter = STAGES * math.gcd(n_units // STAGES, ROUNDS_PER_ITER)

        def body(i, carry):
            u0 = per_iter * i
            for k in range(per_iter):
                score(jnp.minimum(u0 + k + LOOKAHEAD, n_units - 1), (k + LOOKAHEAD) % STAGES)
                softmax(u0 + k, k % STAGES)
            return carry
        lax.fori_loop(0, n_units // per_iter, body, 0)
    if diff_lambda_init is None:
        _finish_plain(o_ref, acc_ref, n_maps)
    else:
        _finish_diff(o_ref, acc_ref, lamp_ref, subw_ref, n_maps, diff_lambda_init)


def _dense_attention(q, k_ctx, vt_ctx, k_lat, vt_lat, *, n_maps, dq, sink=None, lamp=None, subw=None,
                     diff_lambda_init=None):
    bsz, t, qw = q.shape
    key_tiles, n_ctx = k_ctx.shape[1:3]
    assert vt_ctx.shape[1] == 1 and vt_ctx.shape[-1] == n_ctx
    grouped = key_tiles * K_TILE != n_maps * dq
    tq = min(DENSE_Q_TILE, t)
    has_latent = k_lat is not None
    whole = _whole_per_batch
    assert n_maps % STAGES == 0 and n_ctx % PV_ROWS == 0

    in_specs = [pl.BlockSpec((1, tq, qw), lambda b, i: (b, i, 0)), whole(k_ctx), whole(vt_ctx),
                pl.BlockSpec(memory_space=pltpu.SMEM)]
    args = [q, k_ctx, vt_ctx, jnp.zeros((1,), jnp.int32)]
    stage_rows = 8
    if has_latent:
        n_chunks, tk = vt_lat.shape[1], vt_lat.shape[-1]
        assert tk % PV_ROWS == 0 and tk % SCORE_ROWS == 0 and n_chunks * tk == k_lat.shape[2]
        k_lat = k_lat.reshape(bsz, key_tiles, n_chunks, tk, K_TILE)
        in_specs += [whole(k_lat), whole(vt_lat)]
        args += [k_lat, vt_lat]
        stage_rows = tk
    if sink is not None:
        in_specs.append(pl.BlockSpec(memory_space=pltpu.SMEM))
        args.append(sink)
    if diff_lambda_init is not None:
        in_specs += [pl.BlockSpec(lamp.shape, lambda b, i: (0, 0)), pl.BlockSpec(subw.shape, lambda b, i: (0, 0))]
        args += [lamp, subw]
    scratch = _attn_scratch(n_maps, tq, n_ctx, stage_rows)
    return pl.pallas_call(
        functools.partial(_dense_attn_kernel, n_maps=n_maps, dq=dq, grouped=grouped, has_latent=has_latent,
                          has_sink=sink is not None, diff_lambda_init=diff_lambda_init),
        grid=(bsz, t // tq),
        in_specs=in_specs,
        out_specs=pl.BlockSpec((1, tq, 256), lambda b, i: (b, i, 0)),
        out_shape=jax.ShapeDtypeStruct((bsz, t, 256), BF16),
        scratch_shapes=scratch,
        compiler_params=_params("parallel", "parallel"),
        name="dense_attention",
    )(*args)


def _window_attn_kernel(q_ref, kp_ref, kc_ref, kn_ref, vtp_ref, vtc_ref, vtn_ref, kctx_ref, vtctx_ref, zero_ref,
                        sink_ref, o_ref, qpad_ref, acc_ref, m_ref, cs_ref, cbm_ref, *stage_refs, n_maps, seq):
    s_refs, bm_refs = stage_refs[:STAGES], stage_refs[STAGES:]
    tq = q_ref.shape[1]
    q0 = pl.program_id(1) * tq
    _fill_qpad(q_ref, qpad_ref, n_maps, HEAD_DIM, True)
    _init_state(m_ref, acc_ref, n_maps, sink_ref)
    k_win = jnp.concatenate([kp_ref[0], kc_ref[0], kn_ref[0]], axis=0)
    vt_win = jnp.concatenate([vtp_ref[0, 0], vtc_ref[0, 0], vtn_ref[0, 0]], axis=2)
    n_ctx, n_win = kctx_ref.shape[2], k_win.shape[0]
    row0 = zero_ref[0]

    def in_window(rows):
        kpos = q0 - WINDOW + rows.start + lax.broadcasted_iota(jnp.int32, (rows.stop - rows.start, tq), 0)
        qpos = q0 + lax.broadcasted_iota(jnp.int32, (rows.stop - rows.start, tq), 1)
        return (jnp.abs(kpos - qpos) <= WINDOW) & (kpos >= 0) & (kpos < seq)

    for j in range(n_maps):
        _score_unit(lambda rows: kctx_ref[0, 0, rows, :], j, n_ctx, qpad_ref, cs_ref.at[j], cbm_ref.at[j])
    for j in range(n_maps):
        _score_unit(lambda rows: k_win[rows, :], j, n_win, qpad_ref, s_refs[j], bm_refs[j], valid=in_window)
    for j in range(n_maps):
        _softmax_unit(lambda cols: vtctx_ref[0, 0, j // 2, :, cols], j, n_ctx, cs_ref.at[j], cbm_ref.at[j],
                      m_ref, acc_ref, row0)
    for j in range(n_maps):
        _softmax_unit(lambda cols: vt_win[j // 2][:, cols], j, n_win, s_refs[j], bm_refs[j], m_ref, acc_ref, row0)
    _finish_plain(o_ref, acc_ref, n_maps)


def _window_attention(q, k_ctx, vt_ctx, k_lat, vt_lat, sink):
    bsz, t, qw = q.shape
    key_tiles, _, kw = k_lat.shape[1:]
    assert key_tiles == 1
    tq = Q_TILE
    n_maps = 4
    k_flat = k_lat.reshape(bsz, t, kw)
    wb = tq // WINDOW
    nwb = t // WINDOW
    heads, cw = vt_lat.shape[2], vt_lat.shape[-1]
    whole = _whole_per_batch

    def vt_blk(width, blk):
        per_chunk = cw // width
        return pl.BlockSpec((1, 1, heads, V_ROWS, width),
                            lambda b, i: (b, blk(i) // per_chunk, 0, 0, blk(i) % per_chunk))

    def prev_blk(i):
        return jnp.maximum(i * wb - 1, 0)

    def next_blk(i):
        return jnp.minimum(i * wb + wb, nwb - 1)

    in_specs = [
        pl.BlockSpec((1, tq, qw), lambda b, i: (b, i, 0)),
        pl.BlockSpec((1, WINDOW, kw), lambda b, i: (b, prev_blk(i), 0)),
        pl.BlockSpec((1, tq, kw), lambda b, i: (b, i, 0)),
        pl.BlockSpec((1, WINDOW, kw), lambda b, i: (b, next_blk(i), 0)),
        vt_blk(WINDOW, prev_blk),
        vt_blk(tq, lambda i: i),
        vt_blk(WINDOW, next_blk),
        whole(k_ctx), whole(vt_ctx),
        pl.BlockSpec(memory_space=pltpu.SMEM),
        pl.BlockSpec(memory_space=pltpu.SMEM),
    ]
    assert n_maps == STAGES
    return pl.pallas_call(
        functools.partial(_window_attn_kernel, n_maps=n_maps, seq=t),
        grid=(bsz, t // tq),
        in_specs=in_specs,
        out_specs=pl.BlockSpec((1, tq, 256), lambda b, i: (b, i, 0)),
        out_shape=jax.ShapeDtypeStruct((bsz, t, 256), BF16),
        scratch_shapes=_attn_scratch(n_maps, tq, k_ctx.shape[2], tq + 2 * WINDOW),
        compiler_params=_params("parallel", "parallel"),
        name="window_attention",
    )(q, k_flat, k_flat, k_flat, vt_lat, vt_lat, vt_lat, k_ctx, vt_ctx, jnp.zeros((1,), jnp.int32), sink)


def _outproj_ffn_kernel(x_ref, mod_ref, ya_ref, g3_ref, g3p_ref, g3n_ref, yw_ref, yg_ref, cw_ref, w_ref, g_ref, b_ref,
                        wgu_ref, wdn_ref, g2_ref, b2_ref, o_ref, acc_ref, *, alpha):
    i = pl.program_id(1)
    last = pl.num_programs(1) - 1
    tm = x_ref.shape[1]
    gb = g3_ref[0, :, 0:256]
    hid = g3_ref[0, :, 256:512] * g3_ref[0, :, 512:768]
    halo = g3p_ref.shape[1]
    h_prev = g3p_ref[0, halo - 1:halo, 256:512] * g3p_ref[0, halo - 1:halo, 512:768]
    h_next = g3n_ref[0, 0:1, 256:512] * g3n_ref[0, 0:1, 512:768]
    h_prev = jnp.where(i == 0, 0.0, h_prev)
    h_next = jnp.where(i == last, 0.0, h_next)
    row = lax.broadcasted_iota(jnp.int32, hid.shape, 0)
    below = jnp.where(row == 0, h_prev, pltpu.roll(hid, 1, axis=0))
    above = jnp.where(row == tm - 1, h_next, pltpu.roll(hid, tm - 1, axis=0))
    yb = gb * (cw_ref[0:1, :] * below + cw_ref[1:2, :] * hid + cw_ref[2:3, :] * above)
    y = jnp.dot(ya_ref[0], w_ref[0:256, :], preferred_element_type=F32)
    y += jnp.dot(yb.astype(BF16), w_ref[256:512, :], preferred_element_type=F32)
    y += jnp.dot(yw_ref[0], w_ref[512:768, :], preferred_element_type=F32)
    y += jnp.dot(yg_ref[0], w_ref[768:1024, :], preferred_element_type=F32)
    gate = mod_ref[0, 5:6, :]
    x_mix = _layer_norm(alpha * x_ref[0] + gate * y, g_ref[...], b_ref[...])
    o_ref[0] = _ffn_body(x_mix, mod_ref, wgu_ref, wdn_ref, g2_ref, b2_ref, acc_ref, 6, alpha)


def _out_projection_ffn(x, mods, mod_row, ya, g3, yw, yg, conv_w, w_out, w_gu, w_dn, layer, ln_g, ln_b, ln_g2, ln_b2,
                        alpha):
    bsz, t, d = x.shape
    d_ff = w_dn.shape[1]
    tm = min(TOK_TILE, t)
    halo = 8
    hb = tm // halo
    n_halo = t // halo
    const = lambda b, i: (0, 0)
    y_spec = pl.BlockSpec((1, tm, 256), lambda b, i: (b, i, 0))
    return pl.pallas_call(
        functools.partial(_outproj_ffn_kernel, alpha=alpha),
        grid=(bsz, t // tm),
        in_specs=[
            pl.BlockSpec((1, tm, d), lambda b, i: (b, i, 0)),
            pl.BlockSpec((None, 1, N_MOD, d), lambda b, i: (layer, mod_row(b), 0, 0)),
            y_spec,
            pl.BlockSpec((1, tm, 768), lambda b, i: (b, i, 0)),
            pl.BlockSpec((1, halo, 768), lambda b, i: (b, jnp.maximum(i * hb - 1, 0), 0)),
            pl.BlockSpec((1, halo, 768), lambda b, i: (b, jnp.minimum(i * hb + hb, n_halo - 1), 0)),
            y_spec, y_spec,
            pl.BlockSpec(conv_w.shape, const),
            pl.BlockSpec((None,) + w_out.shape[1:], lambda b, i: (layer, 0, 0), pipeline_mode=pl.Buffered(1)),
            pl.BlockSpec((1, d), const),
            pl.BlockSpec((1, d), const),
            pl.BlockSpec((None, d, 2 * d_ff), lambda b, i: (layer, 0, 0), pipeline_mode=pl.Buffered(1)),
            pl.BlockSpec((None, d_ff, d), lambda b, i: (layer, 0, 0), pipeline_mode=pl.Buffered(1)),
            pl.BlockSpec((1, d), const),
            pl.BlockSpec((1, d), const),
        ],
        out_specs=pl.BlockSpec((1, tm, d), lambda b, i: (b, i, 0)),
        out_shape=jax.ShapeDtypeStruct(x.shape, F32),
        scratch_shapes=[pltpu.VMEM((tm, d), F32)],
        compiler_params=_params("parallel", "parallel"),
        name="out_projection_ffn",
    )(x, mods, ya, g3, g3, g3, yw, yg, conv_w, w_out, ln_g.reshape(1, d), ln_b.reshape(1, d), w_gu, w_dn,
      ln_g2.reshape(1, d), ln_b2.reshape(1, d))


def _rope_tables(rows, dim):
    n_freq = dim // 4
    inv_freq = ROPE_BASE ** (-jnp.arange(n_freq, dtype=F32) / n_freq)
    ang_r = jnp.arange(rows, dtype=F32)[:, None] * inv_freq
    ang_c = jnp.arange(GRID_W, dtype=F32)[:, None] * inv_freq

    def on_grid(of_row, of_col):
        r = jnp.broadcast_to(of_row[:, None, :], (rows, GRID_W, n_freq))
        c = jnp.broadcast_to(of_col[None, :, :], (rows, GRID_W, n_freq))
        return jnp.concatenate([r, r, c, c], axis=-1).reshape(rows * GRID_W, dim)

    cos = on_grid(jnp.cos(ang_r), jnp.cos(ang_c))
    sin = on_grid(jnp.sin(ang_r), jnp.sin(ang_c))
    reps = 128 // dim
    cos = jnp.tile(cos, (1, reps))
    sin = jnp.tile(sin, (1, reps))
    first_half = (jnp.arange(128) % (dim // 2)) < (dim // 4)
    sin_next = jnp.where(first_half, -sin, 0.0)
    sin_prev = jnp.where(first_half, 0.0, sin)
    return cos, sin_next, sin_prev


def _identity_tables(t):
    return jnp.ones((t, 128), F32), jnp.zeros((t, 128), F32), jnp.zeros((t, 128), F32)


def kernel(x, c, ctx, c_ctx, w_mod, b_mod, ln_g, ln_b, w_gu1, w_dn1, w_in, w_out, conv_w, lam_q1, lam_k1, lam_q2,
           lam_k2, subln_w, sink, qn_w, kn_w, w_gu2, w_dn2):
    bsz, seq, d = x.shape
    ctx_len = ctx.shape[1]
    depth = w_mod.shape[0]
    alpha = (2.0 * depth) ** 0.25
    assert seq % TOK_TILE == 0 and ctx_len % Q_TILE == 0 and bsz < MOD_ROWS

    tabs_a = _rope_tables(seq // GRID_W, A_QK_DIM)
    tabs_h = _rope_tables(seq // GRID_W, HEAD_DIM)
    tabs_id = _identity_tables(ctx_len)
    gsum = (jnp.arange(256)[:, None] // HEAD_DIM == jnp.arange(256)[None, :] // HEAD_DIM).astype(BF16)

    cvec = jnp.zeros((MOD_ROWS, d), F32).at[:bsz].set(c).at[bsz].set(c_ctx)
    mods_all = _mod_vectors(cvec, w_mod, b_mod).reshape(depth, MOD_ROWS, N_MOD, d)
    lat_row = lambda b: b
    ctx_row = lambda b: bsz

    wgu1, wdn1, wgu2, wdn2 = (w.astype(BF16) for w in (w_gu1, w_dn1, w_gu2, w_dn2))
    win, wout = w_in.astype(BF16), w_out.astype(BF16)

    xc = ctx
    for l in range(depth):
        need_ctx = l < depth - 1
        lambda_init = 0.8 - 0.6 * math.exp(-0.3 * l)
        mods = mods_all
        qnw = jnp.tile(qn_w[l], 256 // HEAD_DIM).reshape(1, 256)
        knw = jnp.tile(kn_w[l], 128 // HEAD_DIM).reshape(1, 128)
        lamp = jnp.stack([lam_q1[l], lam_k1[l], lam_q2[l], lam_k2[l]])
        subw = subln_w[l].reshape(HEAD_DIM, 1)

        x, qa, ka, vat, g3, qc, kc, vct, qd, kd, vdt = _ffn_in_projection(
            x, mods, lat_row, wgu1, wdn1, ln_g[l, 0], ln_b[l, 0], alpha, win, l, tabs_a, tabs_h, qnw, knw, gsum)
        xc, qa_c, ka_c, vat_c, g3_c, qc_c, kc_c, vct_c, qd_c, kd_c, vdt_c = _ffn_in_projection(
            xc, mods, ctx_row, wgu1, wdn1, ln_g[l, 0], ln_b[l, 0], alpha, win, l, tabs_id, tabs_id, qnw, knw, gsum)

        ya = _dense_attention(qa, ka_c, vat_c, ka, vat, n_maps=8, dq=A_QK_DIM, lamp=lamp, subw=subw,
                              diff_lambda_init=lambda_init)
        yw = _window_attention(qc, kc_c, vct_c, kc, vct, sink[l])
        yg = _dense_attention(qd, kd_c, vdt_c, kd, vdt, n_maps=4, dq=HEAD_DIM)
        x = _out_projection_ffn(x, mods, lat_row, ya, g3, yw, yg, conv_w[l], wout, wgu2, wdn2, l,
                                ln_g[l, 1], ln_b[l, 1], ln_g[l, 2], ln_b[l, 2], alpha)

        if need_ctx:
            ya_c = _dense_attention(qa_c, ka_c, vat_c, None, None, n_maps=8, dq=A_QK_DIM, lamp=lamp, subw=subw,
                                    diff_lambda_init=lambda_init)
            yw_c = _dense_attention(qc_c, kc_c, vct_c, None, None, n_maps=4, dq=HEAD_DIM, sink=sink[l])
            yg_c = _dense_attention(qd_c, kd_c, vdt_c, None, None, n_maps=4, dq=HEAD_DIM)
            xc = _out_projection_ffn(xc, mods, ctx_row, ya_c, g3_c, yw_c, yg_c, conv_w[l], wout, wgu2, wdn2, l,
                                     ln_g[l, 1], ln_b[l, 1], ln_g[l, 2], ln_b[l, 2], alpha)
    return x
```

```python
import functools
import math

import jax
import jax.numpy as jnp
from jax import lax
from jax.experimental import pallas as pl
from jax.experimental.pallas import tpu as pltpu

F32 = jnp.float32
BF16 = jnp.bfloat16

GRID_W = 64
HEAD_DIM = 64
WINDOW = 128
ROPE_BASE = 10000.0
EPS = 1e-6
NEG = -1e30
A_QK_DIM = 32
N_MOD = 9
MOD_ROWS = 8
VMEM_LIMIT = 56 * 1024 * 1024

OFF_AQ, OFF_AK, OFF_AV = 0, 256, 512
OFF_B = 768
OFF_CQ, OFF_CK, OFF_CV = 1536, 1792, 1920
OFF_DQ, OFF_DK, OFF_DV = 2048, 2304, 2432
IN_W = 2560

TOK_TILE = 512
Q_TILE = 256
DENSE_Q_TILE = 256
FF_CHUNK = 256
SCORE_ROWS = 128
PV_ROWS = 256
ATTN_CHUNK = 1024
STAGES = 4
LOOKAHEAD = 2
ROUNDS_PER_ITER = 8
K_TILE = 128
ONES_ROWS = 16
V_ROWS = HEAD_DIM + ONES_ROWS
LOG2E = math.log2(math.e)


def _params(*sem):
    return pltpu.CompilerParams(dimension_semantics=sem, vmem_limit_bytes=VMEM_LIMIT)


def _layer_norm(r, g, b):
    mu = jnp.mean(r, axis=-1, keepdims=True)
    d = r - mu
    var = jnp.mean(d * d, axis=-1, keepdims=True)
    return d * lax.rsqrt(var + EPS) * g + b


def _mod_kernel(c_ref, w_ref, b_ref, o_ref):
    c = c_ref[...]
    a = (c * jax.nn.sigmoid(c)).astype(BF16)
    o_ref[0] = jnp.dot(a, w_ref[0].astype(BF16), preferred_element_type=F32) + b_ref[0]


def _mod_vectors(cvec, w_mod, b_mod):
    depth, d, n = w_mod.shape
    tn = n // 8
    return pl.pallas_call(
        _mod_kernel,
        grid=(depth, n // tn),
        in_specs=[
            pl.BlockSpec((MOD_ROWS, d), lambda l, j: (0, 0)),
            pl.BlockSpec((1, d, tn), lambda l, j: (l, 0, j)),
            pl.BlockSpec((1, 1, tn), lambda l, j: (l, 0, j)),
        ],
        out_specs=pl.BlockSpec((1, MOD_ROWS, tn), lambda l, j: (l, 0, j)),
        out_shape=jax.ShapeDtypeStruct((depth, MOD_ROWS, n), F32),
        compiler_params=_params("parallel", "parallel"),
        name="mod_vectors",
    )(cvec, w_mod, b_mod.reshape(depth, 1, n))


def _ffn_body(x, mod_ref, wgu_ref, wdn_ref, g_ref, b_ref, acc_ref, mod_base, alpha):
    shift = mod_ref[0, mod_base:mod_base + 1, :]
    scale = mod_ref[0, mod_base + 1:mod_base + 2, :]
    gate = mod_ref[0, mod_base + 2:mod_base + 3, :]
    h = (x * (1.0 + scale) + shift).astype(BF16)
    d_ff = wdn_ref.shape[0]
    for c in range(d_ff // FF_CHUNK):
        lo = c * FF_CHUNK
        g = jnp.dot(h, wgu_ref[:, lo:lo + FF_CHUNK], preferred_element_type=F32)
        u = jnp.dot(h, wgu_ref[:, d_ff + lo:d_ff + lo + FF_CHUNK], preferred_element_type=F32)
        a = (g * jax.nn.sigmoid(g) * u).astype(BF16)
        y = jnp.dot(a, wdn_ref[lo:lo + FF_CHUNK, :], preferred_element_type=F32)
        if c == 0:
            acc_ref[...] = y
        else:
            acc_ref[...] += y
    r = alpha * x + (0.5 * gate) * acc_ref[...]
    return _layer_norm(r, g_ref[...], b_ref[...])


def _rope(x, cos, sin_next, sin_prev, quarter):
    width = x.shape[1]
    reps = width // cos.shape[1]
    if reps > 1:
        cos = jnp.concatenate([cos] * reps, axis=1)
        sin_next = jnp.concatenate([sin_next] * reps, axis=1)
        sin_prev = jnp.concatenate([sin_prev] * reps, axis=1)
    x_next = pltpu.roll(x, width - quarter, axis=1)
    x_prev = pltpu.roll(x, quarter, axis=1)
    return x * cos + x_next * sin_next + x_prev * sin_prev


def _head_rms(x, gsum_ref, w):
    x2 = x * x
    hi = x2.astype(BF16)
    lo = (x2 - hi.astype(F32)).astype(BF16)
    gs = gsum_ref[:x.shape[1], :x.shape[1]]
    ss = jnp.dot(hi, gs, preferred_element_type=F32) + jnp.dot(lo, gs, preferred_element_type=F32)
    return x * lax.rsqrt(ss * (1.0 / HEAD_DIM) + EPS) * w


def _store_k(k_ref, k, tile0=0):
    for kt in range(k.shape[1] // K_TILE):
        k_ref[0, tile0 + kt] = k[:, kt * K_TILE:(kt + 1) * K_TILE].astype(BF16)


def _store_vt(vt_ref, v, head0=0):
    vt = v.T
    tm = v.shape[0]
    for hd in range(v.shape[1] // HEAD_DIM):
        vt_ref[0, 0, head0 + hd, 0:HEAD_DIM, :] = vt[hd * HEAD_DIM:(hd + 1) * HEAD_DIM, :].astype(BF16)
        vt_ref[0, 0, head0 + hd, HEAD_DIM:V_ROWS, :] = jnp.ones((ONES_ROWS, tm), BF16)


def _ffn_inproj_kernel(x_ref, mod_ref, wgu_ref, wdn_ref, g_ref, b_ref, w_ref,
                       ca_ref, san_ref, sap_ref, ch_ref, shn_ref, shp_ref, qnw_ref, knw_ref, gsum_ref,
                       x1_ref, qad_ref, kad_ref, vadt_ref, g3_ref, qc_ref, kc_ref, vct_ref,
                       acc_ref, *, alpha):
    x = _ffn_body(x_ref[0], mod_ref, wgu_ref, wdn_ref, g_ref, b_ref, acc_ref, 0, alpha)
    x1_ref[0] = x
    shift = mod_ref[0, 3:4, :]
    scale = mod_ref[0, 4:5, :]
    h = (x * (1.0 + scale) + shift).astype(BF16)

    def seg(lo, hi):
        return jnp.dot(h, w_ref[:, lo:hi], preferred_element_type=F32)

    rope_a = functools.partial(_rope, cos=ca_ref[...], sin_next=san_ref[...], sin_prev=sap_ref[...],
                               quarter=A_QK_DIM // 4)
    rope_h = functools.partial(_rope, cos=ch_ref[...], sin_next=shn_ref[...], sin_prev=shp_ref[...],
                               quarter=HEAD_DIM // 4)
    qad_ref[0, :, 0:256] = (rope_a(seg(OFF_AQ, OFF_AK)) * (LOG2E * A_QK_DIM ** -0.5)).astype(BF16)
    _store_k(kad_ref, rope_a(seg(OFF_AK, OFF_AV)))
    _store_vt(vadt_ref, seg(OFF_AV, OFF_B))
    qc_ref[0] = (rope_h(seg(OFF_CQ, OFF_CK)) * (LOG2E * HEAD_DIM ** -0.5)).astype(BF16)
    kv = seg(OFF_CK, OFF_DQ)
    _store_k(kc_ref, rope_h(kv[:, :OFF_CV - OFF_CK]))
    _store_vt(vct_ref, kv[:, OFF_CV - OFF_CK:])
    qd = _head_rms(seg(OFF_DQ, OFF_DK), gsum_ref, qnw_ref[...])
    qad_ref[0, :, 256:512] = (rope_h(qd) * (LOG2E * HEAD_DIM ** -0.5)).astype(BF16)
    kv = seg(OFF_DK, IN_W)
    kd = _head_rms(kv[:, :OFF_DV - OFF_DK], gsum_ref, knw_ref[...])
    _store_k(kad_ref, rope_h(kd), tile0=2)
    _store_vt(vadt_ref, kv[:, OFF_DV - OFF_DK:], head0=4)
    g3_ref[0] = seg(OFF_B, OFF_CQ)


def _ffn_in_projection(x, mods, mod_row, w_gu, w_dn, ln_g, ln_b, alpha, w_in, layer, tabs_a, tabs_h, qnw, knw, gsum):
    bsz, t, d = x.shape
    d_ff = w_dn.shape[1]
    tm = min(TOK_TILE, t)
    nc = t // tm
    const = lambda b, i: (0, 0)
    tab = pl.BlockSpec((tm, 128), lambda b, i: (i, 0))
    q_spec = pl.BlockSpec((1, tm, 256), lambda b, i: (b, i, 0))

    def k_spec(w):
        return pl.BlockSpec((1, w // K_TILE, tm, K_TILE), lambda b, i: (b, 0, i, 0))

    cw = min(ATTN_CHUNK, t)
    per_chunk = cw // tm

    def vt_spec(w):
        return pl.BlockSpec((1, 1, w // HEAD_DIM, V_ROWS, tm),
                            lambda b, i: (b, i // per_chunk, 0, 0, i % per_chunk))

    def q_shape():
        return jax.ShapeDtypeStruct((bsz, t, 256), BF16)

    def k_shape(w):
        return jax.ShapeDtypeStruct((bsz, w // K_TILE, t, K_TILE), BF16)

    def vt_shape(w):
        return jax.ShapeDtypeStruct((bsz, t // cw, w // HEAD_DIM, V_ROWS, cw), BF16)

    x_spec = pl.BlockSpec((1, tm, d), lambda b, i: (b, i, 0))
    return pl.pallas_call(
        functools.partial(_ffn_inproj_kernel, alpha=alpha),
        grid=(bsz, nc),
        in_specs=[
            x_spec,
            pl.BlockSpec((None, 1, N_MOD, d), lambda b, i: (layer, mod_row(b), 0, 0)),
            pl.BlockSpec((None, d, 2 * d_ff), lambda b, i: (layer, 0, 0), pipeline_mode=pl.Buffered(1)),
            pl.BlockSpec((None, d_ff, d), lambda b, i: (layer, 0, 0), pipeline_mode=pl.Buffered(1)),
            pl.BlockSpec((1, d), const),
            pl.BlockSpec((1, d), const),
            pl.BlockSpec((None, d, IN_W), lambda b, i: (layer, 0, 0), pipeline_mode=pl.Buffered(1)),
            tab, tab, tab, tab, tab, tab,
            pl.BlockSpec((1, 256), const),
            pl.BlockSpec((1, 128), const),
            pl.BlockSpec((256, 256), const),
        ],
        out_specs=[x_spec, pl.BlockSpec((1, tm, 512), lambda b, i: (b, i, 0)), k_spec(384), vt_spec(384),
                   pl.BlockSpec((1, tm, 768), lambda b, i: (b, i, 0)),
                   q_spec, k_spec(128), vt_spec(128)],
        out_shape=[jax.ShapeDtypeStruct(x.shape, F32), jax.ShapeDtypeStruct((bsz, t, 512), BF16),
                   k_shape(384), vt_shape(384),
                   jax.ShapeDtypeStruct((bsz, t, 768), F32),
                   q_shape(), k_shape(128), vt_shape(128)],
        scratch_shapes=[pltpu.VMEM((tm, d), F32)],
        compiler_params=_params("parallel", "parallel"),
        name="ffn_in_projection",
    )(x, mods, w_gu, w_dn, ln_g.reshape(1, d), ln_b.reshape(1, d), w_in, *tabs_a, *tabs_h, qnw, knw, gsum)


def _whole_per_batch(arr):
    zeros = (0,) * (arr.ndim - 1)
    return pl.BlockSpec((1,) + arr.shape[1:], lambda b, i: (b,) + zeros, pipeline_mode=pl.Buffered(1))


def _attn_scratch(n_maps, tq, ctx_rows, stage_rows):
    shapes = ([((n_maps, K_TILE, tq), BF16), ((n_maps, V_ROWS, tq), F32), ((n_maps, 8, tq), F32),
               ((n_maps, ctx_rows, tq), F32), ((n_maps, 8, tq), F32)]
              + [((stage_rows, tq), F32)] * STAGES + [((8, tq), F32)] * STAGES)
    return [pltpu.VMEM(shape, dtype) for shape, dtype in shapes]


MAPS_PER_KEY_TILE = 4
MAPS_PER_V_HEAD = 2


def _map_table(segments):
    table, q_col, key_col = [], 0, 0
    for kind, n_maps, dq in segments:
        grouped = kind == "plain"
        for j in range(n_maps):
            table.append((q_col + j * dq, dq, key_col + ((j // 2) * dq if grouped else j * dq)))
        q_col += n_maps * dq
        key_col += (n_maps // 2) * dq if grouped else n_maps * dq
        assert key_col % K_TILE == 0 and n_maps % MAPS_PER_KEY_TILE == 0
    assert all(kc // K_TILE == j // MAPS_PER_KEY_TILE for j, (_, _, kc) in enumerate(table))
    return tuple(table)


def _fill_qpad(q_ref, qpad_ref, table):
    tq = q_ref.shape[1]
    qt = q_ref[0].astype(F32).T
    for j, (q_col, dq, key_col) in enumerate(table):
        row = key_col % K_TILE
        parts = []
        if row:
            parts.append(jnp.zeros((row, tq), F32))
        parts.append(qt[q_col:q_col + dq, :])
        if K_TILE - row - dq:
            parts.append(jnp.zeros((K_TILE - row - dq, tq), F32))
        qpad_ref[j] = jnp.concatenate(parts, axis=0).astype(BF16)


def _score_unit(keys, j, nk, qpad_ref, s_ref, bm_ref, valid=None):
    tq = s_ref.shape[1]
    w = qpad_ref[j]
    bm = None
    for r in range(nk // SCORE_ROWS):
        rows = slice(r * SCORE_ROWS, (r + 1) * SCORE_ROWS)
        s = jnp.dot(keys(rows), w, preferred_element_type=F32)
        if valid is not None:
            s = jnp.where(valid(rows), s, NEG)
        s_ref[rows, :] = s
        part = jnp.max(s.reshape(SCORE_ROWS // 8, 8, tq), axis=0)
        bm = part if bm is None else jnp.maximum(bm, part)
    bm_ref[...] = bm


def _softmax_unit(values, j, nk, s_ref, bm_ref, m_ref, acc_ref, row0):
    m_old = m_ref[j, 0:1, :]
    m_new = jnp.maximum(m_old, jnp.max(bm_ref[...], axis=0, keepdims=True))
    alpha = jnp.exp2(m_old - m_new)
    pv = None
    for r in range(nk // PV_ROWS):
        rows = pl.ds(pl.multiple_of(row0 + r * PV_ROWS, PV_ROWS), PV_ROWS)
        p = jnp.exp2(s_ref[rows, :] - m_new)
        d = jnp.dot(values(slice(r * PV_ROWS, (r + 1) * PV_ROWS)), p.astype(BF16), preferred_element_type=F32)
        pv = d if pv is None else pv + d
    acc_ref[j] = alpha * acc_ref[j] + pv
    m_ref[j] = jnp.broadcast_to(m_new, m_ref.shape[1:])


def _init_state(m_ref, acc_ref, n_maps, sink_ref):
    tq = m_ref.shape[-1]
    for j in range(n_maps):
        if sink_ref is None:
            m_ref[j] = jnp.full(m_ref.shape[1:], NEG, F32)
            acc_ref[j] = jnp.zeros((V_ROWS, tq), F32)
        else:
            m_ref[j] = jnp.full(m_ref.shape[1:], sink_ref[j] * LOG2E, F32)
            acc_ref[j] = jnp.concatenate([jnp.zeros((HEAD_DIM, tq), F32), jnp.ones((ONES_ROWS, tq), F32)], axis=0)


def _normalised(acc_ref, j):
    acc = acc_ref[j]
    return acc[0:HEAD_DIM, :] / acc[HEAD_DIM:HEAD_DIM + 1, :]


def _finish_plain(o_ref, acc_ref, first, n_maps):
    ot = jnp.concatenate([_normalised(acc_ref, first + j) for j in range(n_maps)], axis=0)
    o_ref[0] = ot.T.astype(BF16)


def _finish_diff(o_ref, acc_ref, lamp_ref, subw_ref, first, n_maps, lambda_init):
    lp = lamp_ref[...]
    lam = (jnp.exp(jnp.sum(lp[0:1] * lp[1:2], axis=1, keepdims=True))
           - jnp.exp(jnp.sum(lp[2:3] * lp[3:4], axis=1, keepdims=True)) + lambda_init)
    outs = []
    for hd in range(n_maps // 2):
        o = (_normalised(acc_ref, first + 2 * hd)
             - lam * _normalised(acc_ref, first + 2 * hd + 1))
        ms = jnp.mean(o * o, axis=0, keepdims=True)
        outs.append(o * lax.rsqrt(ms + EPS) * subw_ref[...] * (1.0 - lambda_init))
    o_ref[0] = jnp.concatenate(outs, axis=0).T.astype(BF16)


def _dense_attn_kernel(*refs, segments, has_latent, has_sink, diff_lambda_init):
    refs = list(refs)
    q_ref, kc_ref, vct_ref, zero_ref = refs[:4]
    pos = 4
    k_ref = vt_ref = sink_ref = lamp_ref = subw_ref = None
    if has_latent:
        k_ref, vt_ref = refs[pos:pos + 2]
        pos += 2
    if has_sink:
        sink_ref = refs[pos]
        pos += 1
    if diff_lambda_init is not None:
        lamp_ref, subw_ref = refs[pos:pos + 2]
        pos += 2
    o_refs = refs[pos:pos + len(segments)]
    pos += len(segments)
    qpad_ref, acc_ref, m_ref, cs_ref, cbm_ref = refs[pos:pos + 5]
    s_refs = refs[pos + 5:pos + 5 + STAGES]
    bm_refs = refs[pos + 5 + STAGES:pos + 5 + 2 * STAGES]

    table = _map_table(segments)
    n_maps = len(table)
    _fill_qpad(q_ref, qpad_ref, table)
    _init_state(m_ref, acc_ref, n_maps, sink_ref)
    n_ctx = kc_ref.shape[2]
    row0 = zero_ref[0]

    for j in range(n_maps):
        _score_unit(lambda rows: kc_ref[0, j // MAPS_PER_KEY_TILE, rows, :], j, n_ctx, qpad_ref, cs_ref.at[j],
                    cbm_ref.at[j])

    if has_latent:
        tk = k_ref.shape[3]
        n_units = n_maps * k_ref.shape[2]

        def score(u, slot):
            j, c = u % n_maps, u // n_maps
            _score_unit(lambda rows: k_ref[0, j // MAPS_PER_KEY_TILE, c, rows, :], j, tk, qpad_ref, s_refs[slot],
                        bm_refs[slot])

        def softmax(u, slot):
            j, c = u % n_maps, u // n_maps
            _softmax_unit(lambda cols: vt_ref[0, c, j // MAPS_PER_V_HEAD, :, cols], j, tk, s_refs[slot],
                          bm_refs[slot], m_ref, acc_ref, row0)

        for u in range(LOOKAHEAD):
            score(u, u % STAGES)

    for j in range(n_maps):
        _softmax_unit(lambda cols: vct_ref[0, 0, j // MAPS_PER_V_HEAD, :, cols], j, n_ctx, cs_ref.at[j],
                      cbm_ref.at[j], m_ref, acc_ref, row0)

    if has_latent:
        per_iter = STAGES * math.gcd(n_units // STAGES, ROUNDS_PER_ITER)

        def body(i, carry):
            u0 = per_iter * i
            for k in range(per_iter):
                score(jnp.minimum(u0 + k + LOOKAHEAD, n_units - 1), (k + LOOKAHEAD) % STAGES)
                softmax(u0 + k, k % STAGES)
            return carry
        lax.fori_loop(0, n_units // per_iter, body, 0)
    first = 0
    for (kind, seg_maps, _), o_ref in zip(segments, o_refs):
        if kind == "diff":
            _finish_diff(o_ref, acc_ref, lamp_ref, subw_ref, first, seg_maps, diff_lambda_init)
        else:
            _finish_plain(o_ref, acc_ref, first, seg_maps)
        first += seg_maps


def _dense_attention(q, k_ctx, vt_ctx, k_lat, vt_lat, *, segments, sink=None, lamp=None, subw=None,
                     diff_lambda_init=None):
    bsz, t, qw = q.shape
    key_tiles, n_ctx = k_ctx.shape[1:3]
    n_maps = len(_map_table(segments))
    assert vt_ctx.shape[1] == 1 and vt_ctx.shape[-1] == n_ctx
    assert key_tiles * MAPS_PER_KEY_TILE == n_maps and vt_ctx.shape[2] * MAPS_PER_V_HEAD == n_maps
    tq = min(DENSE_Q_TILE, t)
    has_latent = k_lat is not None
    whole = _whole_per_batch
    assert n_maps % STAGES == 0 and n_ctx % PV_ROWS == 0

    in_specs = [pl.BlockSpec((1, tq, qw), lambda b, i: (b, i, 0)), whole(k_ctx), whole(vt_ctx),
                pl.BlockSpec(memory_space=pltpu.SMEM)]
    args = [q, k_ctx, vt_ctx, jnp.zeros((1,), jnp.int32)]
    stage_rows = 8
    if has_latent:
        n_chunks, tk = vt_lat.shape[1], vt_lat.shape[-1]
        assert tk % PV_ROWS == 0 and tk % SCORE_ROWS == 0 and n_chunks * tk == k_lat.shape[2]
        k_lat = k_lat.reshape(bsz, key_tiles, n_chunks, tk, K_TILE)
        in_specs += [whole(k_lat), whole(vt_lat)]
        args += [k_lat, vt_lat]
        stage_rows = tk
    if sink is not None:
        in_specs.append(pl.BlockSpec(memory_space=pltpu.SMEM))
        args.append(sink)
    if diff_lambda_init is not None:
        in_specs += [pl.BlockSpec(lamp.shape, lambda b, i: (0, 0)), pl.BlockSpec(subw.shape, lambda b, i: (0, 0))]
        args += [lamp, subw]
    scratch = _attn_scratch(n_maps, tq, n_ctx, stage_rows)
    return pl.pallas_call(
        functools.partial(_dense_attn_kernel, segments=segments, has_latent=has_latent,
                          has_sink=sink is not None, diff_lambda_init=diff_lambda_init),
        grid=(bsz, t // tq),
        in_specs=in_specs,
        out_specs=[pl.BlockSpec((1, tq, 256), lambda b, i: (b, i, 0))] * len(segments),
        out_shape=[jax.ShapeDtypeStruct((bsz, t, 256), BF16)] * len(segments),
        scratch_shapes=scratch,
        compiler_params=_params("parallel", "parallel"),
        name="dense_attention",
    )(*args)


def _window_attn_kernel(q_ref, kp_ref, kc_ref, kn_ref, vtp_ref, vtc_ref, vtn_ref, kctx_ref, vtctx_ref, zero_ref,
                        sink_ref, o_ref, qpad_ref, acc_ref, m_ref, cs_ref, cbm_ref, *stage_refs, n_maps, seq):
    s_refs, bm_refs = stage_refs[:STAGES], stage_refs[STAGES:]
    tq = q_ref.shape[1]
    q0 = pl.program_id(1) * tq
    _fill_qpad(q_ref, qpad_ref, _map_table((("plain", n_maps, HEAD_DIM),)))
    _init_state(m_ref, acc_ref, n_maps, sink_ref)
    k_win = jnp.concatenate([kp_ref[0], kc_ref[0], kn_ref[0]], axis=0)
    vt_win = jnp.concatenate([vtp_ref[0, 0], vtc_ref[0, 0], vtn_ref[0, 0]], axis=2)
    n_ctx, n_win = kctx_ref.shape[2], k_win.shape[0]
    row0 = zero_ref[0]

    def in_window(rows):
        kpos = q0 - WINDOW + rows.start + lax.broadcasted_iota(jnp.int32, (rows.stop - rows.start, tq), 0)
        qpos = q0 + lax.broadcasted_iota(jnp.int32, (rows.stop - rows.start, tq), 1)
        return (jnp.abs(kpos - qpos) <= WINDOW) & (kpos >= 0) & (kpos < seq)

    for j in range(n_maps):
        _score_unit(lambda rows: kctx_ref[0, 0, rows, :], j, n_ctx, qpad_ref, cs_ref.at[j], cbm_ref.at[j])
    for j in range(n_maps):
        _score_unit(lambda rows: k_win[rows, :], j, n_win, qpad_ref, s_refs[j], bm_refs[j], valid=in_window)
    for j in range(n_maps):
        _softmax_unit(lambda cols: vtctx_ref[0, 0, j // 2, :, cols], j, n_ctx, cs_ref.at[j], cbm_ref.at[j],
                      m_ref, acc_ref, row0)
    for j in range(n_maps):
        _softmax_unit(lambda cols: vt_win[j // 2][:, cols], j, n_win, s_refs[j], bm_refs[j], m_ref, acc_ref, row0)
    _finish_plain(o_ref, acc_ref, 0, n_maps)


def _window_attention(q, k_ctx, vt_ctx, k_lat, vt_lat, sink):
    bsz, t, qw = q.shape
    key_tiles, _, kw = k_lat.shape[1:]
    assert key_tiles == 1
    tq = Q_TILE
    n_maps = 4
    k_flat = k_lat.reshape(bsz, t, kw)
    wb = tq // WINDOW
    nwb = t // WINDOW
    heads, cw = vt_lat.shape[2], vt_lat.shape[-1]
    whole = _whole_per_batch

    def vt_blk(width, blk):
        per_chunk = cw // width
        return pl.BlockSpec((1, 1, heads, V_ROWS, width),
                            lambda b, i: (b, blk(i) // per_chunk, 0, 0, blk(i) % per_chunk))

    def prev_blk(i):
        return jnp.maximum(i * wb - 1, 0)

    def next_blk(i):
        return jnp.minimum(i * wb + wb, nwb - 1)

    in_specs = [
        pl.BlockSpec((1, tq, qw), lambda b, i: (b, i, 0)),
        pl.BlockSpec((1, WINDOW, kw), lambda b, i: (b, prev_blk(i), 0)),
        pl.BlockSpec((1, tq, kw), lambda b, i: (b, i, 0)),
        pl.BlockSpec((1, WINDOW, kw), lambda b, i: (b, next_blk(i), 0)),
        vt_blk(WINDOW, prev_blk),
        vt_blk(tq, lambda i: i),
        vt_blk(WINDOW, next_blk),
        whole(k_ctx), whole(vt_ctx),
        pl.BlockSpec(memory_space=pltpu.SMEM),
        pl.BlockSpec(memory_space=pltpu.SMEM),
    ]
    assert n_maps == STAGES
    return pl.pallas_call(
        functools.partial(_window_attn_kernel, n_maps=n_maps, seq=t),
        grid=(bsz, t // tq),
        in_specs=in_specs,
        out_specs=pl.BlockSpec((1, tq, 256), lambda b, i: (b, i, 0)),
        out_shape=jax.ShapeDtypeStruct((bsz, t, 256), BF16),
        scratch_shapes=_attn_scratch(n_maps, tq, k_ctx.shape[2], tq + 2 * WINDOW),
        compiler_params=_params("parallel", "parallel"),
        name="window_attention",
    )(q, k_flat, k_flat, k_flat, vt_lat, vt_lat, vt_lat, k_ctx, vt_ctx, jnp.zeros((1,), jnp.int32), sink)


def _outproj_ffn_kernel(x_ref, mod_ref, ya_ref, g3_ref, g3p_ref, g3n_ref, yw_ref, yg_ref, cw_ref, w_ref, g_ref, b_ref,
                        wgu_ref, wdn_ref, g2_ref, b2_ref, o_ref, acc_ref, *, alpha):
    i = pl.program_id(1)
    last = pl.num_programs(1) - 1
    tm = x_ref.shape[1]
    gb = g3_ref[0, :, 0:256]
    hid = g3_ref[0, :, 256:512] * g3_ref[0, :, 512:768]
    halo = g3p_ref.shape[1]
    h_prev = g3p_ref[0, halo - 1:halo, 256:512] * g3p_ref[0, halo - 1:halo, 512:768]
    h_next = g3n_ref[0, 0:1, 256:512] * g3n_ref[0, 0:1, 512:768]
    h_prev = jnp.where(i == 0, 0.0, h_prev)
    h_next = jnp.where(i == last, 0.0, h_next)
    row = lax.broadcasted_iota(jnp.int32, hid.shape, 0)
    below = jnp.where(row == 0, h_prev, pltpu.roll(hid, 1, axis=0))
    above = jnp.where(row == tm - 1, h_next, pltpu.roll(hid, tm - 1, axis=0))
    yb = gb * (cw_ref[0:1, :] * below + cw_ref[1:2, :] * hid + cw_ref[2:3, :] * above)
    y = jnp.dot(ya_ref[0], w_ref[0:256, :], preferred_element_type=F32)
    y += jnp.dot(yb.astype(BF16), w_ref[256:512, :], preferred_element_type=F32)
    y += jnp.dot(yw_ref[0], w_ref[512:768, :], preferred_element_type=F32)
    y += jnp.dot(yg_ref[0], w_ref[768:1024, :], preferred_element_type=F32)
    gate = mod_ref[0, 5:6, :]
    x_mix = _layer_norm(alpha * x_ref[0] + gate * y, g_ref[...], b_ref[...])
    o_ref[0] = _ffn_body(x_mix, mod_ref, wgu_ref, wdn_ref, g2_ref, b2_ref, acc_ref, 6, alpha)


def _out_projection_ffn(x, mods, mod_row, ya, g3, yw, yg, conv_w, w_out, w_gu, w_dn, layer, ln_g, ln_b, ln_g2, ln_b2,
                        alpha):
    bsz, t, d = x.shape
    d_ff = w_dn.shape[1]
    tm = min(TOK_TILE, t)
    halo = 8
    hb = tm // halo
    n_halo = t // halo
    const = lambda b, i: (0, 0)
    y_spec = pl.BlockSpec((1, tm, 256), lambda b, i: (b, i, 0))
    return pl.pallas_call(
        functools.partial(_outproj_ffn_kernel, alpha=alpha),
        grid=(bsz, t // tm),
        in_specs=[
            pl.BlockSpec((1, tm, d), lambda b, i: (b, i, 0)),
            pl.BlockSpec((None, 1, N_MOD, d), lambda b, i: (layer, mod_row(b), 0, 0)),
            y_spec,
            pl.BlockSpec((1, tm, 768), lambda b, i: (b, i, 0)),
            pl.BlockSpec((1, halo, 768), lambda b, i: (b, jnp.maximum(i * hb - 1, 0), 0)),
            pl.BlockSpec((1, halo, 768), lambda b, i: (b, jnp.minimum(i * hb + hb, n_halo - 1), 0)),
            y_spec, y_spec,
            pl.BlockSpec(conv_w.shape, const),
            pl.BlockSpec((None,) + w_out.shape[1:], lambda b, i: (layer, 0, 0), pipeline_mode=pl.Buffered(1)),
            pl.BlockSpec((1, d), const),
            pl.BlockSpec((1, d), const),
            pl.BlockSpec((None, d, 2 * d_ff), lambda b, i: (layer, 0, 0), pipeline_mode=pl.Buffered(1)),
            pl.BlockSpec((None, d_ff, d), lambda b, i: (layer, 0, 0), pipeline_mode=pl.Buffered(1)),
            pl.BlockSpec((1, d), const),
            pl.BlockSpec((1, d), const),
        ],
        out_specs=pl.BlockSpec((1, tm, d), lambda b, i: (b, i, 0)),
        out_shape=jax.ShapeDtypeStruct(x.shape, F32),
        scratch_shapes=[pltpu.VMEM((tm, d), F32)],
        compiler_params=_params("parallel", "parallel"),
        name="out_projection_ffn",
    )(x, mods, ya, g3, g3, g3, yw, yg, conv_w, w_out, ln_g.reshape(1, d), ln_b.reshape(1, d), w_gu, w_dn,
      ln_g2.reshape(1, d), ln_b2.reshape(1, d))


def _rope_tables(rows, dim):
    n_freq = dim // 4
    inv_freq = ROPE_BASE ** (-jnp.arange(n_freq, dtype=F32) / n_freq)
    ang_r = jnp.arange(rows, dtype=F32)[:, None] * inv_freq
    ang_c = jnp.arange(GRID_W, dtype=F32)[:, None] * inv_freq

    def on_grid(of_row, of_col):
        r = jnp.broadcast_to(of_row[:, None, :], (rows, GRID_W, n_freq))
        c = jnp.broadcast_to(of_col[None, :, :], (rows, GRID_W, n_freq))
        return jnp.concatenate([r, r, c, c], axis=-1).reshape(rows * GRID_W, dim)

    cos = on_grid(jnp.cos(ang_r), jnp.cos(ang_c))
    sin = on_grid(jnp.sin(ang_r), jnp.sin(ang_c))
    reps = 128 // dim
    cos = jnp.tile(cos, (1, reps))
    sin = jnp.tile(sin, (1, reps))
    first_half = (jnp.arange(128) % (dim // 2)) < (dim // 4)
    sin_next = jnp.where(first_half, -sin, 0.0)
    sin_prev = jnp.where(first_half, 0.0, sin)
    return cos, sin_next, sin_prev


def _identity_tables(t):
    return jnp.ones((t, 128), F32), jnp.zeros((t, 128), F32), jnp.zeros((t, 128), F32)


def kernel(x, c, ctx, c_ctx, w_mod, b_mod, ln_g, ln_b, w_gu1, w_dn1, w_in, w_out, conv_w, lam_q1, lam_k1, lam_q2,
           lam_k2, subln_w, sink, qn_w, kn_w, w_gu2, w_dn2):
    bsz, seq, d = x.shape
    ctx_len = ctx.shape[1]
    depth = w_mod.shape[0]
    alpha = (2.0 * depth) ** 0.25
    assert seq % TOK_TILE == 0 and ctx_len % Q_TILE == 0 and bsz < MOD_ROWS

    tabs_a = _rope_tables(seq // GRID_W, A_QK_DIM)
    tabs_h = _rope_tables(seq // GRID_W, HEAD_DIM)
    tabs_id = _identity_tables(ctx_len)
    gsum = (jnp.arange(256)[:, None] // HEAD_DIM == jnp.arange(256)[None, :] // HEAD_DIM).astype(BF16)

    cvec = jnp.zeros((MOD_ROWS, d), F32).at[:bsz].set(c).at[bsz].set(c_ctx)
    mods_all = _mod_vectors(cvec, w_mod, b_mod).reshape(depth, MOD_ROWS, N_MOD, d)
    lat_row = lambda b: b
    ctx_row = lambda b: bsz

    wgu1, wdn1, wgu2, wdn2 = (w.astype(BF16) for w in (w_gu1, w_dn1, w_gu2, w_dn2))
    win, wout = w_in.astype(BF16), w_out.astype(BF16)

    xc = ctx
    for l in range(depth):
        need_ctx = l < depth - 1
        lambda_init = 0.8 - 0.6 * math.exp(-0.3 * l)
        mods = mods_all
        qnw = jnp.tile(qn_w[l], 256 // HEAD_DIM).reshape(1, 256)
        knw = jnp.tile(kn_w[l], 128 // HEAD_DIM).reshape(1, 128)
        lamp = jnp.stack([lam_q1[l], lam_k1[l], lam_q2[l], lam_k2[l]])
        subw = subln_w[l].reshape(HEAD_DIM, 1)

        x, qad, kad, vadt, g3, qc, kc, vct = _ffn_in_projection(
            x, mods, lat_row, wgu1, wdn1, ln_g[l, 0], ln_b[l, 0], alpha, win, l, tabs_a, tabs_h, qnw, knw, gsum)
        xc, qad_c, kad_c, vadt_c, g3_c, qc_c, kc_c, vct_c = _ffn_in_projection(
            xc, mods, ctx_row, wgu1, wdn1, ln_g[l, 0], ln_b[l, 0], alpha, win, l, tabs_id, tabs_id, qnw, knw, gsum)

        dense = (("diff", 8, A_QK_DIM), ("plain", 4, HEAD_DIM))
        ya, yg = _dense_attention(qad, kad_c, vadt_c, kad, vadt, segments=dense, lamp=lamp, subw=subw,
                                  diff_lambda_init=lambda_init)
        yw = _window_attention(qc, kc_c, vct_c, kc, vct, sink[l])
        x = _out_projection_ffn(x, mods, lat_row, ya, g3, yw, yg, conv_w[l], wout, wgu2, wdn2, l,
                                ln_g[l, 1], ln_b[l, 1], ln_g[l, 2], ln_b[l, 2], alpha)

        if need_ctx:
            ya_c, yg_c = _dense_attention(qad_c, kad_c, vadt_c, None, None, segments=dense, lamp=lamp, subw=subw,
                                          diff_lambda_init=lambda_init)
            yw_c, = _dense_attention(qc_c, kc_c, vct_c, None, None, segments=(("plain", 4, HEAD_DIM),),
                                     sink=sink[l])
            xc = _out_projection_ffn(xc, mods, ctx_row, ya_c, g3_c, yw_c, yg_c, conv_w[l], wout, wgu2, wdn2, l,
                                     ln_g[l, 1], ln_b[l, 1], ln_g[l, 2], ln_b[l, 2], alpha)
    return x
```

```python
import functools
import math

import jax
import jax.numpy as jnp
from jax import lax
from jax.experimental import pallas as pl
from jax.experimental.pallas import tpu as pltpu

F32 = jnp.float32
BF16 = jnp.bfloat16

GRID_W = 64
HEAD_DIM = 64
WINDOW = 128
ROPE_BASE = 10000.0
EPS = 1e-6
NEG = -1e30
A_QK_DIM = 32
N_MOD = 9
MOD_ROWS = 8
VMEM_LIMIT = 56 * 1024 * 1024

OFF_AQ, OFF_AK, OFF_AV = 0, 256, 512
OFF_B = 768
OFF_CQ, OFF_CK, OFF_CV = 1536, 1792, 1920
OFF_DQ, OFF_DK, OFF_DV = 2048, 2304, 2432
IN_W = 2560

TOK_TILE = 512
Q_TILE = 256
DENSE_Q_TILE = 256
FF_CHUNK = 256
SCORE_ROWS = 128
PV_ROWS = 256
ATTN_CHUNK = 1024
STAGES = 4
LOOKAHEAD = 2
ROUNDS_PER_ITER = 12
K_TILE = 128
ONES_ROWS = 16
V_ROWS = HEAD_DIM + ONES_ROWS
LOG2E = math.log2(math.e)


def _params(*sem):
    return pltpu.CompilerParams(dimension_semantics=sem, vmem_limit_bytes=VMEM_LIMIT)


def _layer_norm(r, g, b):
    mu = jnp.mean(r, axis=-1, keepdims=True)
    d = r - mu
    var = jnp.mean(d * d, axis=-1, keepdims=True)
    return d * lax.rsqrt(var + EPS) * g + b


def _mod_kernel(c_ref, w_ref, b_ref, o_ref):
    c = c_ref[...]
    a = (c * jax.nn.sigmoid(c)).astype(BF16)
    o_ref[0] = jnp.dot(a, w_ref[0].astype(BF16), preferred_element_type=F32) + b_ref[0]


def _mod_vectors(cvec, w_mod, b_mod):
    depth, d, n = w_mod.shape
    tn = n // 8
    return pl.pallas_call(
        _mod_kernel,
        grid=(depth, n // tn),
        in_specs=[
            pl.BlockSpec((MOD_ROWS, d), lambda l, j: (0, 0)),
            pl.BlockSpec((1, d, tn), lambda l, j: (l, 0, j)),
            pl.BlockSpec((1, 1, tn), lambda l, j: (l, 0, j)),
        ],
        out_specs=pl.BlockSpec((1, MOD_ROWS, tn), lambda l, j: (l, 0, j)),
        out_shape=jax.ShapeDtypeStruct((depth, MOD_ROWS, n), F32),
        compiler_params=_params("parallel", "parallel"),
        name="mod_vectors",
    )(cvec, w_mod, b_mod.reshape(depth, 1, n))


def _ffn_body(x, mod_ref, wgu_ref, wdn_ref, g_ref, b_ref, acc_ref, mod_base, alpha):
    shift = mod_ref[0, mod_base:mod_base + 1, :]
    scale = mod_ref[0, mod_base + 1:mod_base + 2, :]
    gate = mod_ref[0, mod_base + 2:mod_base + 3, :]
    h = (x * (1.0 + scale) + shift).astype(BF16)
    d_ff = wdn_ref.shape[0]
    for c in range(d_ff // FF_CHUNK):
        lo = c * FF_CHUNK
        g = jnp.dot(h, wgu_ref[:, lo:lo + FF_CHUNK], preferred_element_type=F32)
        u = jnp.dot(h, wgu_ref[:, d_ff + lo:d_ff + lo + FF_CHUNK], preferred_element_type=F32)
        a = (g * jax.nn.sigmoid(g) * u).astype(BF16)
        y = jnp.dot(a, wdn_ref[lo:lo + FF_CHUNK, :], preferred_element_type=F32)
        if c == 0:
            acc_ref[...] = y
        else:
            acc_ref[...] += y
    r = alpha * x + (0.5 * gate) * acc_ref[...]
    return _layer_norm(r, g_ref[...], b_ref[...])


def _rope(x, cos, sin_next, sin_prev, quarter):
    width = x.shape[1]
    reps = width // cos.shape[1]
    if reps > 1:
        cos = jnp.concatenate([cos] * reps, axis=1)
        sin_next = jnp.concatenate([sin_next] * reps, axis=1)
        sin_prev = jnp.concatenate([sin_prev] * reps, axis=1)
    x_next = pltpu.roll(x, width - quarter, axis=1)
    x_prev = pltpu.roll(x, quarter, axis=1)
    return x * cos + x_next * sin_next + x_prev * sin_prev


def _head_rms(x, gsum_ref, w):
    x2 = x * x
    hi = x2.astype(BF16)
    lo = (x2 - hi.astype(F32)).astype(BF16)
    gs = gsum_ref[:x.shape[1], :x.shape[1]]
    ss = jnp.dot(hi, gs, preferred_element_type=F32) + jnp.dot(lo, gs, preferred_element_type=F32)
    return x * lax.rsqrt(ss * (1.0 / HEAD_DIM) + EPS) * w


def _store_k(k_ref, k, tile0=0):
    for kt in range(k.shape[1] // K_TILE):
        k_ref[0, tile0 + kt] = k[:, kt * K_TILE:(kt + 1) * K_TILE].astype(BF16)


def _store_vt(vt_ref, v, head0=0):
    vt = v.T
    tm = v.shape[0]
    for hd in range(v.shape[1] // HEAD_DIM):
        vt_ref[0, 0, head0 + hd, 0:HEAD_DIM, :] = vt[hd * HEAD_DIM:(hd + 1) * HEAD_DIM, :].astype(BF16)
        vt_ref[0, 0, head0 + hd, HEAD_DIM:V_ROWS, :] = jnp.ones((ONES_ROWS, tm), BF16)


def _ffn_inproj_kernel(x_ref, mod_ref, wgu_ref, wdn_ref, g_ref, b_ref, w_ref,
                       ca_ref, san_ref, sap_ref, ch_ref, shn_ref, shp_ref, qnw_ref, knw_ref, gsum_ref,
                       x1_ref, qad_ref, kad_ref, vadt_ref, g3_ref, qc_ref, kc_ref, vct_ref,
                       acc_ref, *, alpha):
    x = _ffn_body(x_ref[0], mod_ref, wgu_ref, wdn_ref, g_ref, b_ref, acc_ref, 0, alpha)
    x1_ref[0] = x
    shift = mod_ref[0, 3:4, :]
    scale = mod_ref[0, 4:5, :]
    h = (x * (1.0 + scale) + shift).astype(BF16)

    def seg(lo, hi):
        return jnp.dot(h, w_ref[:, lo:hi], preferred_element_type=F32)

    rope_a = functools.partial(_rope, cos=ca_ref[...], sin_next=san_ref[...], sin_prev=sap_ref[...],
                               quarter=A_QK_DIM // 4)
    rope_h = functools.partial(_rope, cos=ch_ref[...], sin_next=shn_ref[...], sin_prev=shp_ref[...],
                               quarter=HEAD_DIM // 4)
    qad_ref[0, :, 0:256] = (rope_a(seg(OFF_AQ, OFF_AK)) * (LOG2E * A_QK_DIM ** -0.5)).astype(BF16)
    _store_k(kad_ref, rope_a(seg(OFF_AK, OFF_AV)))
    _store_vt(vadt_ref, seg(OFF_AV, OFF_B))
    qc_ref[0] = (rope_h(seg(OFF_CQ, OFF_CK)) * (LOG2E * HEAD_DIM ** -0.5)).astype(BF16)
    kv = seg(OFF_CK, OFF_DQ)
    _store_k(kc_ref, rope_h(kv[:, :OFF_CV - OFF_CK]))
    _store_vt(vct_ref, kv[:, OFF_CV - OFF_CK:])
    qd = _head_rms(seg(OFF_DQ, OFF_DK), gsum_ref, qnw_ref[...])
    qad_ref[0, :, 256:512] = (rope_h(qd) * (LOG2E * HEAD_DIM ** -0.5)).astype(BF16)
    kv = seg(OFF_DK, IN_W)
    kd = _head_rms(kv[:, :OFF_DV - OFF_DK], gsum_ref, knw_ref[...])
    _store_k(kad_ref, rope_h(kd), tile0=2)
    _store_vt(vadt_ref, kv[:, OFF_DV - OFF_DK:], head0=4)
    g3_ref[0] = seg(OFF_B, OFF_CQ)


def _ffn_in_projection(x, mods, mod_row, w_gu, w_dn, ln_g, ln_b, alpha, w_in, layer, tabs_a, tabs_h, qnw, knw, gsum):
    bsz, t, d = x.shape
    d_ff = w_dn.shape[1]
    tm = min(TOK_TILE, t)
    nc = t // tm
    const = lambda b, i: (0, 0)
    tab = pl.BlockSpec((tm, 128), lambda b, i: (i, 0))
    q_spec = pl.BlockSpec((1, tm, 256), lambda b, i: (b, i, 0))

    def k_spec(w):
        return pl.BlockSpec((1, w // K_TILE, tm, K_TILE), lambda b, i: (b, 0, i, 0))

    cw = min(ATTN_CHUNK, t)
    per_chunk = cw // tm

    def vt_spec(w):
        return pl.BlockSpec((1, 1, w // HEAD_DIM, V_ROWS, tm),
                            lambda b, i: (b, i // per_chunk, 0, 0, i % per_chunk))

    def q_shape():
        return jax.ShapeDtypeStruct((bsz, t, 256), BF16)

    def k_shape(w):
        return jax.ShapeDtypeStruct((bsz, w // K_TILE, t, K_TILE), BF16)

    def vt_shape(w):
        return jax.ShapeDtypeStruct((bsz, t // cw, w // HEAD_DIM, V_ROWS, cw), BF16)

    x_spec = pl.BlockSpec((1, tm, d), lambda b, i: (b, i, 0))
    return pl.pallas_call(
        functools.partial(_ffn_inproj_kernel, alpha=alpha),
        grid=(bsz, nc),
        in_specs=[
            x_spec,
            pl.BlockSpec((None, 1, N_MOD, d), lambda b, i: (layer, mod_row(b), 0, 0)),
            pl.BlockSpec((None, d, 2 * d_ff), lambda b, i: (layer, 0, 0), pipeline_mode=pl.Buffered(1)),
            pl.BlockSpec((None, d_ff, d), lambda b, i: (layer, 0, 0), pipeline_mode=pl.Buffered(1)),
            pl.BlockSpec((1, d), const),
            pl.BlockSpec((1, d), const),
            pl.BlockSpec((None, d, IN_W), lambda b, i: (layer, 0, 0), pipeline_mode=pl.Buffered(1)),
            tab, tab, tab, tab, tab, tab,
            pl.BlockSpec((1, 256), const),
            pl.BlockSpec((1, 128), const),
            pl.BlockSpec((256, 256), const),
        ],
        out_specs=[x_spec, pl.BlockSpec((1, tm, 512), lambda b, i: (b, i, 0)), k_spec(384), vt_spec(384),
                   pl.BlockSpec((1, tm, 768), lambda b, i: (b, i, 0)),
                   q_spec, k_spec(128), vt_spec(128)],
        out_shape=[jax.ShapeDtypeStruct(x.shape, F32), jax.ShapeDtypeStruct((bsz, t, 512), BF16),
                   k_shape(384), vt_shape(384),
                   jax.ShapeDtypeStruct((bsz, t, 768), F32),
                   q_shape(), k_shape(128), vt_shape(128)],
        scratch_shapes=[pltpu.VMEM((tm, d), F32)],
        compiler_params=_params("parallel", "parallel"),
        name="ffn_in_projection",
    )(x, mods, w_gu, w_dn, ln_g.reshape(1, d), ln_b.reshape(1, d), w_in, *tabs_a, *tabs_h, qnw, knw, gsum)


def _whole_per_batch(arr):
    zeros = (0,) * (arr.ndim - 1)
    return pl.BlockSpec((1,) + arr.shape[1:], lambda b, i: (b,) + zeros, pipeline_mode=pl.Buffered(1))


def _attn_scratch(n_maps, tq, ctx_rows, stage_rows):
    shapes = ([((n_maps, K_TILE, tq), BF16), ((n_maps, V_ROWS, tq), F32), ((n_maps, 8, tq), F32),
               ((n_maps, ctx_rows, tq), F32), ((n_maps, 8, tq), F32)]
              + [((stage_rows, tq), F32)] * STAGES + [((8, tq), F32)] * STAGES)
    return [pltpu.VMEM(shape, dtype) for shape, dtype in shapes]


MAPS_PER_KEY_TILE = 4
MAPS_PER_V_HEAD = 2


def _map_table(segments):
    table, q_col, key_col = [], 0, 0
    for kind, n_maps, dq in segments:
        grouped = kind == "plain"
        for j in range(n_maps):
            table.append((q_col + j * dq, dq, key_col + ((j // 2) * dq if grouped else j * dq)))
        q_col += n_maps * dq
        key_col += (n_maps // 2) * dq if grouped else n_maps * dq
        assert key_col % K_TILE == 0 and n_maps % MAPS_PER_KEY_TILE == 0
    assert all(kc // K_TILE == j // MAPS_PER_KEY_TILE for j, (_, _, kc) in enumerate(table))
    return tuple(table)


def _fill_qpad(q_ref, qpad_ref, table):
    tq = q_ref.shape[1]
    qt = q_ref[0].astype(F32).T
    for j, (q_col, dq, key_col) in enumerate(table):
        row = key_col % K_TILE
        parts = []
        if row:
            parts.append(jnp.zeros((row, tq), F32))
        parts.append(qt[q_col:q_col + dq, :])
        if K_TILE - row - dq:
            parts.append(jnp.zeros((K_TILE - row - dq, tq), F32))
        qpad_ref[j] = jnp.concatenate(parts, axis=0).astype(BF16)


def _score_unit(keys, j, nk, qpad_ref, s_ref, bm_ref, valid=None):
    tq = s_ref.shape[1]
    w = qpad_ref[j]
    bm = None
    for r in range(nk // SCORE_ROWS):
        rows = slice(r * SCORE_ROWS, (r + 1) * SCORE_ROWS)
        s = jnp.dot(keys(rows), w, preferred_element_type=F32)
        if valid is not None:
            s = jnp.where(valid(rows), s, NEG)
        s_ref[rows, :] = s
        part = jnp.max(s.reshape(SCORE_ROWS // 8, 8, tq), axis=0)
        bm = part if bm is None else jnp.maximum(bm, part)
    bm_ref[...] = bm


def _softmax_unit(values, j, nk, s_ref, bm_ref, m_ref, acc_ref, row0):
    m_old = m_ref[j, 0:1, :]
    m_new = jnp.maximum(m_old, jnp.max(bm_ref[...], axis=0, keepdims=True))
    alpha = jnp.exp2(m_old - m_new)
    pv = None
    for r in range(nk // PV_ROWS):
        rows = pl.ds(pl.multiple_of(row0 + r * PV_ROWS, PV_ROWS), PV_ROWS)
        p = jnp.exp2(s_ref[rows, :] - m_new)
        d = jnp.dot(values(slice(r * PV_ROWS, (r + 1) * PV_ROWS)), p.astype(BF16), preferred_element_type=F32)
        pv = d if pv is None else pv + d
    acc_ref[j] = alpha * acc_ref[j] + pv
    m_ref[j] = jnp.broadcast_to(m_new, m_ref.shape[1:])


def _init_state(m_ref, acc_ref, n_maps, sink_ref):
    tq = m_ref.shape[-1]
    for j in range(n_maps):
        if sink_ref is None:
            m_ref[j] = jnp.full(m_ref.shape[1:], NEG, F32)
            acc_ref[j] = jnp.zeros((V_ROWS, tq), F32)
        else:
            m_ref[j] = jnp.full(m_ref.shape[1:], sink_ref[j] * LOG2E, F32)
            acc_ref[j] = jnp.concatenate([jnp.zeros((HEAD_DIM, tq), F32), jnp.ones((ONES_ROWS, tq), F32)], axis=0)


def _normalised(acc_ref, j):
    acc = acc_ref[j]
    return acc[0:HEAD_DIM, :] / acc[HEAD_DIM:HEAD_DIM + 1, :]


def _finish_plain(o_ref, acc_ref, first, n_maps):
    ot = jnp.concatenate([_normalised(acc_ref, first + j) for j in range(n_maps)], axis=0)
    o_ref[0] = ot.T.astype(BF16)


def _finish_diff(o_ref, acc_ref, lamp_ref, subw_ref, first, n_maps, lambda_init):
    lp = lamp_ref[...]
    lam = (jnp.exp(jnp.sum(lp[0:1] * lp[1:2], axis=1, keepdims=True))
           - jnp.exp(jnp.sum(lp[2:3] * lp[3:4], axis=1, keepdims=True)) + lambda_init)
    outs = []
    for hd in range(n_maps // 2):
        o = (_normalised(acc_ref, first + 2 * hd)
             - lam * _normalised(acc_ref, first + 2 * hd + 1))
        ms = jnp.mean(o * o, axis=0, keepdims=True)
        outs.append(o * lax.rsqrt(ms + EPS) * subw_ref[...] * (1.0 - lambda_init))
    o_ref[0] = jnp.concatenate(outs, axis=0).T.astype(BF16)


def _dense_attn_kernel(*refs, segments, has_latent, has_sink, diff_lambda_init):
    refs = list(refs)
    q_ref, kc_ref, vct_ref, zero_ref = refs[:4]
    pos = 4
    k_ref = vt_ref = sink_ref = lamp_ref = subw_ref = None
    if has_latent:
        k_ref, vt_ref = refs[pos:pos + 2]
        pos += 2
    if has_sink:
        sink_ref = refs[pos]
        pos += 1
    if diff_lambda_init is not None:
        lamp_ref, subw_ref = refs[pos:pos + 2]
        pos += 2
    o_refs = refs[pos:pos + len(segments)]
    pos += len(segments)
    qpad_ref, acc_ref, m_ref, cs_ref, cbm_ref = refs[pos:pos + 5]
    s_refs = refs[pos + 5:pos + 5 + STAGES]
    bm_refs = refs[pos + 5 + STAGES:pos + 5 + 2 * STAGES]

    table = _map_table(segments)
    n_maps = len(table)
    _fill_qpad(q_ref, qpad_ref, table)
    _init_state(m_ref, acc_ref, n_maps, sink_ref)
    n_ctx = kc_ref.shape[2]
    row0 = zero_ref[0]

    for j in range(n_maps):
        _score_unit(lambda rows: kc_ref[0, j // MAPS_PER_KEY_TILE, rows, :], j, n_ctx, qpad_ref, cs_ref.at[j],
                    cbm_ref.at[j])

    if has_latent:
        tk = k_ref.shape[3]
        n_units = n_maps * k_ref.shape[2]

        def score(u, slot):
            j, c = u % n_maps, u // n_maps
            _score_unit(lambda rows: k_ref[0, j // MAPS_PER_KEY_TILE, c, rows, :], j, tk, qpad_ref, s_refs[slot],
                        bm_refs[slot])

        def softmax(u, slot):
            j, c = u % n_maps, u // n_maps
            _softmax_unit(lambda cols: vt_ref[0, c, j // MAPS_PER_V_HEAD, :, cols], j, tk, s_refs[slot],
                          bm_refs[slot], m_ref, acc_ref, row0)

        for u in range(LOOKAHEAD):
            score(u, u % STAGES)

    for j in range(n_maps):
        _softmax_unit(lambda cols: vct_ref[0, 0, j // MAPS_PER_V_HEAD, :, cols], j, n_ctx, cs_ref.at[j],
                      cbm_ref.at[j], m_ref, acc_ref, row0)

    if has_latent:
        per_iter = STAGES * math.gcd(n_units // STAGES, ROUNDS_PER_ITER)

        def body(i, carry):
            u0 = per_iter * i
            for k in range(per_iter):
                score(jnp.minimum(u0 + k + LOOKAHEAD, n_units - 1), (k + LOOKAHEAD) % STAGES)
                softmax(u0 + k, k % STAGES)
            return carry
        lax.fori_loop(0, n_units // per_iter, body, 0)
    first = 0
    for (kind, seg_maps, _), o_ref in zip(segments, o_refs):
        if kind == "diff":
            _finish_diff(o_ref, acc_ref, lamp_ref, subw_ref, first, seg_maps, diff_lambda_init)
        else:
            _finish_plain(o_ref, acc_ref, first, seg_maps)
        first += seg_maps


def _dense_attention(q, k_ctx, vt_ctx, k_lat, vt_lat, *, segments, sink=None, lamp=None, subw=None,
                     diff_lambda_init=None):
    bsz, t, qw = q.shape
    key_tiles, n_ctx = k_ctx.shape[1:3]
    n_maps = len(_map_table(segments))
    assert vt_ctx.shape[1] == 1 and vt_ctx.shape[-1] == n_ctx
    assert key_tiles * MAPS_PER_KEY_TILE == n_maps and vt_ctx.shape[2] * MAPS_PER_V_HEAD == n_maps
    tq = min(DENSE_Q_TILE, t)
    has_latent = k_lat is not None
    whole = _whole_per_batch
    assert n_maps % STAGES == 0 and n_ctx % PV_ROWS == 0

    in_specs = [pl.BlockSpec((1, tq, qw), lambda b, i: (b, i, 0)), whole(k_ctx), whole(vt_ctx),
                pl.BlockSpec(memory_space=pltpu.SMEM)]
    args = [q, k_ctx, vt_ctx, jnp.zeros((1,), jnp.int32)]
    stage_rows = 8
    if has_latent:
        n_chunks, tk = vt_lat.shape[1], vt_lat.shape[-1]
        assert tk % PV_ROWS == 0 and tk % SCORE_ROWS == 0 and n_chunks * tk == k_lat.shape[2]
        k_lat = k_lat.reshape(bsz, key_tiles, n_chunks, tk, K_TILE)
        in_specs += [whole(k_lat), whole(vt_lat)]
        args += [k_lat, vt_lat]
        stage_rows = tk
    if sink is not None:
        in_specs.append(pl.BlockSpec(memory_space=pltpu.SMEM))
        args.append(sink)
    if diff_lambda_init is not None:
        in_specs += [pl.BlockSpec(lamp.shape, lambda b, i: (0, 0)), pl.BlockSpec(subw.shape, lambda b, i: (0, 0))]
        args += [lamp, subw]
    scratch = _attn_scratch(n_maps, tq, n_ctx, stage_rows)
    return pl.pallas_call(
        functools.partial(_dense_attn_kernel, segments=segments, has_latent=has_latent,
                          has_sink=sink is not None, diff_lambda_init=diff_lambda_init),
        grid=(bsz, t // tq),
        in_specs=in_specs,
        out_specs=[pl.BlockSpec((1, tq, 256), lambda b, i: (b, i, 0))] * len(segments),
        out_shape=[jax.ShapeDtypeStruct((bsz, t, 256), BF16)] * len(segments),
        scratch_shapes=scratch,
        compiler_params=_params("parallel", "parallel"),
        name="dense_attention",
    )(*args)


def _window_attn_kernel(q_ref, kp_ref, kc_ref, kn_ref, vtp_ref, vtc_ref, vtn_ref, kctx_ref, vtctx_ref, zero_ref,
                        sink_ref, o_ref, qpad_ref, acc_ref, m_ref, cs_ref, cbm_ref, *stage_refs, n_maps, seq):
    s_refs, bm_refs = stage_refs[:STAGES], stage_refs[STAGES:]
    tq = q_ref.shape[1]
    q0 = pl.program_id(1) * tq
    _fill_qpad(q_ref, qpad_ref, _map_table((("plain", n_maps, HEAD_DIM),)))
    _init_state(m_ref, acc_ref, n_maps, sink_ref)
    k_win = jnp.concatenate([kp_ref[0], kc_ref[0], kn_ref[0]], axis=0)
    vt_win = jnp.concatenate([vtp_ref[0, 0], vtc_ref[0, 0], vtn_ref[0, 0]], axis=2)
    n_ctx, n_win = kctx_ref.shape[2], k_win.shape[0]
    row0 = zero_ref[0]

    def in_window(rows):
        kpos = q0 - WINDOW + rows.start + lax.broadcasted_iota(jnp.int32, (rows.stop - rows.start, tq), 0)
        qpos = q0 + lax.broadcasted_iota(jnp.int32, (rows.stop - rows.start, tq), 1)
        return (jnp.abs(kpos - qpos) <= WINDOW) & (kpos >= 0) & (kpos < seq)

    for j in range(n_maps):
        _score_unit(lambda rows: kctx_ref[0, 0, rows, :], j, n_ctx, qpad_ref, cs_ref.at[j], cbm_ref.at[j])
    for j in range(n_maps):
        _score_unit(lambda rows: k_win[rows, :], j, n_win, qpad_ref, s_refs[j], bm_refs[j], valid=in_window)
    for j in range(n_maps):
        _softmax_unit(lambda cols: vtctx_ref[0, 0, j // 2, :, cols], j, n_ctx, cs_ref.at[j], cbm_ref.at[j],
                      m_ref, acc_ref, row0)
    for j in range(n_maps):
        _softmax_unit(lambda cols: vt_win[j // 2][:, cols], j, n_win, s_refs[j], bm_refs[j], m_ref, acc_ref, row0)
    _finish_plain(o_ref, acc_ref, 0, n_maps)


def _window_attention(q, k_ctx, vt_ctx, k_lat, vt_lat, sink):
    bsz, t, qw = q.shape
    key_tiles, _, kw = k_lat.shape[1:]
    assert key_tiles == 1
    tq = Q_TILE
    n_maps = 4
    k_flat = k_lat.reshape(bsz, t, kw)
    wb = tq // WINDOW
    nwb = t // WINDOW
    heads, cw = vt_lat.shape[2], vt_lat.shape[-1]
    whole = _whole_per_batch

    def vt_blk(width, blk):
        per_chunk = cw // width
        return pl.BlockSpec((1, 1, heads, V_ROWS, width),
                            lambda b, i: (b, blk(i) // per_chunk, 0, 0, blk(i) % per_chunk))

    def prev_blk(i):
        return jnp.maximum(i * wb - 1, 0)

    def next_blk(i):
        return jnp.minimum(i * wb + wb, nwb - 1)

    in_specs = [
        pl.BlockSpec((1, tq, qw), lambda b, i: (b, i, 0)),
        pl.BlockSpec((1, WINDOW, kw), lambda b, i: (b, prev_blk(i), 0)),
        pl.BlockSpec((1, tq, kw), lambda b, i: (b, i, 0)),
        pl.BlockSpec((1, WINDOW, kw), lambda b, i: (b, next_blk(i), 0)),
        vt_blk(WINDOW, prev_blk),
        vt_blk(tq, lambda i: i),
        vt_blk(WINDOW, next_blk),
        whole(k_ctx), whole(vt_ctx),
        pl.BlockSpec(memory_space=pltpu.SMEM),
        pl.BlockSpec(memory_space=pltpu.SMEM),
    ]
    assert n_maps == STAGES
    return pl.pallas_call(
        functools.partial(_window_attn_kernel, n_maps=n_maps, seq=t),
        grid=(bsz, t // tq),
        in_specs=in_specs,
        out_specs=pl.BlockSpec((1, tq, 256), lambda b, i: (b, i, 0)),
        out_shape=jax.ShapeDtypeStruct((bsz, t, 256), BF16),
        scratch_shapes=_attn_scratch(n_maps, tq, k_ctx.shape[2], tq + 2 * WINDOW),
        compiler_params=_params("parallel", "parallel"),
        name="window_attention",
    )(q, k_flat, k_flat, k_flat, vt_lat, vt_lat, vt_lat, k_ctx, vt_ctx, jnp.zeros((1,), jnp.int32), sink)


def _outproj_ffn_kernel(x_ref, mod_ref, ya_ref, g3_ref, g3p_ref, g3n_ref, yw_ref, yg_ref, cw_ref, w_ref, g_ref, b_ref,
                        wgu_ref, wdn_ref, g2_ref, b2_ref, o_ref, acc_ref, *, alpha):
    i = pl.program_id(1)
    last = pl.num_programs(1) - 1
    tm = x_ref.shape[1]
    gb = g3_ref[0, :, 0:256]
    hid = g3_ref[0, :, 256:512] * g3_ref[0, :, 512:768]
    halo = g3p_ref.shape[1]
    h_prev = g3p_ref[0, halo - 1:halo, 256:512] * g3p_ref[0, halo - 1:halo, 512:768]
    h_next = g3n_ref[0, 0:1, 256:512] * g3n_ref[0, 0:1, 512:768]
    h_prev = jnp.where(i == 0, 0.0, h_prev)
    h_next = jnp.where(i == last, 0.0, h_next)
    row = lax.broadcasted_iota(jnp.int32, hid.shape, 0)
    below = jnp.where(row == 0, h_prev, pltpu.roll(hid, 1, axis=0))
    above = jnp.where(row == tm - 1, h_next, pltpu.roll(hid, tm - 1, axis=0))
    yb = gb * (cw_ref[0:1, :] * below + cw_ref[1:2, :] * hid + cw_ref[2:3, :] * above)
    y = jnp.dot(ya_ref[0], w_ref[0:256, :], preferred_element_type=F32)
    y += jnp.dot(yb.astype(BF16), w_ref[256:512, :], preferred_element_type=F32)
    y += jnp.dot(yw_ref[0], w_ref[512:768, :], preferred_element_type=F32)
    y += jnp.dot(yg_ref[0], w_ref[768:1024, :], preferred_element_type=F32)
    gate = mod_ref[0, 5:6, :]
    x_mix = _layer_norm(alpha * x_ref[0] + gate * y, g_ref[...], b_ref[...])
    o_ref[0] = _ffn_body(x_mix, mod_ref, wgu_ref, wdn_ref, g2_ref, b2_ref, acc_ref, 6, alpha)


def _out_projection_ffn(x, mods, mod_row, ya, g3, yw, yg, conv_w, w_out, w_gu, w_dn, layer, ln_g, ln_b, ln_g2, ln_b2,
                        alpha):
    bsz, t, d = x.shape
    d_ff = w_dn.shape[1]
    tm = min(TOK_TILE, t)
    halo = 8
    hb = tm // halo
    n_halo = t // halo
    const = lambda b, i: (0, 0)
    y_spec = pl.BlockSpec((1, tm, 256), lambda b, i: (b, i, 0))
    return pl.pallas_call(
        functools.partial(_outproj_ffn_kernel, alpha=alpha),
        grid=(bsz, t // tm),
        in_specs=[
            pl.BlockSpec((1, tm, d), lambda b, i: (b, i, 0)),
            pl.BlockSpec((None, 1, N_MOD, d), lambda b, i: (layer, mod_row(b), 0, 0)),
            y_spec,
            pl.BlockSpec((1, tm, 768), lambda b, i: (b, i, 0)),
            pl.BlockSpec((1, halo, 768), lambda b, i: (b, jnp.maximum(i * hb - 1, 0), 0)),
            pl.BlockSpec((1, halo, 768), lambda b, i: (b, jnp.minimum(i * hb + hb, n_halo - 1), 0)),
            y_spec, y_spec,
            pl.BlockSpec(conv_w.shape, const),
            pl.BlockSpec((None,) + w_out.shape[1:], lambda b, i: (layer, 0, 0), pipeline_mode=pl.Buffered(1)),
            pl.BlockSpec((1, d), const),
            pl.BlockSpec((1, d), const),
            pl.BlockSpec((None, d, 2 * d_ff), lambda b, i: (layer, 0, 0), pipeline_mode=pl.Buffered(1)),
            pl.BlockSpec((None, d_ff, d), lambda b, i: (layer, 0, 0), pipeline_mode=pl.Buffered(1)),
            pl.BlockSpec((1, d), const),
            pl.BlockSpec((1, d), const),
        ],
        out_specs=pl.BlockSpec((1, tm, d), lambda b, i: (b, i, 0)),
        out_shape=jax.ShapeDtypeStruct(x.shape, F32),
        scratch_shapes=[pltpu.VMEM((tm, d), F32)],
        compiler_params=_params("parallel", "parallel"),
        name="out_projection_ffn",
    )(x, mods, ya, g3, g3, g3, yw, yg, conv_w, w_out, ln_g.reshape(1, d), ln_b.reshape(1, d), w_gu, w_dn,
      ln_g2.reshape(1, d), ln_b2.reshape(1, d))


def _rope_tables(rows, dim):
    n_freq = dim // 4
    inv_freq = ROPE_BASE ** (-jnp.arange(n_freq, dtype=F32) / n_freq)
    ang_r = jnp.arange(rows, dtype=F32)[:, None] * inv_freq
    ang_c = jnp.arange(GRID_W, dtype=F32)[:, None] * inv_freq

    def on_grid(of_row, of_col):
        r = jnp.broadcast_to(of_row[:, None, :], (rows, GRID_W, n_freq))
        c = jnp.broadcast_to(of_col[None, :, :], (rows, GRID_W, n_freq))
        return jnp.concatenate([r, r, c, c], axis=-1).reshape(rows * GRID_W, dim)

    cos = on_grid(jnp.cos(ang_r), jnp.cos(ang_c))
    sin = on_grid(jnp.sin(ang_r), jnp.sin(ang_c))
    reps = 128 // dim
    cos = jnp.tile(cos, (1, reps))
    sin = jnp.tile(sin, (1, reps))
    first_half = (jnp.arange(128) % (dim // 2)) < (dim // 4)
    sin_next = jnp.where(first_half, -sin, 0.0)
    sin_prev = jnp.where(first_half, 0.0, sin)
    return cos, sin_next, sin_prev


def _identity_tables(t):
    return jnp.ones((t, 128), F32), jnp.zeros((t, 128), F32), jnp.zeros((t, 128), F32)


def kernel(x, c, ctx, c_ctx, w_mod, b_mod, ln_g, ln_b, w_gu1, w_dn1, w_in, w_out, conv_w, lam_q1, lam_k1, lam_q2,
           lam_k2, subln_w, sink, qn_w, kn_w, w_gu2, w_dn2):
    bsz, seq, d = x.shape
    ctx_len = ctx.shape[1]
    depth = w_mod.shape[0]
    alpha = (2.0 * depth) ** 0.25
    assert seq % TOK_TILE == 0 and ctx_len % Q_TILE == 0 and bsz < MOD_ROWS

    tabs_a = _rope_tables(seq // GRID_W, A_QK_DIM)
    tabs_h = _rope_tables(seq // GRID_W, HEAD_DIM)
    tabs_id = _identity_tables(ctx_len)
    gsum = (jnp.arange(256)[:, None] // HEAD_DIM == jnp.arange(256)[None, :] // HEAD_DIM).astype(BF16)

    cvec = jnp.zeros((MOD_ROWS, d), F32).at[:bsz].set(c).at[bsz].set(c_ctx)
    mods_all = _mod_vectors(cvec, w_mod, b_mod).reshape(depth, MOD_ROWS, N_MOD, d)
    lat_row = lambda b: b
    ctx_row = lambda b: bsz

    wgu1, wdn1, wgu2, wdn2 = (w.astype(BF16) for w in (w_gu1, w_dn1, w_gu2, w_dn2))
    win, wout = w_in.astype(BF16), w_out.astype(BF16)

    xc = ctx
    for l in range(depth):
        need_ctx = l < depth - 1
        lambda_init = 0.8 - 0.6 * math.exp(-0.3 * l)
        mods = mods_all
        qnw = jnp.tile(qn_w[l], 256 // HEAD_DIM).reshape(1, 256)
        knw = jnp.tile(kn_w[l], 128 // HEAD_DIM).reshape(1, 128)
        lamp = jnp.stack([lam_q1[l], lam_k1[l], lam_q2[l], lam_k2[l]])
        subw = subln_w[l].reshape(HEAD_DIM, 1)

        x, qad, kad, vadt, g3, qc, kc, vct = _ffn_in_projection(
            x, mods, lat_row, wgu1, wdn1, ln_g[l, 0], ln_b[l, 0], alpha, win, l, tabs_a, tabs_h, qnw, knw, gsum)
        xc, qad_c, kad_c, vadt_c, g3_c, qc_c, kc_c, vct_c = _ffn_in_projection(
            xc, mods, ctx_row, wgu1, wdn1, ln_g[l, 0], ln_b[l, 0], alpha, win, l, tabs_id, tabs_id, qnw, knw, gsum)

        dense = (("diff", 8, A_QK_DIM), ("plain", 4, HEAD_DIM))
        ya, yg = _dense_attention(qad, kad_c, vadt_c, kad, vadt, segments=dense, lamp=lamp, subw=subw,
                                  diff_lambda_init=lambda_init)
        yw = _window_attention(qc, kc_c, vct_c, kc, vct, sink[l])
        x = _out_projection_ffn(x, mods, lat_row, ya, g3, yw, yg, conv_w[l], wout, wgu2, wdn2, l,
                                ln_g[l, 1], ln_b[l, 1], ln_g[l, 2], ln_b[l, 2], alpha)

        if need_ctx:
            ya_c, yg_c = _dense_attention(qad_c, kad_c, vadt_c, None, None, segments=dense, lamp=lamp, subw=subw,
                                          diff_lambda_init=lambda_init)
            yw_c, = _dense_attention(qc_c, kc_c, vct_c, None, None, segments=(("plain", 4, HEAD_DIM),),
                                     sink=sink[l])
            xc = _out_projection_ffn(xc, mods, ctx_row, ya_c, g3_c, yw_c, yg_c, conv_w[l], wout, wgu2, wdn2, l,
                                     ln_g[l, 1], ln_b[l, 1], ln_g[l, 2], ln_b[l, 2], alpha)
    return x
```

```python
import functools
import math

import jax
import jax.numpy as jnp
from jax import lax
from jax.experimental import pallas as pl
from jax.experimental.pallas import tpu as pltpu

F32 = jnp.float32
BF16 = jnp.bfloat16

GRID_W = 64
HEAD_DIM = 64
WINDOW = 128
ROPE_BASE = 10000.0
EPS = 1e-6
NEG = -1e30
A_QK_DIM = 32
N_MOD = 9
MOD_ROWS = 8
VMEM_LIMIT = 56 * 1024 * 1024

OFF_AQ, OFF_AK, OFF_AV = 0, 256, 512
OFF_B = 768
OFF_CQ, OFF_CK, OFF_CV = 1536, 1792, 1920
OFF_DQ, OFF_DK, OFF_DV = 2048, 2304, 2432
IN_W = 2560

TOK_TILE = 512
Q_TILE = 256
DENSE_Q_TILE = 256
FF_CHUNK = 256
SCORE_ROWS = 128
PV_ROWS = 256
ATTN_CHUNK = 1024
STAGES = 4
LOOKAHEAD = 2
ROUNDS_PER_ITER = 16
K_TILE = 128
ONES_ROWS = 16
V_ROWS = HEAD_DIM + ONES_ROWS
LOG2E = math.log2(math.e)


def _params(*sem):
    return pltpu.CompilerParams(dimension_semantics=sem, vmem_limit_bytes=VMEM_LIMIT)


def _layer_norm(r, g, b):
    mu = jnp.mean(r, axis=-1, keepdims=True)
    d = r - mu
    var = jnp.mean(d * d, axis=-1, keepdims=True)
    return d * lax.rsqrt(var + EPS) * g + b


def _mod_kernel(c_ref, w_ref, b_ref, o_ref):
    c = c_ref[...]
    a = (c * jax.nn.sigmoid(c)).astype(BF16)
    o_ref[0] = jnp.dot(a, w_ref[0].astype(BF16), preferred_element_type=F32) + b_ref[0]


def _mod_vectors(cvec, w_mod, b_mod):
    depth, d, n = w_mod.shape
    tn = n // 8
    return pl.pallas_call(
        _mod_kernel,
        grid=(depth, n // tn),
        in_specs=[
            pl.BlockSpec((MOD_ROWS, d), lambda l, j: (0, 0)),
            pl.BlockSpec((1, d, tn), lambda l, j: (l, 0, j)),
            pl.BlockSpec((1, 1, tn), lambda l, j: (l, 0, j)),
        ],
        out_specs=pl.BlockSpec((1, MOD_ROWS, tn), lambda l, j: (l, 0, j)),
        out_shape=jax.ShapeDtypeStruct((depth, MOD_ROWS, n), F32),
        compiler_params=_params("parallel", "parallel"),
        name="mod_vectors",
    )(cvec, w_mod, b_mod.reshape(depth, 1, n))


def _ffn_body(x, mod_ref, wgu_ref, wdn_ref, g_ref, b_ref, acc_ref, mod_base, alpha):
    shift = mod_ref[0, mod_base:mod_base + 1, :]
    scale = mod_ref[0, mod_base + 1:mod_base + 2, :]
    gate = mod_ref[0, mod_base + 2:mod_base + 3, :]
    h = (x * (1.0 + scale) + shift).astype(BF16)
    d_ff = wdn_ref.shape[0]
    for c in range(d_ff // FF_CHUNK):
        lo = c * FF_CHUNK
        g = jnp.dot(h, wgu_ref[:, lo:lo + FF_CHUNK], preferred_element_type=F32)
        u = jnp.dot(h, wgu_ref[:, d_ff + lo:d_ff + lo + FF_CHUNK], preferred_element_type=F32)
        a = (g * jax.nn.sigmoid(g) * u).astype(BF16)
        y = jnp.dot(a, wdn_ref[lo:lo + FF_CHUNK, :], preferred_element_type=F32)
        if c == 0:
            acc_ref[...] = y
        else:
            acc_ref[...] += y
    r = alpha * x + (0.5 * gate) * acc_ref[...]
    return _layer_norm(r, g_ref[...], b_ref[...])


def _rope(x, cos, sin_next, sin_prev, quarter):
    width = x.shape[1]
    reps = width // cos.shape[1]
    if reps > 1:
        cos = jnp.concatenate([cos] * reps, axis=1)
        sin_next = jnp.concatenate([sin_next] * reps, axis=1)
        sin_prev = jnp.concatenate([sin_prev] * reps, axis=1)
    x_next = pltpu.roll(x, width - quarter, axis=1)
    x_prev = pltpu.roll(x, quarter, axis=1)
    return x * cos + x_next * sin_next + x_prev * sin_prev


def _head_rms(x, gsum_ref, w):
    x2 = x * x
    hi = x2.astype(BF16)
    lo = (x2 - hi.astype(F32)).astype(BF16)
    gs = gsum_ref[:x.shape[1], :x.shape[1]]
    ss = jnp.dot(hi, gs, preferred_element_type=F32) + jnp.dot(lo, gs, preferred_element_type=F32)
    return x * lax.rsqrt(ss * (1.0 / HEAD_DIM) + EPS) * w


def _store_k(k_ref, k, tile0=0):
    for kt in range(k.shape[1] // K_TILE):
        k_ref[0, tile0 + kt] = k[:, kt * K_TILE:(kt + 1) * K_TILE].astype(BF16)


def _store_vt(vt_ref, v, head0=0):
    vt = v.T
    tm = v.shape[0]
    for hd in range(v.shape[1] // HEAD_DIM):
        vt_ref[0, 0, head0 + hd, 0:HEAD_DIM, :] = vt[hd * HEAD_DIM:(hd + 1) * HEAD_DIM, :].astype(BF16)
        vt_ref[0, 0, head0 + hd, HEAD_DIM:V_ROWS, :] = jnp.ones((ONES_ROWS, tm), BF16)


def _ffn_inproj_kernel(x_ref, mod_ref, wgu_ref, wdn_ref, g_ref, b_ref, w_ref,
                       ca_ref, san_ref, sap_ref, ch_ref, shn_ref, shp_ref, qnw_ref, knw_ref, gsum_ref,
                       x1_ref, qad_ref, kad_ref, vadt_ref, g3_ref, qc_ref, kc_ref, vct_ref,
                       acc_ref, *, alpha):
    x = _ffn_body(x_ref[0], mod_ref, wgu_ref, wdn_ref, g_ref, b_ref, acc_ref, 0, alpha)
    x1_ref[0] = x
    shift = mod_ref[0, 3:4, :]
    scale = mod_ref[0, 4:5, :]
    h = (x * (1.0 + scale) + shift).astype(BF16)

    def seg(lo, hi):
        return jnp.dot(h, w_ref[:, lo:hi], preferred_element_type=F32)

    rope_a = functools.partial(_rope, cos=ca_ref[...], sin_next=san_ref[...], sin_prev=sap_ref[...],
                               quarter=A_QK_DIM // 4)
    rope_h = functools.partial(_rope, cos=ch_ref[...], sin_next=shn_ref[...], sin_prev=shp_ref[...],
                               quarter=HEAD_DIM // 4)
    qad_ref[0, :, 0:256] = (rope_a(seg(OFF_AQ, OFF_AK)) * (LOG2E * A_QK_DIM ** -0.5)).astype(BF16)
    _store_k(kad_ref, rope_a(seg(OFF_AK, OFF_AV)))
    _store_vt(vadt_ref, seg(OFF_AV, OFF_B))
    qc_ref[0] = (rope_h(seg(OFF_CQ, OFF_CK)) * (LOG2E * HEAD_DIM ** -0.5)).astype(BF16)
    kv = seg(OFF_CK, OFF_DQ)
    _store_k(kc_ref, rope_h(kv[:, :OFF_CV - OFF_CK]))
    _store_vt(vct_ref, kv[:, OFF_CV - OFF_CK:])
    qd = _head_rms(seg(OFF_DQ, OFF_DK), gsum_ref, qnw_ref[...])
    qad_ref[0, :, 256:512] = (rope_h(qd) * (LOG2E * HEAD_DIM ** -0.5)).astype(BF16)
    kv = seg(OFF_DK, IN_W)
    kd = _head_rms(kv[:, :OFF_DV - OFF_DK], gsum_ref, knw_ref[...])
    _store_k(kad_ref, rope_h(kd), tile0=2)
    _store_vt(vadt_ref, kv[:, OFF_DV - OFF_DK:], head0=4)
    g3_ref[0] = seg(OFF_B, OFF_CQ)


def _ffn_in_projection(x, mods, mod_row, w_gu, w_dn, ln_g, ln_b, alpha, w_in, layer, tabs_a, tabs_h, qnw, knw, gsum):
    bsz, t, d = x.shape
    d_ff = w_dn.shape[1]
    tm = min(TOK_TILE, t)
    nc = t // tm
    const = lambda b, i: (0, 0)
    tab = pl.BlockSpec((tm, 128), lambda b, i: (i, 0))
    q_spec = pl.BlockSpec((1, tm, 256), lambda b, i: (b, i, 0))

    def k_spec(w):
        return pl.BlockSpec((1, w // K_TILE, tm, K_TILE), lambda b, i: (b, 0, i, 0))

    cw = min(ATTN_CHUNK, t)
    per_chunk = cw // tm

    def vt_spec(w):
        return pl.BlockSpec((1, 1, w // HEAD_DIM, V_ROWS, tm),
                            lambda b, i: (b, i // per_chunk, 0, 0, i % per_chunk))

    def q_shape():
        return jax.ShapeDtypeStruct((bsz, t, 256), BF16)

    def k_shape(w):
        return jax.ShapeDtypeStruct((bsz, w // K_TILE, t, K_TILE), BF16)

    def vt_shape(w):
        return jax.ShapeDtypeStruct((bsz, t // cw, w // HEAD_DIM, V_ROWS, cw), BF16)

    x_spec = pl.BlockSpec((1, tm, d), lambda b, i: (b, i, 0))
    return pl.pallas_call(
        functools.partial(_ffn_inproj_kernel, alpha=alpha),
        grid=(bsz, nc),
        in_specs=[
            x_spec,
            pl.BlockSpec((None, 1, N_MOD, d), lambda b, i: (layer, mod_row(b), 0, 0)),
            pl.BlockSpec((None, d, 2 * d_ff), lambda b, i: (layer, 0, 0), pipeline_mode=pl.Buffered(1)),
            pl.BlockSpec((None, d_ff, d), lambda b, i: (layer, 0, 0), pipeline_mode=pl.Buffered(1)),
            pl.BlockSpec((1, d), const),
            pl.BlockSpec((1, d), const),
            pl.BlockSpec((None, d, IN_W), lambda b, i: (layer, 0, 0), pipeline_mode=pl.Buffered(1)),
            tab, tab, tab, tab, tab, tab,
            pl.BlockSpec((1, 256), const),
            pl.BlockSpec((1, 128), const),
            pl.BlockSpec((256, 256), const),
        ],
        out_specs=[x_spec, pl.BlockSpec((1, tm, 512), lambda b, i: (b, i, 0)), k_spec(384), vt_spec(384),
                   pl.BlockSpec((1, tm, 768), lambda b, i: (b, i, 0)),
                   q_spec, k_spec(128), vt_spec(128)],
        out_shape=[jax.ShapeDtypeStruct(x.shape, F32), jax.ShapeDtypeStruct((bsz, t, 512), BF16),
                   k_shape(384), vt_shape(384),
                   jax.ShapeDtypeStruct((bsz, t, 768), F32),
                   q_shape(), k_shape(128), vt_shape(128)],
        scratch_shapes=[pltpu.VMEM((tm, d), F32)],
        compiler_params=_params("parallel", "parallel"),
        name="ffn_in_projection",
    )(x, mods, w_gu, w_dn, ln_g.reshape(1, d), ln_b.reshape(1, d), w_in, *tabs_a, *tabs_h, qnw, knw, gsum)


def _whole_per_batch(arr):
    zeros = (0,) * (arr.ndim - 1)
    return pl.BlockSpec((1,) + arr.shape[1:], lambda b, i: (b,) + zeros, pipeline_mode=pl.Buffered(1))


def _attn_scratch(n_maps, tq, ctx_rows, stage_rows):
    shapes = ([((n_maps, K_TILE, tq), BF16), ((n_maps, V_ROWS, tq), F32), ((n_maps, 8, tq), F32),
               ((n_maps, ctx_rows, tq), F32), ((n_maps, 8, tq), F32)]
              + [((stage_rows, tq), F32)] * STAGES + [((8, tq), F32)] * STAGES)
    return [pltpu.VMEM(shape, dtype) for shape, dtype in shapes]


MAPS_PER_KEY_TILE = 4
MAPS_PER_V_HEAD = 2


def _map_table(segments):
    table, q_col, key_col = [], 0, 0
    for kind, n_maps, dq in segments:
        grouped = kind == "plain"
        for j in range(n_maps):
            table.append((q_col + j * dq, dq, key_col + ((j // 2) * dq if grouped else j * dq)))
        q_col += n_maps * dq
        key_col += (n_maps // 2) * dq if grouped else n_maps * dq
        assert key_col % K_TILE == 0 and n_maps % MAPS_PER_KEY_TILE == 0
    assert all(kc // K_TILE == j // MAPS_PER_KEY_TILE for j, (_, _, kc) in enumerate(table))
    return tuple(table)


def _fill_qpad(q_ref, qpad_ref, table):
    tq = q_ref.shape[1]
    qt = q_ref[0].astype(F32).T
    for j, (q_col, dq, key_col) in enumerate(table):
        row = key_col % K_TILE
        parts = []
        if row:
            parts.append(jnp.zeros((row, tq), F32))
        parts.append(qt[q_col:q_col + dq, :])
        if K_TILE - row - dq:
            parts.append(jnp.zeros((K_TILE - row - dq, tq), F32))
        qpad_ref[j] = jnp.concatenate(parts, axis=0).astype(BF16)


def _score_unit(keys, j, nk, qpad_ref, s_ref, bm_ref, valid=None):
    tq = s_ref.shape[1]
    w = qpad_ref[j]
    bm = None
    for r in range(nk // SCORE_ROWS):
        rows = slice(r * SCORE_ROWS, (r + 1) * SCORE_ROWS)
        s = jnp.dot(keys(rows), w, preferred_element_type=F32)
        if valid is not None:
            s = jnp.where(valid(rows), s, NEG)
        s_ref[rows, :] = s
        part = jnp.max(s.reshape(SCORE_ROWS // 8, 8, tq), axis=0)
        bm = part if bm is None else jnp.maximum(bm, part)
    bm_ref[...] = bm


def _softmax_unit(values, j, nk, s_ref, bm_ref, m_ref, acc_ref, row0):
    m_old = m_ref[j, 0:1, :]
    m_new = jnp.maximum(m_old, jnp.max(bm_ref[...], axis=0, keepdims=True))
    alpha = jnp.exp2(m_old - m_new)
    pv = None
    for r in range(nk // PV_ROWS):
        rows = pl.ds(pl.multiple_of(row0 + r * PV_ROWS, PV_ROWS), PV_ROWS)
        p = jnp.exp2(s_ref[rows, :] - m_new)
        d = jnp.dot(values(slice(r * PV_ROWS, (r + 1) * PV_ROWS)), p.astype(BF16), preferred_element_type=F32)
        pv = d if pv is None else pv + d
    acc_ref[j] = alpha * acc_ref[j] + pv
    m_ref[j] = jnp.broadcast_to(m_new, m_ref.shape[1:])


def _init_state(m_ref, acc_ref, n_maps, sink_ref):
    tq = m_ref.shape[-1]
    for j in range(n_maps):
        if sink_ref is None:
            m_ref[j] = jnp.full(m_ref.shape[1:], NEG, F32)
            acc_ref[j] = jnp.zeros((V_ROWS, tq), F32)
        else:
            m_ref[j] = jnp.full(m_ref.shape[1:], sink_ref[j] * LOG2E, F32)
            acc_ref[j] = jnp.concatenate([jnp.zeros((HEAD_DIM, tq), F32), jnp.ones((ONES_ROWS, tq), F32)], axis=0)


def _normalised(acc_ref, j):
    acc = acc_ref[j]
    return acc[0:HEAD_DIM, :] / acc[HEAD_DIM:HEAD_DIM + 1, :]


def _finish_plain(o_ref, acc_ref, first, n_maps):
    ot = jnp.concatenate([_normalised(acc_ref, first + j) for j in range(n_maps)], axis=0)
    o_ref[0] = ot.T.astype(BF16)


def _finish_diff(o_ref, acc_ref, lamp_ref, subw_ref, first, n_maps, lambda_init):
    lp = lamp_ref[...]
    lam = (jnp.exp(jnp.sum(lp[0:1] * lp[1:2], axis=1, keepdims=True))
           - jnp.exp(jnp.sum(lp[2:3] * lp[3:4], axis=1, keepdims=True)) + lambda_init)
    outs = []
    for hd in range(n_maps // 2):
        o = (_normalised(acc_ref, first + 2 * hd)
             - lam * _normalised(acc_ref, first + 2 * hd + 1))
        ms = jnp.mean(o * o, axis=0, keepdims=True)
        outs.append(o * lax.rsqrt(ms + EPS) * subw_ref[...] * (1.0 - lambda_init))
    o_ref[0] = jnp.concatenate(outs, axis=0).T.astype(BF16)


def _dense_attn_kernel(*refs, segments, has_latent, has_sink, diff_lambda_init):
    refs = list(refs)
    q_ref, kc_ref, vct_ref, zero_ref = refs[:4]
    pos = 4
    k_ref = vt_ref = sink_ref = lamp_ref = subw_ref = None
    if has_latent:
        k_ref, vt_ref = refs[pos:pos + 2]
        pos += 2
    if has_sink:
        sink_ref = refs[pos]
        pos += 1
    if diff_lambda_init is not None:
        lamp_ref, subw_ref = refs[pos:pos + 2]
        pos += 2
    o_refs = refs[pos:pos + len(segments)]
    pos += len(segments)
    qpad_ref, acc_ref, m_ref, cs_ref, cbm_ref = refs[pos:pos + 5]
    s_refs = refs[pos + 5:pos + 5 + STAGES]
    bm_refs = refs[pos + 5 + STAGES:pos + 5 + 2 * STAGES]

    table = _map_table(segments)
    n_maps = len(table)
    _fill_qpad(q_ref, qpad_ref, table)
    _init_state(m_ref, acc_ref, n_maps, sink_ref)
    n_ctx = kc_ref.shape[2]
    row0 = zero_ref[0]

    for j in range(n_maps):
        _score_unit(lambda rows: kc_ref[0, j // MAPS_PER_KEY_TILE, rows, :], j, n_ctx, qpad_ref, cs_ref.at[j],
                    cbm_ref.at[j])

    if has_latent:
        tk = k_ref.shape[3]
        n_units = n_maps * k_ref.shape[2]

        def score(u, slot):
            j, c = u % n_maps, u // n_maps
            _score_unit(lambda rows: k_ref[0, j // MAPS_PER_KEY_TILE, c, rows, :], j, tk, qpad_ref, s_refs[slot],
                        bm_refs[slot])

        def softmax(u, slot):
            j, c = u % n_maps, u // n_maps
            _softmax_unit(lambda cols: vt_ref[0, c, j // MAPS_PER_V_HEAD, :, cols], j, tk, s_refs[slot],
                          bm_refs[slot], m_ref, acc_ref, row0)

        for u in range(LOOKAHEAD):
            score(u, u % STAGES)

    for j in range(n_maps):
        _softmax_unit(lambda cols: vct_ref[0, 0, j // MAPS_PER_V_HEAD, :, cols], j, n_ctx, cs_ref.at[j],
                      cbm_ref.at[j], m_ref, acc_ref, row0)

    if has_latent:
        per_iter = STAGES * math.gcd(n_units // STAGES, ROUNDS_PER_ITER)

        def body(i, carry):
            u0 = per_iter * i
            for k in range(per_iter):
                score(jnp.minimum(u0 + k + LOOKAHEAD, n_units - 1), (k + LOOKAHEAD) % STAGES)
                softmax(u0 + k, k % STAGES)
            return carry
        lax.fori_loop(0, n_units // per_iter, body, 0)
    first = 0
    for (kind, seg_maps, _), o_ref in zip(segments, o_refs):
        if kind == "diff":
            _finish_diff(o_ref, acc_ref, lamp_ref, subw_ref, first, seg_maps, diff_lambda_init)
        else:
            _finish_plain(o_ref, acc_ref, first, seg_maps)
        first += seg_maps


def _dense_attention(q, k_ctx, vt_ctx, k_lat, vt_lat, *, segments, sink=None, lamp=None, subw=None,
                     diff_lambda_init=None):
    bsz, t, qw = q.shape
    key_tiles, n_ctx = k_ctx.shape[1:3]
    n_maps = len(_map_table(segments))
    assert vt_ctx.shape[1] == 1 and vt_ctx.shape[-1] == n_ctx
    assert key_tiles * MAPS_PER_KEY_TILE == n_maps and vt_ctx.shape[2] * MAPS_PER_V_HEAD == n_maps
    tq = min(DENSE_Q_TILE, t)
    has_latent = k_lat is not None
    whole = _whole_per_batch
    assert n_maps % STAGES == 0 and n_ctx % PV_ROWS == 0

    in_specs = [pl.BlockSpec((1, tq, qw), lambda b, i: (b, i, 0)), whole(k_ctx), whole(vt_ctx),
                pl.BlockSpec(memory_space=pltpu.SMEM)]
    args = [q, k_ctx, vt_ctx, jnp.zeros((1,), jnp.int32)]
    stage_rows = 8
    if has_latent:
        n_chunks, tk = vt_lat.shape[1], vt_lat.shape[-1]
        assert tk % PV_ROWS == 0 and tk % SCORE_ROWS == 0 and n_chunks * tk == k_lat.shape[2]
        k_lat = k_lat.reshape(bsz, key_tiles, n_chunks, tk, K_TILE)
        in_specs += [whole(k_lat), whole(vt_lat)]
        args += [k_lat, vt_lat]
        stage_rows = tk
    if sink is not None:
        in_specs.append(pl.BlockSpec(memory_space=pltpu.SMEM))
        args.append(sink)
    if diff_lambda_init is not None:
        in_specs += [pl.BlockSpec(lamp.shape, lambda b, i: (0, 0)), pl.BlockSpec(subw.shape, lambda b, i: (0, 0))]
        args += [lamp, subw]
    scratch = _attn_scratch(n_maps, tq, n_ctx, stage_rows)
    return pl.pallas_call(
        functools.partial(_dense_attn_kernel, segments=segments, has_latent=has_latent,
                          has_sink=sink is not None, diff_lambda_init=diff_lambda_init),
        grid=(bsz, t // tq),
        in_specs=in_specs,
        out_specs=[pl.BlockSpec((1, tq, 256), lambda b, i: (b, i, 0))] * len(segments),
        out_shape=[jax.ShapeDtypeStruct((bsz, t, 256), BF16)] * len(segments),
        scratch_shapes=scratch,
        compiler_params=_params("parallel", "parallel"),
        name="dense_attention",
    )(*args)


def _window_attn_kernel(q_ref, kp_ref, kc_ref, kn_ref, vtp_ref, vtc_ref, vtn_ref, kctx_ref, vtctx_ref, zero_ref,
                        sink_ref, o_ref, qpad_ref, acc_ref, m_ref, cs_ref, cbm_ref, *stage_refs, n_maps, seq):
    s_refs, bm_refs = stage_refs[:STAGES], stage_refs[STAGES:]
    tq = q_ref.shape[1]
    q0 = pl.program_id(1) * tq
    _fill_qpad(q_ref, qpad_ref, _map_table((("plain", n_maps, HEAD_DIM),)))
    _init_state(m_ref, acc_ref, n_maps, sink_ref)
    k_win = jnp.concatenate([kp_ref[0], kc_ref[0], kn_ref[0]], axis=0)
    vt_win = jnp.concatenate([vtp_ref[0, 0], vtc_ref[0, 0], vtn_ref[0, 0]], axis=2)
    n_ctx, n_win = kctx_ref.shape[2], k_win.shape[0]
    row0 = zero_ref[0]

    def in_window(rows):
        kpos = q0 - WINDOW + rows.start + lax.broadcasted_iota(jnp.int32, (rows.stop - rows.start, tq), 0)
        qpos = q0 + lax.broadcasted_iota(jnp.int32, (rows.stop - rows.start, tq), 1)
        return (jnp.abs(kpos - qpos) <= WINDOW) & (kpos >= 0) & (kpos < seq)

    for j in range(n_maps):
        _score_unit(lambda rows: kctx_ref[0, 0, rows, :], j, n_ctx, qpad_ref, cs_ref.at[j], cbm_ref.at[j])
    for j in range(n_maps):
        _score_unit(lambda rows: k_win[rows, :], j, n_win, qpad_ref, s_refs[j], bm_refs[j], valid=in_window)
    for j in range(n_maps):
        _softmax_unit(lambda cols: vtctx_ref[0, 0, j // 2, :, cols], j, n_ctx, cs_ref.at[j], cbm_ref.at[j],
                      m_ref, acc_ref, row0)
    for j in range(n_maps):
        _softmax_unit(lambda cols: vt_win[j // 2][:, cols], j, n_win, s_refs[j], bm_refs[j], m_ref, acc_ref, row0)
    _finish_plain(o_ref, acc_ref, 0, n_maps)


def _window_attention(q, k_ctx, vt_ctx, k_lat, vt_lat, sink):
    bsz, t, qw = q.shape
    key_tiles, _, kw = k_lat.shape[1:]
    assert key_tiles == 1
    tq = Q_TILE
    n_maps = 4
    k_flat = k_lat.reshape(bsz, t, kw)
    wb = tq // WINDOW
    nwb = t // WINDOW
    heads, cw = vt_lat.shape[2], vt_lat.shape[-1]
    whole = _whole_per_batch

    def vt_blk(width, blk):
        per_chunk = cw // width
        return pl.BlockSpec((1, 1, heads, V_ROWS, width),
                            lambda b, i: (b, blk(i) // per_chunk, 0, 0, blk(i) % per_chunk))

    def prev_blk(i):
        return jnp.maximum(i * wb - 1, 0)

    def next_blk(i):
        return jnp.minimum(i * wb + wb, nwb - 1)

    in_specs = [
        pl.BlockSpec((1, tq, qw), lambda b, i: (b, i, 0)),
        pl.BlockSpec((1, WINDOW, kw), lambda b, i: (b, prev_blk(i), 0)),
        pl.BlockSpec((1, tq, kw), lambda b, i: (b, i, 0)),
        pl.BlockSpec((1, WINDOW, kw), lambda b, i: (b, next_blk(i), 0)),
        vt_blk(WINDOW, prev_blk),
        vt_blk(tq, lambda i: i),
        vt_blk(WINDOW, next_blk),
        whole(k_ctx), whole(vt_ctx),
        pl.BlockSpec(memory_space=pltpu.SMEM),
        pl.BlockSpec(memory_space=pltpu.SMEM),
    ]
    assert n_maps == STAGES
    return pl.pallas_call(
        functools.partial(_window_attn_kernel, n_maps=n_maps, seq=t),
        grid=(bsz, t // tq),
        in_specs=in_specs,
        out_specs=pl.BlockSpec((1, tq, 256), lambda b, i: (b, i, 0)),
        out_shape=jax.ShapeDtypeStruct((bsz, t, 256), BF16),
        scratch_shapes=_attn_scratch(n_maps, tq, k_ctx.shape[2], tq + 2 * WINDOW),
        compiler_params=_params("parallel", "parallel"),
        name="window_attention",
    )(q, k_flat, k_flat, k_flat, vt_lat, vt_lat, vt_lat, k_ctx, vt_ctx, jnp.zeros((1,), jnp.int32), sink)


def _outproj_ffn_kernel(x_ref, mod_ref, ya_ref, g3_ref, g3p_ref, g3n_ref, yw_ref, yg_ref, cw_ref, w_ref, g_ref, b_ref,
                        wgu_ref, wdn_ref, g2_ref, b2_ref, o_ref, acc_ref, *, alpha):
    i = pl.program_id(1)
    last = pl.num_programs(1) - 1
    tm = x_ref.shape[1]
    gb = g3_ref[0, :, 0:256]
    hid = g3_ref[0, :, 256:512] * g3_ref[0, :, 512:768]
    halo = g3p_ref.shape[1]
    h_prev = g3p_ref[0, halo - 1:halo, 256:512] * g3p_ref[0, halo - 1:halo, 512:768]
    h_next = g3n_ref[0, 0:1, 256:512] * g3n_ref[0, 0:1, 512:768]
    h_prev = jnp.where(i == 0, 0.0, h_prev)
    h_next = jnp.where(i == last, 0.0, h_next)
    row = lax.broadcasted_iota(jnp.int32, hid.shape, 0)
    below = jnp.where(row == 0, h_prev, pltpu.roll(hid, 1, axis=0))
    above = jnp.where(row == tm - 1, h_next, pltpu.roll(hid, tm - 1, axis=0))
    yb = gb * (cw_ref[0:1, :] * below + cw_ref[1:2, :] * hid + cw_ref[2:3, :] * above)
    y = jnp.dot(ya_ref[0], w_ref[0:256, :], preferred_element_type=F32)
    y += jnp.dot(yb.astype(BF16), w_ref[256:512, :], preferred_element_type=F32)
    y += jnp.dot(yw_ref[0], w_ref[512:768, :], preferred_element_type=F32)
    y += jnp.dot(yg_ref[0], w_ref[768:1024, :], preferred_element_type=F32)
    gate = mod_ref[0, 5:6, :]
    x_mix = _layer_norm(alpha * x_ref[0] + gate * y, g_ref[...], b_ref[...])
    o_ref[0] = _ffn_body(x_mix, mod_ref, wgu_ref, wdn_ref, g2_ref, b2_ref, acc_ref, 6, alpha)


def _out_projection_ffn(x, mods, mod_row, ya, g3, yw, yg, conv_w, w_out, w_gu, w_dn, layer, ln_g, ln_b, ln_g2, ln_b2,
                        alpha):
    bsz, t, d = x.shape
    d_ff = w_dn.shape[1]
    tm = min(TOK_TILE, t)
    halo = 8
    hb = tm // halo
    n_halo = t // halo
    const = lambda b, i: (0, 0)
    y_spec = pl.BlockSpec((1, tm, 256), lambda b, i: (b, i, 0))
    return pl.pallas_call(
        functools.partial(_outproj_ffn_kernel, alpha=alpha),
        grid=(bsz, t // tm),
        in_specs=[
            pl.BlockSpec((1, tm, d), lambda b, i: (b, i, 0)),
            pl.BlockSpec((None, 1, N_MOD, d), lambda b, i: (layer, mod_row(b), 0, 0)),
            y_spec,
            pl.BlockSpec((1, tm, 768), lambda b, i: (b, i, 0)),
            pl.BlockSpec((1, halo, 768), lambda b, i: (b, jnp.maximum(i * hb - 1, 0), 0)),
            pl.BlockSpec((1, halo, 768), lambda b, i: (b, jnp.minimum(i * hb + hb, n_halo - 1), 0)),
            y_spec, y_spec,
            pl.BlockSpec(conv_w.shape, const),
            pl.BlockSpec((None,) + w_out.shape[1:], lambda b, i: (layer, 0, 0), pipeline_mode=pl.Buffered(1)),
            pl.BlockSpec((1, d), const),
            pl.BlockSpec((1, d), const),
            pl.BlockSpec((None, d, 2 * d_ff), lambda b, i: (layer, 0, 0), pipeline_mode=pl.Buffered(1)),
            pl.BlockSpec((None, d_ff, d), lambda b, i: (layer, 0, 0), pipeline_mode=pl.Buffered(1)),
            pl.BlockSpec((1, d), const),
            pl.BlockSpec((1, d), const),
        ],
        out_specs=pl.BlockSpec((1, tm, d), lambda b, i: (b, i, 0)),
        out_shape=jax.ShapeDtypeStruct(x.shape, F32),
        scratch_shapes=[pltpu.VMEM((tm, d), F32)],
        compiler_params=_params("parallel", "parallel"),
        name="out_projection_ffn",
    )(x, mods, ya, g3, g3, g3, yw, yg, conv_w, w_out, ln_g.reshape(1, d), ln_b.reshape(1, d), w_gu, w_dn,
      ln_g2.reshape(1, d), ln_b2.reshape(1, d))


def _rope_tables(rows, dim):
    n_freq = dim // 4
    inv_freq = ROPE_BASE ** (-jnp.arange(n_freq, dtype=F32) / n_freq)
    ang_r = jnp.arange(rows, dtype=F32)[:, None] * inv_freq
    ang_c = jnp.arange(GRID_W, dtype=F32)[:, None] * inv_freq

    def on_grid(of_row, of_col):
        r = jnp.broadcast_to(of_row[:, None, :], (rows, GRID_W, n_freq))
        c = jnp.broadcast_to(of_col[None, :, :], (rows, GRID_W, n_freq))
        return jnp.concatenate([r, r, c, c], axis=-1).reshape(rows * GRID_W, dim)

    cos = on_grid(jnp.cos(ang_r), jnp.cos(ang_c))
    sin = on_grid(jnp.sin(ang_r), jnp.sin(ang_c))
    reps = 128 // dim
    cos = jnp.tile(cos, (1, reps))
    sin = jnp.tile(sin, (1, reps))
    first_half = (jnp.arange(128) % (dim // 2)) < (dim // 4)
    sin_next = jnp.where(first_half, -sin, 0.0)
    sin_prev = jnp.where(first_half, 0.0, sin)
    return cos, sin_next, sin_prev


def _identity_tables(t):
    return jnp.ones((t, 128), F32), jnp.zeros((t, 128), F32), jnp.zeros((t, 128), F32)


def kernel(x, c, ctx, c_ctx, w_mod, b_mod, ln_g, ln_b, w_gu1, w_dn1, w_in, w_out, conv_w, lam_q1, lam_k1, lam_q2,
           lam_k2, subln_w, sink, qn_w, kn_w, w_gu2, w_dn2):
    bsz, seq, d = x.shape
    ctx_len = ctx.shape[1]
    depth = w_mod.shape[0]
    alpha = (2.0 * depth) ** 0.25
    assert seq % TOK_TILE == 0 and ctx_len % Q_TILE == 0 and bsz < MOD_ROWS

    tabs_a = _rope_tables(seq // GRID_W, A_QK_DIM)
    tabs_h = _rope_tables(seq // GRID_W, HEAD_DIM)
    tabs_id = _identity_tables(ctx_len)
    gsum = (jnp.arange(256)[:, None] // HEAD_DIM == jnp.arange(256)[None, :] // HEAD_DIM).astype(BF16)

    cvec = jnp.zeros((MOD_ROWS, d), F32).at[:bsz].set(c).at[bsz].set(c_ctx)
    mods_all = _mod_vectors(cvec, w_mod, b_mod).reshape(depth, MOD_ROWS, N_MOD, d)
    lat_row = lambda b: b
    ctx_row = lambda b: bsz

    wgu1, wdn1, wgu2, wdn2 = (w.astype(BF16) for w in (w_gu1, w_dn1, w_gu2, w_dn2))
    win, wout = w_in.astype(BF16), w_out.astype(BF16)

    xc = ctx
    for l in range(depth):
        need_ctx = l < depth - 1
        lambda_init = 0.8 - 0.6 * math.exp(-0.3 * l)
        mods = mods_all
        qnw = jnp.tile(qn_w[l], 256 // HEAD_DIM).reshape(1, 256)
        knw = jnp.tile(kn_w[l], 128 // HEAD_DIM).reshape(1, 128)
        lamp = jnp.stack([lam_q1[l], lam_k1[l], lam_q2[l], lam_k2[l]])
        subw = subln_w[l].reshape(HEAD_DIM, 1)

        x, qad, kad, vadt, g3, qc, kc, vct = _ffn_in_projection(
            x, mods, lat_row, wgu1, wdn1, ln_g[l, 0], ln_b[l, 0], alpha, win, l, tabs_a, tabs_h, qnw, knw, gsum)
        xc, qad_c, kad_c, vadt_c, g3_c, qc_c, kc_c, vct_c = _ffn_in_projection(
            xc, mods, ctx_row, wgu1, wdn1, ln_g[l, 0], ln_b[l, 0], alpha, win, l, tabs_id, tabs_id, qnw, knw, gsum)

        dense = (("diff", 8, A_QK_DIM), ("plain", 4, HEAD_DIM))
        ya, yg = _dense_attention(qad, kad_c, vadt_c, kad, vadt, segments=dense, lamp=lamp, subw=subw,
                                  diff_lambda_init=lambda_init)
        yw = _window_attention(qc, kc_c, vct_c, kc, vct, sink[l])
        x = _out_projection_ffn(x, mods, lat_row, ya, g3, yw, yg, conv_w[l], wout, wgu2, wdn2, l,
                                ln_g[l, 1], ln_b[l, 1], ln_g[l, 2], ln_b[l, 2], alpha)

        if need_ctx:
            ya_c, yg_c = _dense_attention(qad_c, kad_c, vadt_c, None, None, segments=dense, lamp=lamp, subw=subw,
                                          diff_lambda_init=lambda_init)
            yw_c, = _dense_attention(qc_c, kc_c, vct_c, None, None, segments=(("plain", 4, HEAD_DIM),),
                                     sink=sink[l])
            xc = _out_projection_ffn(xc, mods, ctx_row, ya_c, g3_c, yw_c, yg_c, conv_w[l], wout, wgu2, wdn2, l,
                                     ln_g[l, 1], ln_b[l, 1], ln_g[l, 2], ln_b[l, 2], alpha)
    return x
```

```python
import functools
import math

import jax
import jax.numpy as jnp
from jax import lax
from jax.experimental import pallas as pl
from jax.experimental.pallas import tpu as pltpu

F32 = jnp.float32
BF16 = jnp.bfloat16

GRID_W = 64
HEAD_DIM = 64
WINDOW = 128
ROPE_BASE = 10000.0
EPS = 1e-6
NEG = -1e30
A_QK_DIM = 32
N_MOD = 9
MOD_ROWS = 8
VMEM_LIMIT = 56 * 1024 * 1024

OFF_AQ, OFF_AK, OFF_AV = 0, 256, 512
OFF_B = 768
OFF_CQ, OFF_CK, OFF_CV = 1536, 1792, 1920
OFF_DQ, OFF_DK, OFF_DV = 2048, 2304, 2432
IN_W = 2560

TOK_TILE = 512
Q_TILE = 256
DENSE_Q_TILE = 256
FF_CHUNK = 256
SCORE_ROWS = 128
PV_ROWS = 256
ATTN_CHUNK = 1024
STAGES = 4
LOOKAHEAD = 2
ROUNDS_PER_ITER = 16
K_TILE = 128
ONES_ROWS = 16
V_ROWS = HEAD_DIM + ONES_ROWS
LOG2E = math.log2(math.e)


def _params(*sem):
    return pltpu.CompilerParams(dimension_semantics=sem, vmem_limit_bytes=VMEM_LIMIT)


def _layer_norm(r, g, b):
    mu = jnp.mean(r, axis=-1, keepdims=True)
    d = r - mu
    var = jnp.mean(d * d, axis=-1, keepdims=True)
    return d * lax.rsqrt(var + EPS) * g + b


def _mod_kernel(c_ref, w_ref, b_ref, o_ref):
    c = c_ref[...]
    a = (c * jax.nn.sigmoid(c)).astype(BF16)
    o_ref[0] = jnp.dot(a, w_ref[0].astype(BF16), preferred_element_type=F32) + b_ref[0]


def _mod_vectors(cvec, w_mod, b_mod):
    depth, d, n = w_mod.shape
    tn = n // 8
    return pl.pallas_call(
        _mod_kernel,
        grid=(depth, n // tn),
        in_specs=[
            pl.BlockSpec((MOD_ROWS, d), lambda l, j: (0, 0)),
            pl.BlockSpec((1, d, tn), lambda l, j: (l, 0, j)),
            pl.BlockSpec((1, 1, tn), lambda l, j: (l, 0, j)),
        ],
        out_specs=pl.BlockSpec((1, MOD_ROWS, tn), lambda l, j: (l, 0, j)),
        out_shape=jax.ShapeDtypeStruct((depth, MOD_ROWS, n), F32),
        compiler_params=_params("parallel", "parallel"),
        name="mod_vectors",
    )(cvec, w_mod, b_mod.reshape(depth, 1, n))


def _ffn_body(x, mod_ref, wgu_ref, wdn_ref, g_ref, b_ref, acc_ref, mod_base, alpha):
    shift = mod_ref[0, mod_base:mod_base + 1, :]
    scale = mod_ref[0, mod_base + 1:mod_base + 2, :]
    gate = mod_ref[0, mod_base + 2:mod_base + 3, :]
    h = (x * (1.0 + scale) + shift).astype(BF16)
    d_ff = wdn_ref.shape[0]
    for c in range(d_ff // FF_CHUNK):
        lo = c * FF_CHUNK
        g = jnp.dot(h, wgu_ref[:, lo:lo + FF_CHUNK], preferred_element_type=F32)
        u = jnp.dot(h, wgu_ref[:, d_ff + lo:d_ff + lo + FF_CHUNK], preferred_element_type=F32)
        acc_ref[:, lo:lo + FF_CHUNK] = (g * jax.nn.sigmoid(g) * u).astype(BF16)
    y = jnp.dot(acc_ref[...], wdn_ref[...], preferred_element_type=F32)
    r = alpha * x + (0.5 * gate) * y
    return _layer_norm(r, g_ref[...], b_ref[...])


def _rope(x, cos, sin_next, sin_prev, quarter):
    width = x.shape[1]
    reps = width // cos.shape[1]
    if reps > 1:
        cos = jnp.concatenate([cos] * reps, axis=1)
        sin_next = jnp.concatenate([sin_next] * reps, axis=1)
        sin_prev = jnp.concatenate([sin_prev] * reps, axis=1)
    x_next = pltpu.roll(x, width - quarter, axis=1)
    x_prev = pltpu.roll(x, quarter, axis=1)
    return x * cos + x_next * sin_next + x_prev * sin_prev


def _head_rms(x, gsum_ref, w):
    x2 = x * x
    hi = x2.astype(BF16)
    lo = (x2 - hi.astype(F32)).astype(BF16)
    gs = gsum_ref[:x.shape[1], :x.shape[1]]
    ss = jnp.dot(hi, gs, preferred_element_type=F32) + jnp.dot(lo, gs, preferred_element_type=F32)
    return x * lax.rsqrt(ss * (1.0 / HEAD_DIM) + EPS) * w


def _store_k(k_ref, k, tile0=0):
    for kt in range(k.shape[1] // K_TILE):
        k_ref[0, tile0 + kt] = k[:, kt * K_TILE:(kt + 1) * K_TILE].astype(BF16)


def _store_vt(vt_ref, v, head0=0):
    vt = v.T
    tm = v.shape[0]
    for hd in range(v.shape[1] // HEAD_DIM):
        vt_ref[0, 0, head0 + hd, 0:HEAD_DIM, :] = vt[hd * HEAD_DIM:(hd + 1) * HEAD_DIM, :].astype(BF16)
        vt_ref[0, 0, head0 + hd, HEAD_DIM:V_ROWS, :] = jnp.ones((ONES_ROWS, tm), BF16)


def _ffn_inproj_kernel(x_ref, mod_ref, wgu_ref, wdn_ref, g_ref, b_ref, w_ref,
                       ca_ref, san_ref, sap_ref, ch_ref, shn_ref, shp_ref, qnw_ref, knw_ref, gsum_ref,
                       x1_ref, qad_ref, kad_ref, vadt_ref, g3_ref, qc_ref, kc_ref, vct_ref,
                       acc_ref, *, alpha):
    x = _ffn_body(x_ref[0], mod_ref, wgu_ref, wdn_ref, g_ref, b_ref, acc_ref, 0, alpha)
    x1_ref[0] = x
    shift = mod_ref[0, 3:4, :]
    scale = mod_ref[0, 4:5, :]
    h = (x * (1.0 + scale) + shift).astype(BF16)

    def seg(lo, hi):
        return jnp.dot(h, w_ref[:, lo:hi], preferred_element_type=F32)

    rope_a = functools.partial(_rope, cos=ca_ref[...], sin_next=san_ref[...], sin_prev=sap_ref[...],
                               quarter=A_QK_DIM // 4)
    rope_h = functools.partial(_rope, cos=ch_ref[...], sin_next=shn_ref[...], sin_prev=shp_ref[...],
                               quarter=HEAD_DIM // 4)
    qad_ref[0, :, 0:256] = (rope_a(seg(OFF_AQ, OFF_AK)) * (LOG2E * A_QK_DIM ** -0.5)).astype(BF16)
    _store_k(kad_ref, rope_a(seg(OFF_AK, OFF_AV)))
    _store_vt(vadt_ref, seg(OFF_AV, OFF_B))
    qc_ref[0] = (rope_h(seg(OFF_CQ, OFF_CK)) * (LOG2E * HEAD_DIM ** -0.5)).astype(BF16)
    kv = seg(OFF_CK, OFF_DQ)
    _store_k(kc_ref, rope_h(kv[:, :OFF_CV - OFF_CK]))
    _store_vt(vct_ref, kv[:, OFF_CV - OFF_CK:])
    qd = _head_rms(seg(OFF_DQ, OFF_DK), gsum_ref, qnw_ref[...])
    qad_ref[0, :, 256:512] = (rope_h(qd) * (LOG2E * HEAD_DIM ** -0.5)).astype(BF16)
    kv = seg(OFF_DK, IN_W)
    kd = _head_rms(kv[:, :OFF_DV - OFF_DK], gsum_ref, knw_ref[...])
    _store_k(kad_ref, rope_h(kd), tile0=2)
    _store_vt(vadt_ref, kv[:, OFF_DV - OFF_DK:], head0=4)
    g3_ref[0] = seg(OFF_B, OFF_CQ)


def _ffn_in_projection(x, mods, mod_row, w_gu, w_dn, ln_g, ln_b, alpha, w_in, layer, tabs_a, tabs_h, qnw, knw, gsum):
    bsz, t, d = x.shape
    d_ff = w_dn.shape[1]
    tm = min(TOK_TILE, t)
    nc = t // tm
    const = lambda b, i: (0, 0)
    tab = pl.BlockSpec((tm, 128), lambda b, i: (i, 0))
    q_spec = pl.BlockSpec((1, tm, 256), lambda b, i: (b, i, 0))

    def k_spec(w):
        return pl.BlockSpec((1, w // K_TILE, tm, K_TILE), lambda b, i: (b, 0, i, 0))

    cw = min(ATTN_CHUNK, t)
    per_chunk = cw // tm

    def vt_spec(w):
        return pl.BlockSpec((1, 1, w // HEAD_DIM, V_ROWS, tm),
                            lambda b, i: (b, i // per_chunk, 0, 0, i % per_chunk))

    def q_shape():
        return jax.ShapeDtypeStruct((bsz, t, 256), BF16)

    def k_shape(w):
        return jax.ShapeDtypeStruct((bsz, w // K_TILE, t, K_TILE), BF16)

    def vt_shape(w):
        return jax.ShapeDtypeStruct((bsz, t // cw, w // HEAD_DIM, V_ROWS, cw), BF16)

    x_spec = pl.BlockSpec((1, tm, d), lambda b, i: (b, i, 0))
    return pl.pallas_call(
        functools.partial(_ffn_inproj_kernel, alpha=alpha),
        grid=(bsz, nc),
        in_specs=[
            x_spec,
            pl.BlockSpec((None, 1, N_MOD, d), lambda b, i: (layer, mod_row(b), 0, 0)),
            pl.BlockSpec((None, d, 2 * d_ff), lambda b, i: (layer, 0, 0), pipeline_mode=pl.Buffered(1)),
            pl.BlockSpec((None, d_ff, d), lambda b, i: (layer, 0, 0), pipeline_mode=pl.Buffered(1)),
            pl.BlockSpec((1, d), const),
            pl.BlockSpec((1, d), const),
            pl.BlockSpec((None, d, IN_W), lambda b, i: (layer, 0, 0), pipeline_mode=pl.Buffered(1)),
            tab, tab, tab, tab, tab, tab,
            pl.BlockSpec((1, 256), const),
            pl.BlockSpec((1, 128), const),
            pl.BlockSpec((256, 256), const),
        ],
        out_specs=[x_spec, pl.BlockSpec((1, tm, 512), lambda b, i: (b, i, 0)), k_spec(384), vt_spec(384),
                   pl.BlockSpec((1, tm, 768), lambda b, i: (b, i, 0)),
                   q_spec, k_spec(128), vt_spec(128)],
        out_shape=[jax.ShapeDtypeStruct(x.shape, F32), jax.ShapeDtypeStruct((bsz, t, 512), BF16),
                   k_shape(384), vt_shape(384),
                   jax.ShapeDtypeStruct((bsz, t, 768), F32),
                   q_shape(), k_shape(128), vt_shape(128)],
        scratch_shapes=[pltpu.VMEM((tm, d_ff), BF16)],
        compiler_params=_params("parallel", "parallel"),
        name="ffn_in_projection",
    )(x, mods, w_gu, w_dn, ln_g.reshape(1, d), ln_b.reshape(1, d), w_in, *tabs_a, *tabs_h, qnw, knw, gsum)


def _whole_per_batch(arr):
    zeros = (0,) * (arr.ndim - 1)
    return pl.BlockSpec((1,) + arr.shape[1:], lambda b, i: (b,) + zeros, pipeline_mode=pl.Buffered(1))


def _attn_scratch(n_maps, tq, ctx_rows, stage_rows):
    shapes = ([((n_maps, K_TILE, tq), BF16), ((n_maps, V_ROWS, tq), F32), ((n_maps, 8, tq), F32),
               ((n_maps, ctx_rows, tq), F32), ((n_maps, 8, tq), F32)]
              + [((stage_rows, tq), F32)] * STAGES + [((8, tq), F32)] * STAGES)
    return [pltpu.VMEM(shape, dtype) for shape, dtype in shapes]


MAPS_PER_KEY_TILE = 4
MAPS_PER_V_HEAD = 2


def _map_table(segments):
    table, q_col, key_col = [], 0, 0
    for kind, n_maps, dq in segments:
        grouped = kind == "plain"
        for j in range(n_maps):
            table.append((q_col + j * dq, dq, key_col + ((j // 2) * dq if grouped else j * dq)))
        q_col += n_maps * dq
        key_col += (n_maps // 2) * dq if grouped else n_maps * dq
        assert key_col % K_TILE == 0 and n_maps % MAPS_PER_KEY_TILE == 0
    assert all(kc // K_TILE == j // MAPS_PER_KEY_TILE for j, (_, _, kc) in enumerate(table))
    return tuple(table)


def _fill_qpad(q_ref, qpad_ref, table):
    tq = q_ref.shape[1]
    qt = q_ref[0].astype(F32).T
    for j, (q_col, dq, key_col) in enumerate(table):
        row = key_col % K_TILE
        parts = []
        if row:
            parts.append(jnp.zeros((row, tq), F32))
        parts.append(qt[q_col:q_col + dq, :])
        if K_TILE - row - dq:
            parts.append(jnp.zeros((K_TILE - row - dq, tq), F32))
        qpad_ref[j] = jnp.concatenate(parts, axis=0).astype(BF16)


def _score_unit(keys, j, nk, qpad_ref, s_ref, bm_ref, valid=None):
    tq = s_ref.shape[1]
    w = qpad_ref[j]
    bm = None
    for r in range(nk // SCORE_ROWS):
        rows = slice(r * SCORE_ROWS, (r + 1) * SCORE_ROWS)
        s = jnp.dot(keys(rows), w, preferred_element_type=F32)
        if valid is not None:
            s = jnp.where(valid(rows), s, NEG)
        s_ref[rows, :] = s
        part = jnp.max(s.reshape(SCORE_ROWS // 8, 8, tq), axis=0)
        bm = part if bm is None else jnp.maximum(bm, part)
    bm_ref[...] = bm


def _softmax_unit(values, j, nk, s_ref, bm_ref, m_ref, acc_ref, row0):
    m_old = m_ref[j, 0:1, :]
    m_new = jnp.maximum(m_old, jnp.max(bm_ref[...], axis=0, keepdims=True))
    alpha = jnp.exp2(m_old - m_new)
    pv = None
    for r in range(nk // PV_ROWS):
        rows = pl.ds(pl.multiple_of(row0 + r * PV_ROWS, PV_ROWS), PV_ROWS)
        p = jnp.exp2(s_ref[rows, :] - m_new)
        d = jnp.dot(values(slice(r * PV_ROWS, (r + 1) * PV_ROWS)), p.astype(BF16), preferred_element_type=F32)
        pv = d if pv is None else pv + d
    acc_ref[j] = alpha * acc_ref[j] + pv
    m_ref[j] = jnp.broadcast_to(m_new, m_ref.shape[1:])


def _init_state(m_ref, acc_ref, n_maps, sink_ref):
    tq = m_ref.shape[-1]
    for j in range(n_maps):
        if sink_ref is None:
            m_ref[j] = jnp.full(m_ref.shape[1:], NEG, F32)
            acc_ref[j] = jnp.zeros((V_ROWS, tq), F32)
        else:
            m_ref[j] = jnp.full(m_ref.shape[1:], sink_ref[j] * LOG2E, F32)
            acc_ref[j] = jnp.concatenate([jnp.zeros((HEAD_DIM, tq), F32), jnp.ones((ONES_ROWS, tq), F32)], axis=0)


def _normalised(acc_ref, j):
    acc = acc_ref[j]
    return acc[0:HEAD_DIM, :] / acc[HEAD_DIM:HEAD_DIM + 1, :]


def _finish_plain(o_ref, acc_ref, first, n_maps):
    ot = jnp.concatenate([_normalised(acc_ref, first + j) for j in range(n_maps)], axis=0)
    o_ref[0] = ot.T.astype(BF16)


def _finish_diff(o_ref, acc_ref, lamp_ref, subw_ref, first, n_maps, lambda_init):
    lp = lamp_ref[...]
    lam = (jnp.exp(jnp.sum(lp[0:1] * lp[1:2], axis=1, keepdims=True))
           - jnp.exp(jnp.sum(lp[2:3] * lp[3:4], axis=1, keepdims=True)) + lambda_init)
    outs = []
    for hd in range(n_maps // 2):
        o = (_normalised(acc_ref, first + 2 * hd)
             - lam * _normalised(acc_ref, first + 2 * hd + 1))
        ms = jnp.mean(o * o, axis=0, keepdims=True)
        outs.append(o * lax.rsqrt(ms + EPS) * subw_ref[...] * (1.0 - lambda_init))
    o_ref[0] = jnp.concatenate(outs, axis=0).T.astype(BF16)


def _dense_attn_kernel(*refs, segments, has_latent, has_sink, diff_lambda_init):
    refs = list(refs)
    q_ref, kc_ref, vct_ref, zero_ref = refs[:4]
    pos = 4
    k_ref = vt_ref = sink_ref = lamp_ref = subw_ref = None
    if has_latent:
        k_ref, vt_ref = refs[pos:pos + 2]
        pos += 2
    if has_sink:
        sink_ref = refs[pos]
        pos += 1
    if diff_lambda_init is not None:
        lamp_ref, subw_ref = refs[pos:pos + 2]
        pos += 2
    o_refs = refs[pos:pos + len(segments)]
    pos += len(segments)
    qpad_ref, acc_ref, m_ref, cs_ref, cbm_ref = refs[pos:pos + 5]
    s_refs = refs[pos + 5:pos + 5 + STAGES]
    bm_refs = refs[pos + 5 + STAGES:pos + 5 + 2 * STAGES]

    table = _map_table(segments)
    n_maps = len(table)
    _fill_qpad(q_ref, qpad_ref, table)
    _init_state(m_ref, acc_ref, n_maps, sink_ref)
    n_ctx = kc_ref.shape[2]
    row0 = zero_ref[0]

    for j in range(n_maps):
        _score_unit(lambda rows: kc_ref[0, j // MAPS_PER_KEY_TILE, rows, :], j, n_ctx, qpad_ref, cs_ref.at[j],
                    cbm_ref.at[j])

    if has_latent:
        tk = k_ref.shape[3]
        n_units = n_maps * k_ref.shape[2]

        def score(u, slot):
            j, c = u % n_maps, u // n_maps
            _score_unit(lambda rows: k_ref[0, j // MAPS_PER_KEY_TILE, c, rows, :], j, tk, qpad_ref, s_refs[slot],
                        bm_refs[slot])

        def softmax(u, slot):
            j, c = u % n_maps, u // n_maps
            _softmax_unit(lambda cols: vt_ref[0, c, j // MAPS_PER_V_HEAD, :, cols], j, tk, s_refs[slot],
                          bm_refs[slot], m_ref, acc_ref, row0)

        for u in range(LOOKAHEAD):
            score(u, u % STAGES)

    for j in range(n_maps):
        _softmax_unit(lambda cols: vct_ref[0, 0, j // MAPS_PER_V_HEAD, :, cols], j, n_ctx, cs_ref.at[j],
                      cbm_ref.at[j], m_ref, acc_ref, row0)

    if has_latent:
        per_iter = STAGES * math.gcd(n_units // STAGES, ROUNDS_PER_ITER)

        def body(i, carry):
            u0 = per_iter * i
            for k in range(per_iter):
                score(jnp.minimum(u0 + k + LOOKAHEAD, n_units - 1), (k + LOOKAHEAD) % STAGES)
                softmax(u0 + k, k % STAGES)
            return carry
        lax.fori_loop(0, n_units // per_iter, body, 0)
    first = 0
    for (kind, seg_maps, _), o_ref in zip(segments, o_refs):
        if kind == "diff":
            _finish_diff(o_ref, acc_ref, lamp_ref, subw_ref, first, seg_maps, diff_lambda_init)
        else:
            _finish_plain(o_ref, acc_ref, first, seg_maps)
        first += seg_maps


def _dense_attention(q, k_ctx, vt_ctx, k_lat, vt_lat, *, segments, sink=None, lamp=None, subw=None,
                     diff_lambda_init=None):
    bsz, t, qw = q.shape
    key_tiles, n_ctx = k_ctx.shape[1:3]
    n_maps = len(_map_table(segments))
    assert vt_ctx.shape[1] == 1 and vt_ctx.shape[-1] == n_ctx
    assert key_tiles * MAPS_PER_KEY_TILE == n_maps and vt_ctx.shape[2] * MAPS_PER_V_HEAD == n_maps
    tq = min(DENSE_Q_TILE, t)
    has_latent = k_lat is not None
    whole = _whole_per_batch
    assert n_maps % STAGES == 0 and n_ctx % PV_ROWS == 0

    in_specs = [pl.BlockSpec((1, tq, qw), lambda b, i: (b, i, 0)), whole(k_ctx), whole(vt_ctx),
                pl.BlockSpec(memory_space=pltpu.SMEM)]
    args = [q, k_ctx, vt_ctx, jnp.zeros((1,), jnp.int32)]
    stage_rows = 8
    if has_latent:
        n_chunks, tk = vt_lat.shape[1], vt_lat.shape[-1]
        assert tk % PV_ROWS == 0 and tk % SCORE_ROWS == 0 and n_chunks * tk == k_lat.shape[2]
        k_lat = k_lat.reshape(bsz, key_tiles, n_chunks, tk, K_TILE)
        in_specs += [whole(k_lat), whole(vt_lat)]
        args += [k_lat, vt_lat]
        stage_rows = tk
    if sink is not None:
        in_specs.append(pl.BlockSpec(memory_space=pltpu.SMEM))
        args.append(sink)
    if diff_lambda_init is not None:
        in_specs += [pl.BlockSpec(lamp.shape, lambda b, i: (0, 0)), pl.BlockSpec(subw.shape, lambda b, i: (0, 0))]
        args += [lamp, subw]
    scratch = _attn_scratch(n_maps, tq, n_ctx, stage_rows)
    return pl.pallas_call(
        functools.partial(_dense_attn_kernel, segments=segments, has_latent=has_latent,
                          has_sink=sink is not None, diff_lambda_init=diff_lambda_init),
        grid=(bsz, t // tq),
        in_specs=in_specs,
        out_specs=[pl.BlockSpec((1, tq, 256), lambda b, i: (b, i, 0))] * len(segments),
        out_shape=[jax.ShapeDtypeStruct((bsz, t, 256), BF16)] * len(segments),
        scratch_shapes=scratch,
        compiler_params=_params("parallel", "parallel"),
        name="dense_attention",
    )(*args)


def _window_attn_kernel(q_ref, kp_ref, kc_ref, kn_ref, vtp_ref, vtc_ref, vtn_ref, kctx_ref, vtctx_ref, zero_ref,
                        sink_ref, o_ref, qpad_ref, acc_ref, m_ref, cs_ref, cbm_ref, *stage_refs, n_maps, seq):
    s_refs, bm_refs = stage_refs[:STAGES], stage_refs[STAGES:]
    tq = q_ref.shape[1]
    q0 = pl.program_id(1) * tq
    _fill_qpad(q_ref, qpad_ref, _map_table((("plain", n_maps, HEAD_DIM),)))
    _init_state(m_ref, acc_ref, n_maps, sink_ref)
    k_win = jnp.concatenate([kp_ref[0], kc_ref[0], kn_ref[0]], axis=0)
    vt_win = jnp.concatenate([vtp_ref[0, 0], vtc_ref[0, 0], vtn_ref[0, 0]], axis=2)
    n_ctx, n_win = kctx_ref.shape[2], k_win.shape[0]
    row0 = zero_ref[0]

    def in_window(rows):
        kpos = q0 - WINDOW + rows.start + lax.broadcasted_iota(jnp.int32, (rows.stop - rows.start, tq), 0)
        qpos = q0 + lax.broadcasted_iota(jnp.int32, (rows.stop - rows.start, tq), 1)
        return (jnp.abs(kpos - qpos) <= WINDOW) & (kpos >= 0) & (kpos < seq)

    for j in range(n_maps):
        _score_unit(lambda rows: kctx_ref[0, 0, rows, :], j, n_ctx, qpad_ref, cs_ref.at[j], cbm_ref.at[j])
    for j in range(n_maps):
        _score_unit(lambda rows: k_win[rows, :], j, n_win, qpad_ref, s_refs[j], bm_refs[j], valid=in_window)
    for j in range(n_maps):
        _softmax_unit(lambda cols: vtctx_ref[0, 0, j // 2, :, cols], j, n_ctx, cs_ref.at[j], cbm_ref.at[j],
                      m_ref, acc_ref, row0)
    for j in range(n_maps):
        _softmax_unit(lambda cols: vt_win[j // 2][:, cols], j, n_win, s_refs[j], bm_refs[j], m_ref, acc_ref, row0)
    _finish_plain(o_ref, acc_ref, 0, n_maps)


def _window_attention(q, k_ctx, vt_ctx, k_lat, vt_lat, sink):
    bsz, t, qw = q.shape
    key_tiles, _, kw = k_lat.shape[1:]
    assert key_tiles == 1
    tq = Q_TILE
    n_maps = 4
    k_flat = k_lat.reshape(bsz, t, kw)
    wb = tq // WINDOW
    nwb = t // WINDOW
    heads, cw = vt_lat.shape[2], vt_lat.shape[-1]
    whole = _whole_per_batch

    def vt_blk(width, blk):
        per_chunk = cw // width
        return pl.BlockSpec((1, 1, heads, V_ROWS, width),
                            lambda b, i: (b, blk(i) // per_chunk, 0, 0, blk(i) % per_chunk))

    def prev_blk(i):
        return jnp.maximum(i * wb - 1, 0)

    def next_blk(i):
        return jnp.minimum(i * wb + wb, nwb - 1)

    in_specs = [
        pl.BlockSpec((1, tq, qw), lambda b, i: (b, i, 0)),
        pl.BlockSpec((1, WINDOW, kw), lambda b, i: (b, prev_blk(i), 0)),
        pl.BlockSpec((1, tq, kw), lambda b, i: (b, i, 0)),
        pl.BlockSpec((1, WINDOW, kw), lambda b, i: (b, next_blk(i), 0)),
        vt_blk(WINDOW, prev_blk),
        vt_blk(tq, lambda i: i),
        vt_blk(WINDOW, next_blk),
        whole(k_ctx), whole(vt_ctx),
        pl.BlockSpec(memory_space=pltpu.SMEM),
        pl.BlockSpec(memory_space=pltpu.SMEM),
    ]
    assert n_maps == STAGES
    return pl.pallas_call(
        functools.partial(_window_attn_kernel, n_maps=n_maps, seq=t),
        grid=(bsz, t // tq),
        in_specs=in_specs,
        out_specs=pl.BlockSpec((1, tq, 256), lambda b, i: (b, i, 0)),
        out_shape=jax.ShapeDtypeStruct((bsz, t, 256), BF16),
        scratch_shapes=_attn_scratch(n_maps, tq, k_ctx.shape[2], tq + 2 * WINDOW),
        compiler_params=_params("parallel", "parallel"),
        name="window_attention",
    )(q, k_flat, k_flat, k_flat, vt_lat, vt_lat, vt_lat, k_ctx, vt_ctx, jnp.zeros((1,), jnp.int32), sink)


def _outproj_ffn_kernel(x_ref, mod_ref, ya_ref, g3_ref, g3p_ref, g3n_ref, yw_ref, yg_ref, cw_ref, w_ref, g_ref, b_ref,
                        wgu_ref, wdn_ref, g2_ref, b2_ref, o_ref, acc_ref, *, alpha):
    i = pl.program_id(1)
    last = pl.num_programs(1) - 1
    tm = x_ref.shape[1]
    gb = g3_ref[0, :, 0:256]
    hid = g3_ref[0, :, 256:512] * g3_ref[0, :, 512:768]
    halo = g3p_ref.shape[1]
    h_prev = g3p_ref[0, halo - 1:halo, 256:512] * g3p_ref[0, halo - 1:halo, 512:768]
    h_next = g3n_ref[0, 0:1, 256:512] * g3n_ref[0, 0:1, 512:768]
    h_prev = jnp.where(i == 0, 0.0, h_prev)
    h_next = jnp.where(i == last, 0.0, h_next)
    row = lax.broadcasted_iota(jnp.int32, hid.shape, 0)
    below = jnp.where(row == 0, h_prev, pltpu.roll(hid, 1, axis=0))
    above = jnp.where(row == tm - 1, h_next, pltpu.roll(hid, tm - 1, axis=0))
    yb = gb * (cw_ref[0:1, :] * below + cw_ref[1:2, :] * hid + cw_ref[2:3, :] * above)
    y = jnp.dot(ya_ref[0], w_ref[0:256, :], preferred_element_type=F32)
    y += jnp.dot(yb.astype(BF16), w_ref[256:512, :], preferred_element_type=F32)
    y += jnp.dot(yw_ref[0], w_ref[512:768, :], preferred_element_type=F32)
    y += jnp.dot(yg_ref[0], w_ref[768:1024, :], preferred_element_type=F32)
    gate = mod_ref[0, 5:6, :]
    x_mix = _layer_norm(alpha * x_ref[0] + gate * y, g_ref[...], b_ref[...])
    o_ref[0] = _ffn_body(x_mix, mod_ref, wgu_ref, wdn_ref, g2_ref, b2_ref, acc_ref, 6, alpha)


def _out_projection_ffn(x, mods, mod_row, ya, g3, yw, yg, conv_w, w_out, w_gu, w_dn, layer, ln_g, ln_b, ln_g2, ln_b2,
                        alpha):
    bsz, t, d = x.shape
    d_ff = w_dn.shape[1]
    tm = min(TOK_TILE, t)
    halo = 8
    hb = tm // halo
    n_halo = t // halo
    const = lambda b, i: (0, 0)
    y_spec = pl.BlockSpec((1, tm, 256), lambda b, i: (b, i, 0))
    return pl.pallas_call(
        functools.partial(_outproj_ffn_kernel, alpha=alpha),
        grid=(bsz, t // tm),
        in_specs=[
            pl.BlockSpec((1, tm, d), lambda b, i: (b, i, 0)),
            pl.BlockSpec((None, 1, N_MOD, d), lambda b, i: (layer, mod_row(b), 0, 0)),
            y_spec,
            pl.BlockSpec((1, tm, 768), lambda b, i: (b, i, 0)),
            pl.BlockSpec((1, halo, 768), lambda b, i: (b, jnp.maximum(i * hb - 1, 0), 0)),
            pl.BlockSpec((1, halo, 768), lambda b, i: (b, jnp.minimum(i * hb + hb, n_halo - 1), 0)),
            y_spec, y_spec,
            pl.BlockSpec(conv_w.shape, const),
            pl.BlockSpec((None,) + w_out.shape[1:], lambda b, i: (layer, 0, 0), pipeline_mode=pl.Buffered(1)),
            pl.BlockSpec((1, d), const),
            pl.BlockSpec((1, d), const),
            pl.BlockSpec((None, d, 2 * d_ff), lambda b, i: (layer, 0, 0), pipeline_mode=pl.Buffered(1)),
            pl.BlockSpec((None, d_ff, d), lambda b, i: (layer, 0, 0), pipeline_mode=pl.Buffered(1)),
            pl.BlockSpec((1, d), const),
            pl.BlockSpec((1, d), const),
        ],
        out_specs=pl.BlockSpec((1, tm, d), lambda b, i: (b, i, 0)),
        out_shape=jax.ShapeDtypeStruct(x.shape, F32),
        scratch_shapes=[pltpu.VMEM((tm, d_ff), BF16)],
        compiler_params=_params("parallel", "parallel"),
        name="out_projection_ffn",
    )(x, mods, ya, g3, g3, g3, yw, yg, conv_w, w_out, ln_g.reshape(1, d), ln_b.reshape(1, d), w_gu, w_dn,
      ln_g2.reshape(1, d), ln_b2.reshape(1, d))


def _rope_tables(rows, dim):
    n_freq = dim // 4
    inv_freq = ROPE_BASE ** (-jnp.arange(n_freq, dtype=F32) / n_freq)
    ang_r = jnp.arange(rows, dtype=F32)[:, None] * inv_freq
    ang_c = jnp.arange(GRID_W, dtype=F32)[:, None] * inv_freq

    def on_grid(of_row, of_col):
        r = jnp.broadcast_to(of_row[:, None, :], (rows, GRID_W, n_freq))
        c = jnp.broadcast_to(of_col[None, :, :], (rows, GRID_W, n_freq))
        return jnp.concatenate([r, r, c, c], axis=-1).reshape(rows * GRID_W, dim)

    cos = on_grid(jnp.cos(ang_r), jnp.cos(ang_c))
    sin = on_grid(jnp.sin(ang_r), jnp.sin(ang_c))
    reps = 128 // dim
    cos = jnp.tile(cos, (1, reps))
    sin = jnp.tile(sin, (1, reps))
    first_half = (jnp.arange(128) % (dim // 2)) < (dim // 4)
    sin_next = jnp.where(first_half, -sin, 0.0)
    sin_prev = jnp.where(first_half, 0.0, sin)
    return cos, sin_next, sin_prev


def _identity_tables(t):
    return jnp.ones((t, 128), F32), jnp.zeros((t, 128), F32), jnp.zeros((t, 128), F32)


def kernel(x, c, ctx, c_ctx, w_mod, b_mod, ln_g, ln_b, w_gu1, w_dn1, w_in, w_out, conv_w, lam_q1, lam_k1, lam_q2,
           lam_k2, subln_w, sink, qn_w, kn_w, w_gu2, w_dn2):
    bsz, seq, d = x.shape
    ctx_len = ctx.shape[1]
    depth = w_mod.shape[0]
    alpha = (2.0 * depth) ** 0.25
    assert seq % TOK_TILE == 0 and ctx_len % Q_TILE == 0 and bsz < MOD_ROWS

    tabs_a = _rope_tables(seq // GRID_W, A_QK_DIM)
    tabs_h = _rope_tables(seq // GRID_W, HEAD_DIM)
    tabs_id = _identity_tables(ctx_len)
    gsum = (jnp.arange(256)[:, None] // HEAD_DIM == jnp.arange(256)[None, :] // HEAD_DIM).astype(BF16)

    cvec = jnp.zeros((MOD_ROWS, d), F32).at[:bsz].set(c).at[bsz].set(c_ctx)
    mods_all = _mod_vectors(cvec, w_mod, b_mod).reshape(depth, MOD_ROWS, N_MOD, d)
    lat_row = lambda b: b
    ctx_row = lambda b: bsz

    wgu1, wdn1, wgu2, wdn2 = (w.astype(BF16) for w in (w_gu1, w_dn1, w_gu2, w_dn2))
    win, wout = w_in.astype(BF16), w_out.astype(BF16)

    xc = ctx
    for l in range(depth):
        need_ctx = l < depth - 1
        lambda_init = 0.8 - 0.6 * math.exp(-0.3 * l)
        mods = mods_all
        qnw = jnp.tile(qn_w[l], 256 // HEAD_DIM).reshape(1, 256)
        knw = jnp.tile(kn_w[l], 128 // HEAD_DIM).reshape(1, 128)
        lamp = jnp.stack([lam_q1[l], lam_k1[l], lam_q2[l], lam_k2[l]])
        subw = subln_w[l].reshape(HEAD_DIM, 1)

        x, qad, kad, vadt, g3, qc, kc, vct = _ffn_in_projection(
            x, mods, lat_row, wgu1, wdn1, ln_g[l, 0], ln_b[l, 0], alpha, win, l, tabs_a, tabs_h, qnw, knw, gsum)
        xc, qad_c, kad_c, vadt_c, g3_c, qc_c, kc_c, vct_c = _ffn_in_projection(
            xc, mods, ctx_row, wgu1, wdn1, ln_g[l, 0], ln_b[l, 0], alpha, win, l, tabs_id, tabs_id, qnw, knw, gsum)

        dense = (("diff", 8, A_QK_DIM), ("plain", 4, HEAD_DIM))
        ya, yg = _dense_attention(qad, kad_c, vadt_c, kad, vadt, segments=dense, lamp=lamp, subw=subw,
                                  diff_lambda_init=lambda_init)
        yw = _window_attention(qc, kc_c, vct_c, kc, vct, sink[l])
        x = _out_projection_ffn(x, mods, lat_row, ya, g3, yw, yg, conv_w[l], wout, wgu2, wdn2, l,
                                ln_g[l, 1], ln_b[l, 1], ln_g[l, 2], ln_b[l, 2], alpha)

        if need_ctx:
            ya_c, yg_c = _dense_attention(qad_c, kad_c, vadt_c, None, None, segments=dense, lamp=lamp, subw=subw,
                                          diff_lambda_init=lambda_init)
            yw_c, = _dense_attention(qc_c, kc_c, vct_c, None, None, segments=(("plain", 4, HEAD_DIM),),
                                     sink=sink[l])
            xc = _out_projection_ffn(xc, mods, ctx_row, ya_c, g3_c, yw_c, yg_c, conv_w[l], wout, wgu2, wdn2, l,
                                     ln_g[l, 1], ln_b[l, 1], ln_g[l, 2], ln_b[l, 2], alpha)
    return x
```
